```python
import jax, jax.numpy as jnp
from jax import lax
import numpy as np

D_MODEL = 2048
BATCH = 2
SEQ = 4096
DEPTH = 2
DEC_BATCH = 128
DEC_SEQ = 1
PAST_LEN = 2048
PAGE_SIZE = 128

HEAD_DIM = 128
RET_W = D_MODEL // 4
POOL_W = D_MODEL // 4
NSA_W = D_MODEL - RET_W - POOL_W
D_MIX = RET_W + POOL_W + NSA_W
R_HEADS = RET_W // HEAD_DIM
N_HEADS = NSA_W // HEAD_DIM
N_KV = max(1, N_HEADS // 4)
HPG = N_HEADS // N_KV
KV_W = N_KV * HEAD_DIM
POOL_WINDOWS = (2, 4, 8, 16)
POOL_GROUPS = len(POOL_WINDOWS)
POOL_GW = POOL_W // POOL_GROUPS
POOL_BUF = max(POOL_WINDOWS) - 1
RET_CHUNK = 128
CMP_LEN = 32
CMP_STRIDE = 16
CMP_RATIO = CMP_LEN // CMP_STRIDE
SEL_LEN = 64
SEL_RATIO = SEL_LEN // CMP_STRIDE
SEL_TOPK = 16
N_LOCAL_SEL = 2
WINDOW = 512
Q_BLOCK = 128
D_FF = 4 * D_MODEL
ROPE_THETA = 10000.0
EPS = 1e-6
D_IN = 4 * RET_W + POOL_W + NSA_W + 6 * KV_W + 3 * N_HEADS

kernel_name = 'hybrid_retention_pool_nsa_decoder_step'


def rms_norm(x, g):
    xf = x.astype(jnp.float32)
    y = xf * lax.rsqrt(jnp.mean(xf * xf, axis=-1, keepdims=True) + EPS)
    return (y * g.astype(jnp.float32)).astype(x.dtype)


def rope(x, pos):
    half = x.shape[-1] // 2
    inv = ROPE_THETA ** (-jnp.arange(half, dtype=jnp.float32) / half)
    ang = pos.astype(jnp.float32)[:, None] * inv[None, :]
    cos = jnp.cos(ang)[None, :, None, :]
    sin = jnp.sin(ang)[None, :, None, :]
    xf = x.astype(jnp.float32)
    x1, x2 = xf[..., :half], xf[..., half:]
    return jnp.concatenate([x1 * cos - x2 * sin, x2 * cos + x1 * sin], axis=-1).astype(x.dtype)


def masked_softmax(s, mask):
    s = jnp.where(mask, s, -jnp.inf)
    m = jnp.max(s, axis=-1, keepdims=True)
    m = jnp.where(jnp.isfinite(m), m, 0.0)
    p = jnp.exp(s - m)
    d = jnp.sum(p, axis=-1, keepdims=True)
    return p / jnp.where(d > 0, d, 1.0)


def project(hn, w_in, pos):
    B, T, _ = hn.shape
    z = hn @ w_in
    sizes = (RET_W, RET_W, RET_W, RET_W, POOL_W, NSA_W) + (KV_W,) * 6 + (3 * N_HEADS,)
    rq, rk, rv, rg, pu, nq, kc, vc, ks, vs, kw, vw, gl = jnp.split(z, np.cumsum(sizes)[:-1].tolist(), axis=-1)
    hd = lambda a: a.reshape(B, T, -1, HEAD_DIM)
    rq = rope(hd(rq), pos)
    rk = rope(hd(rk), pos) * (HEAD_DIM ** -0.5)
    nq = rope(hd(nq), pos)
    kc, ks, kw = rope(hd(kc), pos), rope(hd(ks), pos), rope(hd(kw), pos)
    gates = jax.nn.sigmoid(gl.astype(jnp.float32)).reshape(B, T, N_HEADS, 3)
    return rq, rk, hd(rv), rg, pu, nq, kc, hd(vc), ks, hd(vs), kw, hd(vw), gates


def retention_chunk(S, q, k, v, log_g):
    L = q.shape[1]
    idx = jnp.arange(L, dtype=jnp.float32)
    diff = idx[:, None] - idx[None, :]
    causal = diff >= 0
    decay = jnp.where(causal[None], jnp.exp(jnp.where(causal, diff, 0.0)[None] * log_g[:, None, None]), 0.0)
    att = jnp.einsum('blhd,bmhd->bhlm', q, k) * decay[None]
    o = jnp.einsum('bhlm,bmhe->blhe', att, v)
    q_dec = jnp.exp((idx + 1.0)[:, None] * log_g[None, :])
    o = o + jnp.einsum('blhd,bhde->blhe', q * q_dec[None, :, :, None], S)
    k_dec = jnp.exp((L - 1.0 - idx)[:, None] * log_g[None, :])
    S = jnp.exp(L * log_g)[None, :, None, None] * S + jnp.einsum('blhd,blhe->bhde', k * k_dec[None, :, :, None], v)
    return S, o


def retention(q, k, v, S0, chunk):
    B, T, H, _ = q.shape
    n = T // chunk
    log_g = jnp.log1p(-jnp.exp2(-5.0 - jnp.arange(H, dtype=jnp.float32)))
    to_chunks = lambda a: a.astype(jnp.float32).reshape(B, n, chunk, H, a.shape[-1]).swapaxes(0, 1)
    step = lambda S, xs: retention_chunk(S, xs[0], xs[1], xs[2], log_g)
    S, o = lax.scan(step, S0.astype(jnp.float32), (to_chunks(q), to_chunks(k), to_chunks(v)))
    return S, o.swapaxes(0, 1).reshape(B, T, H, v.shape[-1])


def retention_out(o, rg):
    B, T = o.shape[:2]
    o = o * lax.rsqrt(jnp.mean(o * o, axis=-1, keepdims=True) + EPS)
    return (o.reshape(B, T, RET_W) * jax.nn.silu(rg.astype(jnp.float32))).astype(rg.dtype)


def pool_mix(u, buf, pos0, w_pool, pool_scale):
    B, T, _ = u.shape
    ext = jnp.concatenate([jnp.zeros((B, 1, POOL_W), u.dtype), buf, u], axis=1).astype(jnp.float32)
    cs = jnp.cumsum(ext, axis=1)
    base = 1 + POOL_BUF
    pos = pos0 + jnp.arange(T)
    uf = u.astype(jnp.float32)
    outs = []
    for g, w in enumerate(POOL_WINDOWS):
        sl = slice(g * POOL_GW, (g + 1) * POOL_GW)
        win_sum = cs[:, base:base + T, sl] - cs[:, base - w:base - w + T, sl]
        cnt = jnp.minimum(pos + 1, w).astype(jnp.float32)
        outs.append(win_sum / cnt[None, :, None] - uf[..., sl])
    p = jnp.stack(outs, axis=2)
    y = jnp.einsum('btgc,gcd->btgd', p, w_pool.astype(jnp.float32)).reshape(B, T, POOL_W)
    y = y * pool_scale.astype(jnp.float32)
    new_buf = jnp.concatenate([buf, u], axis=1)[:, -POOL_BUF:]
    return y, new_buf


def compress(rows, w1, w2, pe):
    B, Tk, G, Dh = rows.shape
    n_cmp = (Tk - CMP_LEN) // CMP_STRIDE + 1
    n_chunk = n_cmp + CMP_RATIO - 1
    chunks = rows[:, :n_chunk * CMP_STRIDE].astype(jnp.float32).reshape(B, n_chunk, CMP_STRIDE, G, Dh)
    w1f = w1.astype(jnp.float32)
    w1r = w1f.reshape(CMP_RATIO, CMP_STRIDE, Dh, Dh)
    h = jnp.einsum('ld,lde->e', pe.astype(jnp.float32), w1f)
    for r in range(CMP_RATIO):
        h = h + jnp.einsum('bnsgd,sde->bnge', chunks[:, r:r + n_cmp], w1r[r])
    summ = jnp.einsum('bnge,ef->bngf', jax.nn.silu(h), w2.astype(jnp.float32))
    end_pos = jnp.arange(n_cmp) * CMP_STRIDE + CMP_LEN - 1
    return summ, end_pos


def sel_blocks(rows):
    B, Tk, G, Dh = rows.shape
    n_sel = -(-Tk // SEL_LEN)
    r = jnp.pad(rows.astype(jnp.float32), ((0, 0), (0, n_sel * SEL_LEN - Tk), (0, 0), (0, 0)))
    return r.reshape(B, n_sel, SEL_LEN, G, Dh).transpose(0, 3, 1, 2, 4)


def nsa_queries(q, q_pos, kc_s, vc_s, c_end, ks_b, vs_b, kw, vw, w_pos, gates):
    B, Tq = q.shape[:2]
    scale = HEAD_DIM ** -0.5
    qg = q.astype(jnp.float32).reshape(B, Tq, N_KV, HPG, HEAD_DIM)
    s_c = jnp.einsum('btgqd,bngd->btgqn', qg, kc_s) * scale
    mask_c = (c_end[None, :] <= q_pos[:, None])[None, :, None, None, :]
    p_c = masked_softmax(s_c, mask_c)
    o_c = jnp.einsum('btgqn,bngd->btgqd', p_c, vc_s)
    imp = jnp.sum(p_c, axis=3)
    n_c = kc_s.shape[1]
    n_sel = ks_b.shape[2]
    offs = np.array([m - n for m in range(SEL_RATIO) for n in range(CMP_RATIO)])
    idx = SEL_RATIO * np.arange(n_sel)[:, None] + offs[None, :]
    valid = ((idx >= 0) & (idx < n_c)).astype(np.float32)
    p_slc = jnp.sum(jnp.take(imp, np.clip(idx, 0, n_c - 1), axis=-1) * valid, axis=-1)
    jb = jnp.arange(n_sel)
    blk_valid = jb[None, :] * SEL_LEN <= q_pos[:, None]
    cur = q_pos // SEL_LEN
    forced = (jb[None, :] == 0) | ((jb[None, :] <= cur[:, None]) & (jb[None, :] > cur[:, None] - N_LOCAL_SEL))
    score = jnp.where(blk_valid[None, :, None, :], jnp.where(forced[None, :, None, :], jnp.inf, p_slc), -jnp.inf)
    top_s, top_i = lax.top_k(score, min(SEL_TOPK, n_sel))
    kk = top_i.shape[-1]
    bi = jnp.arange(B)[:, None, None, None]
    gi = jnp.arange(N_KV)[None, None, :, None]
    k_sel = ks_b[bi, gi, top_i]
    v_sel = vs_b[bi, gi, top_i]
    s_s = jnp.einsum('btgqd,btgksd->btgqks', qg, k_sel).reshape(B, Tq, N_KV, HPG, kk * SEL_LEN) * scale
    tok = top_i[..., None] * SEL_LEN + jnp.arange(SEL_LEN)
    mask_s = (top_s > -jnp.inf)[..., None] & (tok <= q_pos[None, :, None, None, None])
    p_s = masked_softmax(s_s, mask_s.reshape(B, Tq, N_KV, 1, kk * SEL_LEN))
    o_s = jnp.einsum('btgqm,btgmd->btgqd', p_s, v_sel.reshape(B, Tq, N_KV, kk * SEL_LEN, HEAD_DIM))
    s_w = jnp.einsum('btgqd,bsgd->btgqs', qg, kw.astype(jnp.float32)) * scale
    dpos = q_pos[:, None] - w_pos[None, :]
    mask_w = ((dpos >= 0) & (dpos < WINDOW) & (w_pos[None, :] >= 0))[None, :, None, None, :]
    p_w = masked_softmax(s_w, mask_w)
    o_w = jnp.einsum('btgqs,bsgd->btgqd', p_w, vw.astype(jnp.float32))
    g = gates.reshape(B, Tq, N_KV, HPG, 3)
    o = g[..., 0:1] * o_c + g[..., 1:2] * o_s + g[..., 2:3] * o_w
    return o.reshape(B, Tq, NSA_W)


def nsa_prompt(q, kc, vc, ks, vs, kw, vw, gates, w1k, w2k, pek, w1v, w2v, pev):
    B, T = q.shape[:2]
    kc_s, c_end = compress(kc, w1k, w2k, pek)
    vc_s, _ = compress(vc, w1v, w2v, pev)
    ks_b, vs_b = sel_blocks(ks), sel_blocks(vs)
    zpad = jnp.zeros((B, WINDOW, N_KV, HEAD_DIM), kw.dtype)
    kw_p = jnp.concatenate([zpad, kw], axis=1)
    vw_p = jnp.concatenate([zpad, vw], axis=1)
    nb = T // Q_BLOCK
    qb = q.reshape(B, nb, Q_BLOCK, N_HEADS, HEAD_DIM).swapaxes(0, 1)
    gb = gates.reshape(B, nb, Q_BLOCK, N_HEADS, 3).swapaxes(0, 1)

    def body(args):
        i, q_i, g_i = args
        start = i * Q_BLOCK
        q_pos = start + jnp.arange(Q_BLOCK)
        kw_i = lax.dynamic_slice_in_dim(kw_p, start, WINDOW + Q_BLOCK, axis=1)
        vw_i = lax.dynamic_slice_in_dim(vw_p, start, WINDOW + Q_BLOCK, axis=1)
        w_pos = start - WINDOW + jnp.arange(WINDOW + Q_BLOCK)
        return nsa_queries(q_i, q_pos, kc_s, vc_s, c_end, ks_b, vs_b, kw_i, vw_i, w_pos, g_i)

    o = lax.map(body, (jnp.arange(nb), qb, gb))
    return o.swapaxes(0, 1).reshape(B, T, NSA_W)


def gather_pages(pool, page_table):
    g = pool[page_table]
    return g.reshape(page_table.shape[0], -1, g.shape[-2], g.shape[-1])


def nsa_sample(q, kc, vc, ks, vs, kw, vw, gates, pool_ck, pool_cv, pool_sk, pool_sv, win_k, win_v,
               page_table, w1k, w2k, pek, w1v, w2v, pev):
    B, T = q.shape[:2]
    past = page_table.shape[1] * PAGE_SIZE
    kc_full = jnp.concatenate([gather_pages(pool_ck, page_table), kc], axis=1)
    vc_full = jnp.concatenate([gather_pages(pool_cv, page_table), vc], axis=1)
    ks_full = jnp.concatenate([gather_pages(pool_sk, page_table), ks], axis=1)
    vs_full = jnp.concatenate([gather_pages(pool_sv, page_table), vs], axis=1)
    kc_s, c_end = compress(kc_full, w1k, w2k, pek)
    vc_s, _ = compress(vc_full, w1v, w2v, pev)
    ks_b, vs_b = sel_blocks(ks_full), sel_blocks(vs_full)
    w_buf = win_k.shape[1]
    kw_band = jnp.concatenate([win_k, kw], axis=1)
    vw_band = jnp.concatenate([win_v, vw], axis=1)
    w_pos = past - w_buf + jnp.arange(w_buf + T)
    q_pos = past + jnp.arange(T)
    o = nsa_queries(q, q_pos, kc_s, vc_s, c_end, ks_b, vs_b, kw_band, vw_band, w_pos, gates)
    return o, kw_band[:, -w_buf:], vw_band[:, -w_buf:]


def mix_out(o_r, rg, o_p, o_n, w_out):
    dt = rg.dtype
    y = jnp.concatenate([retention_out(o_r, rg), o_p.astype(dt), o_n.astype(dt)], axis=-1)
    return y @ w_out


def sq_relu_mlp(h, w_up, w_down):
    u = jax.nn.relu(h @ w_up)
    return (u * u) @ w_down


def setup_inputs(seed: int = 0) -> dict:
    key = jax.random.key(seed)
    ks = jax.random.split(key, 27)
    f32 = jnp.float32
    n_pages = PAST_LEN // PAGE_SIZE
    n_used = DEC_BATCH * n_pages
    n_pool = n_used + max(1, n_used // 4)
    nrm = lambda k, shape, scale: jax.random.normal(k, shape, f32) * scale
    page_table = jax.random.permutation(ks[0], n_pool)[:n_used].reshape(DEC_BATCH, n_pages).astype(jnp.int32)
    kv_pool = (DEPTH, n_pool, PAGE_SIZE, N_KV, HEAD_DIM)
    w_buf = min(WINDOW, PAST_LEN)
    return {
        'x_prompt': nrm(ks[1], (BATCH, SEQ, D_MODEL), 1.0),
        'x_sample': nrm(ks[2], (DEC_BATCH, DEC_SEQ, D_MODEL), 1.0),
        'state_ret': nrm(ks[3], (DEPTH, DEC_BATCH, R_HEADS, HEAD_DIM, HEAD_DIM), 0.5),
        'state_pool': nrm(ks[4], (DEPTH, DEC_BATCH, POOL_BUF, POOL_W), 1.0),
        'cache_cmp_k': nrm(ks[5], kv_pool, 1.0),
        'cache_cmp_v': nrm(ks[6], kv_pool, 1.0),
        'cache_slc_k': nrm(ks[7], kv_pool, 1.0),
        'cache_slc_v': nrm(ks[8], kv_pool, 1.0),
        'cache_win_k': nrm(ks[9], (DEPTH, DEC_BATCH, w_buf, N_KV, HEAD_DIM), 1.0),
        'cache_win_v': nrm(ks[10], (DEPTH, DEC_BATCH, w_buf, N_KV, HEAD_DIM), 1.0),
        'page_table': page_table,
        'norm1': 1.0 + nrm(ks[11], (DEPTH, D_MODEL), 0.05),
        'w_in': nrm(ks[12], (DEPTH, D_MODEL, D_IN), D_MODEL ** -0.5),
        'w_pool': nrm(ks[13], (DEPTH, POOL_GROUPS, POOL_GW, POOL_GW), POOL_GW ** -0.5),
        'pool_scale': 1.0 + nrm(ks[14], (DEPTH, POOL_W), 0.05),
        'cmp_w1_k': nrm(ks[15], (DEPTH, CMP_LEN, HEAD_DIM, HEAD_DIM), (CMP_LEN * HEAD_DIM) ** -0.5),
        'cmp_w2_k': nrm(ks[16], (DEPTH, HEAD_DIM, HEAD_DIM), HEAD_DIM ** -0.5),
        'cmp_pe_k': nrm(ks[17], (DEPTH, CMP_LEN, HEAD_DIM), 0.1),
        'cmp_w1_v': nrm(ks[18], (DEPTH, CMP_LEN, HEAD_DIM, HEAD_DIM), (CMP_LEN * HEAD_DIM) ** -0.5),
        'cmp_w2_v': nrm(ks[19], (DEPTH, HEAD_DIM, HEAD_DIM), HEAD_DIM ** -0.5),
        'cmp_pe_v': nrm(ks[20], (DEPTH, CMP_LEN, HEAD_DIM), 0.1),
        'w_out': nrm(ks[21], (DEPTH, D_MIX, D_MODEL), D_MIX ** -0.5),
        'norm2': 1.0 + nrm(ks[22], (DEPTH, D_MODEL), 0.05),
        'w_up': nrm(ks[23], (DEPTH, D_MODEL, D_FF), D_MODEL ** -0.5),
        'w_down': nrm(ks[24], (DEPTH, D_FF, D_MODEL), D_FF ** -0.5),
        'norm_f': 1.0 + nrm(ks[25], (D_MODEL,), 0.05),
    }


def reference(x_prompt, x_sample, state_ret, state_pool, cache_cmp_k, cache_cmp_v, cache_slc_k, cache_slc_v,
              cache_win_k, cache_win_v, page_table, norm1, w_in, w_pool, pool_scale, cmp_w1_k, cmp_w2_k, cmp_pe_k,
              cmp_w1_v, cmp_w2_v, cmp_pe_v, w_out, norm2, w_up, w_down, norm_f):
    past_len = page_table.shape[1] * PAGE_SIZE
    xp, xs = x_prompt, x_sample
    bp, tp, _ = xp.shape
    bs, ts, _ = xs.shape
    pos_p = jnp.arange(tp)
    pos_s = past_len + jnp.arange(ts)
    wlen = min(WINDOW, tp)
    ret_p, ret_s, pool_p, pool_s = [], [], [], []
    ck_p, ck_s, cv_p, cv_s, sk_p, sk_s, sv_p, sv_s = [], [], [], [], [], [], [], []
    wk_p, wk_s, wv_p, wv_s = [], [], [], []
    for l in range(DEPTH):
        phi = (cmp_w1_k[l], cmp_w2_k[l], cmp_pe_k[l], cmp_w1_v[l], cmp_w2_v[l], cmp_pe_v[l])
        hn = rms_norm(xp, norm1[l])
        rq, rk, rv, rg, pu, nq, kc, vc, ks, vs, kw, vw, gt = project(hn, w_in[l], pos_p)
        s_new, o_r = retention(rq, rk, rv, jnp.zeros((bp, R_HEADS, HEAD_DIM, HEAD_DIM), jnp.float32),
                               min(RET_CHUNK, tp))
        o_p, buf_new = pool_mix(pu, jnp.zeros((bp, POOL_BUF, POOL_W), pu.dtype), 0, w_pool[l], pool_scale[l])
        o_n = nsa_prompt(nq, kc, vc, ks, vs, kw, vw, gt, *phi)
        xp = xp + mix_out(o_r, rg, o_p, o_n, w_out[l])
        xp = xp + sq_relu_mlp(rms_norm(xp, norm2[l]), w_up[l], w_down[l])
        ret_p.append(s_new)
        pool_p.append(buf_new)
        ck_p.append(kc)
        cv_p.append(vc)
        sk_p.append(ks)
        sv_p.append(vs)
        wk_p.append(kw[:, -wlen:])
        wv_p.append(vw[:, -wlen:])
        hn = rms_norm(xs, norm1[l])
        rq, rk, rv, rg, pu, nq, kc, vc, ks, vs, kw, vw, gt = project(hn, w_in[l], pos_s)
        s_new, o_r = retention(rq, rk, rv, state_ret[l], ts)
        o_p, buf_new = pool_mix(pu, state_pool[l], past_len, w_pool[l], pool_scale[l])
        o_n, nwk, nwv = nsa_sample(nq, kc, vc, ks, vs, kw, vw, gt, cache_cmp_k[l], cache_cmp_v[l],
                                   cache_slc_k[l], cache_slc_v[l], cache_win_k[l], cache_win_v[l],
                                   page_table, *phi)
        xs = xs + mix_out(o_r, rg, o_p, o_n, w_out[l])
        xs = xs + sq_relu_mlp(rms_norm(xs, norm2[l]), w_up[l], w_down[l])
        ret_s.append(s_new)
        pool_s.append(buf_new)
        ck_s.append(kc)
        cv_s.append(vc)
        sk_s.append(ks)
        sv_s.append(vs)
        wk_s.append(nwk)
        wv_s.append(nwv)
    return (rms_norm(xp, norm_f), rms_norm(xs, norm_f),
            jnp.stack(ret_p), jnp.stack(ret_s), jnp.stack(pool_p), jnp.stack(pool_s),
            jnp.stack(ck_p), jnp.stack(ck_s), jnp.stack(cv_p), jnp.stack(cv_s),
            jnp.stack(sk_p), jnp.stack(sk_s), jnp.stack(sv_p), jnp.stack(sv_s),
            jnp.stack(wk_p), jnp.stack(wk_s), jnp.stack(wv_p), jnp.stack(wv_s))
```

```python
import functools

import numpy as np
import jax
import jax.numpy as jnp
from jax import lax
from jax.experimental import pallas as pl
from jax.experimental.pallas import tpu as pltpu

F32 = jnp.float32
BF16 = jnp.bfloat16

D_MODEL = 2048
HEAD_DIM = 128
PAGE_SIZE = 128
RET_W = D_MODEL // 4
POOL_W = D_MODEL // 4
NSA_W = D_MODEL - RET_W - POOL_W
R_HEADS = RET_W // HEAD_DIM
N_HEADS = NSA_W // HEAD_DIM
N_KV = max(1, N_HEADS // 4)
HPG = N_HEADS // N_KV
KV_W = N_KV * HEAD_DIM
POOL_WINDOWS = (2, 4, 8, 16)
POOL_GW = POOL_W // len(POOL_WINDOWS)
POOL_BUF = max(POOL_WINDOWS) - 1
RET_CHUNK = 128
CMP_LEN = 32
CMP_STRIDE = 16
CMP_RATIO = CMP_LEN // CMP_STRIDE
SEL_LEN = 64
SEL_RATIO = SEL_LEN // CMP_STRIDE
SEL_TOPK = 16
N_LOCAL_SEL = 2
WINDOW = 512
Q_BLOCK = 128
D_FF = 4 * D_MODEL
ROPE_THETA = 10000.0
EPS = 1e-6
D_IN = 4 * RET_W + POOL_W + NSA_W + 6 * KV_W + 3 * N_HEADS

OFF_RQ = 0
OFF_RK = OFF_RQ + RET_W
OFF_RV = OFF_RK + RET_W
OFF_RG = OFF_RV + RET_W
OFF_PU = OFF_RG + RET_W
OFF_NQ = OFF_PU + POOL_W
OFF_KC = OFF_NQ + NSA_W
OFF_VC = OFF_KC + KV_W
OFF_KS = OFF_VC + KV_W
OFF_VS = OFF_KS + KV_W
OFF_KW = OFF_VS + KV_W
OFF_VW = OFF_KW + KV_W
OFF_GL = OFF_VW + KV_W
PROJ_TN = 512
N_PAD = -(-D_IN // PROJ_TN) * PROJ_TN
ROPE_FULL_TILES = (OFF_RQ // PROJ_TN, OFF_RK // PROJ_TN, OFF_NQ // PROJ_TN, OFF_NQ // PROJ_TN + 1)
ROPE_HALF_TILES = (OFF_KC // PROJ_TN, OFF_KS // PROJ_TN, OFF_KW // PROJ_TN)
RK_TILE = OFF_RK // PROJ_TN

NEG = -1e30
VMEM_LIMIT = 48 * 1024 * 1024
LANES = 128
SUBLANES = 8


def _cparams(sem):
    return pltpu.CompilerParams(dimension_semantics=sem, vmem_limit_bytes=VMEM_LIMIT)


def _dot(a, b):
    return jnp.dot(a, b, preferred_element_type=F32)


def _dot_nt(a, b):
    return lax.dot_general(a, b, (((1,), (1,)), ((), ())), preferred_element_type=F32)


def _dot_tn(a, b):
    return lax.dot_general(a, b, (((0,), (0,)), ((), ())), preferred_element_type=F32)


def _rms(x, g):
    return x * lax.rsqrt(jnp.mean(x * x, axis=-1, keepdims=True) + EPS) * g


def _silu(x):
    return x * jax.nn.sigmoid(x)


def _inproj_kernel(x_ref, g_ref, w_ref, cos_ref, sin_ref, z_ref, hn_ref):
    j = pl.program_id(1)

    @pl.when(j == 0)
    def _():
        hn_ref[...] = _rms(x_ref[...], g_ref[...]).astype(BF16)

    z = _dot(hn_ref[...], w_ref[...])
    n_slab = PROJ_TN // HEAD_DIM

    def rope_slab(s):
        zs = z[:, s * HEAD_DIM:(s + 1) * HEAD_DIM]
        return zs * cos_ref[...] + pltpu.roll(zs, HEAD_DIM // 2, 1) * sin_ref[...]

    is_full = functools.reduce(jnp.logical_or, [j == t for t in ROPE_FULL_TILES])
    is_half = functools.reduce(jnp.logical_or, [j == t for t in ROPE_HALF_TILES])

    @pl.when(is_full)
    def _():
        scale = jnp.where(j == RK_TILE, HEAD_DIM ** -0.5, 1.0).astype(F32)
        for s in range(n_slab):
            z_ref[:, s * HEAD_DIM:(s + 1) * HEAD_DIM] = rope_slab(s) * scale

    @pl.when(is_half)
    def _():
        for s in range(n_slab):
            sl = slice(s * HEAD_DIM, (s + 1) * HEAD_DIM)
            z_ref[:, sl] = rope_slab(s) if s < KV_W // HEAD_DIM else z[:, sl]

    @pl.when(jnp.logical_not(jnp.logical_or(is_full, is_half)))
    def _():
        z_ref[...] = z


def _inproj(x, gamma, w_in_bf, layer, cos_t, sin_t, tm):
    m = x.shape[0]
    n_row_tiles = cos_t.shape[0] // tm
    return pl.pallas_call(
        _inproj_kernel,
        grid=(m // tm, N_PAD // PROJ_TN),
        in_specs=[
            pl.BlockSpec((tm, D_MODEL), lambda i, j: (i, 0)),
            pl.BlockSpec((None, 1, D_MODEL), lambda i, j: (layer, 0, 0)),
            pl.BlockSpec((None, D_MODEL, PROJ_TN), lambda i, j: (layer, 0, j)),
            pl.BlockSpec((tm, HEAD_DIM), lambda i, j: (i % n_row_tiles, 0)),
            pl.BlockSpec((tm, HEAD_DIM), lambda i, j: (i % n_row_tiles, 0)),
        ],
        out_specs=pl.BlockSpec((tm, PROJ_TN), lambda i, j: (i, j)),
        out_shape=jax.ShapeDtypeStruct((m, N_PAD), F32),
        scratch_shapes=[pltpu.VMEM((tm, D_MODEL), BF16)],
        compiler_params=_cparams(("parallel", "arbitrary")),
        name="inproj",
    )(x, gamma, w_in_bf, cos_t, sin_t)


def _outproj_kernel(r_ref, p_ref, n_ref, w_ref, x_ref, o_ref):
    acc = _dot(r_ref[...].astype(BF16), w_ref[0:RET_W, :])
    acc += _dot(p_ref[...].astype(BF16), w_ref[RET_W:RET_W + POOL_W, :])
    acc += _dot(n_ref[...].astype(BF16), w_ref[RET_W + POOL_W:, :])
    o_ref[...] = x_ref[...] + acc


def _outproj(o_r, o_p, o_n, w_out_bf, layer, x, tm, tn=1024):
    m = x.shape[0]
    return pl.pallas_call(
        _outproj_kernel,
        grid=(m // tm, D_MODEL // tn),
        in_specs=[
            pl.BlockSpec((tm, RET_W), lambda i, j: (i, 0)),
            pl.BlockSpec((tm, POOL_W), lambda i, j: (i, 0)),
            pl.BlockSpec((tm, NSA_W), lambda i, j: (i, 0)),
            pl.BlockSpec((None, D_MODEL, tn), lambda i, j: (layer, 0, j)),
            pl.BlockSpec((tm, tn), lambda i, j: (i, j)),
        ],
        out_specs=pl.BlockSpec((tm, tn), lambda i, j: (i, j)),
        out_shape=jax.ShapeDtypeStruct((m, D_MODEL), F32),
        compiler_params=_cparams(("parallel", "parallel")),
        name="outproj",
    )(o_r, o_p, o_n, w_out_bf, x)


def _mlp_kernel(x_ref, g_ref, wu_ref, wd_ref, gf_ref, o_ref, hn_ref, acc_ref, *, final_norm):
    f = pl.program_id(1)

    @pl.when(f == 0)
    def _():
        hn_ref[...] = _rms(x_ref[...], g_ref[...]).astype(BF16)
        acc_ref[...] = jnp.zeros_like(acc_ref)

    u = jnp.maximum(_dot(hn_ref[...], wu_ref[...]), 0.0)
    acc_ref[...] += _dot((u * u).astype(BF16), wd_ref[...])

    @pl.when(f == pl.num_programs(1) - 1)
    def _():
        y = x_ref[...] + acc_ref[...]
        o_ref[...] = _rms(y, gf_ref[...]) if final_norm else y


def _mlp(x, gamma, w_up_bf, w_down_bf, layer, gamma_f, final_norm, tm, tf=512):
    m = x.shape[0]
    return pl.pallas_call(
        functools.partial(_mlp_kernel, final_norm=final_norm),
        grid=(m // tm, D_FF // tf),
        in_specs=[
            pl.BlockSpec((tm, D_MODEL), lambda i, f: (i, 0)),
            pl.BlockSpec((None, 1, D_MODEL), lambda i, f: (layer, 0, 0)),
            pl.BlockSpec((None, D_MODEL, tf), lambda i, f: (layer, 0, f)),
            pl.BlockSpec((None, tf, D_MODEL), lambda i, f: (layer, f, 0)),
            pl.BlockSpec((1, D_MODEL), lambda i, f: (0, 0)),
        ],
        out_specs=pl.BlockSpec((tm, D_MODEL), lambda i, f: (i, 0)),
        out_shape=jax.ShapeDtypeStruct((m, D_MODEL), F32),
        scratch_shapes=[pltpu.VMEM((tm, D_MODEL), BF16), pltpu.VMEM((tm, D_MODEL), F32)],
        compiler_params=_cparams(("parallel", "arbitrary")),
        name="mlp",
    )(x, gamma, w_up_bf, w_down_bf, gamma_f)


def _ret_prompt_kernel(lg_ref, q_ref, k_ref, v_ref, g_ref, o_ref, s_ref, *, n_chunk):
    lg = lg_ref[pl.program_id(1)]
    L = RET_CHUNK
    ri = lax.broadcasted_iota(jnp.int32, (L, L), 0)
    ci = lax.broadcasted_iota(jnp.int32, (L, L), 1)
    diff = (ri - ci).astype(F32)
    causal = diff >= 0
    decay = jnp.where(causal, jnp.exp(jnp.where(causal, diff, 0.0) * lg), 0.0)
    idx = lax.broadcasted_iota(jnp.int32, (L, 1), 0).astype(F32)
    q_dec = jnp.exp((idx + 1.0) * lg)
    k_dec = jnp.exp((L - 1.0 - idx) * lg)
    s_dec = jnp.exp(jnp.full((1, 1), float(L), F32) * lg)

    def body(c, state):
        rows = pl.ds(pl.multiple_of(c * L, L), L)
        q, k, v = q_ref[rows, :], k_ref[rows, :], v_ref[rows, :]
        vb = v.astype(BF16)
        att = _dot_nt(q.astype(BF16), k.astype(BF16)) * decay
        o = _dot(att.astype(BF16), vb) + _dot((q * q_dec).astype(BF16), state.astype(BF16))
        state = s_dec * state + _dot_tn((k * k_dec).astype(BF16), vb)
        o = o * lax.rsqrt(jnp.mean(o * o, axis=-1, keepdims=True) + EPS)
        o_ref[rows, :] = o * _silu(g_ref[rows, :])
        return state

    s_ref[...] = lax.fori_loop(0, n_chunk, body, jnp.zeros((HEAD_DIM, HEAD_DIM), F32))


def _ret_prompt(z, log_g, bsz, t):
    cb = lambda off: (lambda b, h, lg: (b, off // HEAD_DIM + h))
    blk = lambda off: pl.BlockSpec((t, HEAD_DIM), cb(off))
    return pl.pallas_call(
        functools.partial(_ret_prompt_kernel, n_chunk=t // RET_CHUNK),
        grid_spec=pltpu.PrefetchScalarGridSpec(
            num_scalar_prefetch=1,
            grid=(bsz, R_HEADS),
            in_specs=[blk(OFF_RQ), blk(OFF_RK), blk(OFF_RV), blk(OFF_RG)],
            out_specs=[
                pl.BlockSpec((t, HEAD_DIM), lambda b, h, lg: (b, h)),
                pl.BlockSpec((None, None, HEAD_DIM, HEAD_DIM), lambda b, h, lg: (b, h, 0, 0)),
            ],
        ),
        out_shape=[
            jax.ShapeDtypeStruct((bsz * t, RET_W), F32),
            jax.ShapeDtypeStruct((bsz, R_HEADS, HEAD_DIM, HEAD_DIM), F32),
        ],
        compiler_params=_cparams(("parallel", "parallel")),
        name="ret_prompt",
    )(log_g, z, z, z, z)


POOL_HALO = 16


def _pool_prompt_kernel(u_ref, wp_ref, ps_ref, o_ref, ext_ref, *, tt):
    t = pl.program_id(1)

    @pl.when(t == 0)
    def _():
        ext_ref[0:POOL_HALO, :] = jnp.zeros((POOL_HALO, POOL_W), F32)

    u = u_ref[...]
    ext_ref[POOL_HALO:POOL_HALO + tt, :] = u
    pos = t * tt + lax.broadcasted_iota(jnp.int32, (tt, 1), 0)
    for g, w in enumerate(POOL_WINDOWS):
        sl = slice(g * POOL_GW, (g + 1) * POOL_GW)
        s = u[:, sl]
        for i in range(1, w):
            s = s + ext_ref[pl.ds(POOL_HALO - i, tt), sl]
        cnt = jnp.minimum(pos + 1, w).astype(F32)
        p = s / cnt - u[:, sl]
        o_ref[:, sl] = _dot(p.astype(BF16), wp_ref[g].astype(BF16)) * ps_ref[:, sl]
    ext_ref[0:POOL_HALO, :] = ext_ref[tt:tt + POOL_HALO, :]


def _pool_prompt(z, w_pool, pool_scale, layer, bsz, t, tt=512):
    nt = t // tt
    return pl.pallas_call(
        functools.partial(_pool_prompt_kernel, tt=tt),
        grid=(bsz, nt),
        in_specs=[
            pl.BlockSpec((tt, POOL_W), lambda b, i: (b * nt + i, OFF_PU // POOL_W)),
            pl.BlockSpec((None, len(POOL_WINDOWS), POOL_GW, POOL_GW), lambda b, i: (layer, 0, 0, 0)),
            pl.BlockSpec((None, 1, POOL_W), lambda b, i: (layer, 0, 0)),
        ],
        out_specs=pl.BlockSpec((tt, POOL_W), lambda b, i: (b * nt + i, 0)),
        out_shape=jax.ShapeDtypeStruct((bsz * t, POOL_W), F32),
        scratch_shapes=[pltpu.VMEM((POOL_HALO + tt, POOL_W), F32)],
        compiler_params=_cparams(("parallel", "arbitrary")),
        name="pool_prompt",
    )(z, w_pool, pool_scale)


def _compress_core(rows_at, w1_ref, w2_ref, pe_ref, n_rows):
    acc = jnp.zeros((n_rows + SUBLANES, 2 * HEAD_DIM), F32)
    pad = jnp.zeros((SUBLANES - CMP_RATIO, HEAD_DIM), F32)
    for s in range(CMP_STRIDE):
        pe_rows = [pe_ref[pl.ds(r * CMP_STRIDE + s, 1), :] for r in range(CMP_RATIO)]
        xa = jnp.concatenate([rows_at(s)] + pe_rows + [pad], axis=0).astype(BF16)
        acc += _dot(xa, w1_ref[s])
    first = acc[:n_rows, :HEAD_DIM]
    second = acc[:n_rows, HEAD_DIM:]
    pe_term = acc[n_rows:n_rows + 1, :HEAD_DIM] + acc[n_rows + 1:n_rows + 2, HEAD_DIM:]
    h = first + pltpu.roll(second, n_rows - 1, 0) + pe_term
    return _dot(_silu(h).astype(BF16), w2_ref[...])


def _prep_prompt_kernel(kc_ref, vc_ref, vs_ref, vw_ref, w1k_ref, w2k_ref, pek_ref, w1v_ref, w2v_ref, pev_ref,
                        kcs_ref, vcst_ref, vst_ref, vwt_ref, *, n_chunk, n_tile):
    rows_of = lambda r: (lambda s: r[pl.ds(s, n_chunk, stride=CMP_STRIDE), :])
    kcs_ref[...] = _compress_core(rows_of(kc_ref), w1k_ref, w2k_ref, pek_ref, n_chunk)
    vcst_ref[...] = _compress_core(rows_of(vc_ref), w1v_ref, w2v_ref, pev_ref, n_chunk).T

    def tbody(i, carry):
        rows = pl.ds(pl.multiple_of(i * HEAD_DIM, HEAD_DIM), HEAD_DIM)
        vst_ref[i] = vs_ref[rows, :].T
        vwt_ref[i] = vw_ref[rows, :].T
        return carry

    lax.fori_loop(0, n_tile, tbody, 0)


def _prep_prompt(z, cw, layer, bsz, t):
    n_chunk = t // CMP_STRIDE
    n_tile = t // HEAD_DIM
    col = lambda off: pl.BlockSpec((t, HEAD_DIM), lambda b, g: (b, off // HEAD_DIM + g))
    w1 = pl.BlockSpec((None, CMP_STRIDE, HEAD_DIM, 2 * HEAD_DIM), lambda b, g: (layer, 0, 0, 0))
    w2 = pl.BlockSpec((None, HEAD_DIM, HEAD_DIM), lambda b, g: (layer, 0, 0))
    pe = pl.BlockSpec((None, CMP_LEN, HEAD_DIM), lambda b, g: (layer, 0, 0))
    return pl.pallas_call(
        functools.partial(_prep_prompt_kernel, n_chunk=n_chunk, n_tile=n_tile),
        grid=(bsz, N_KV),
        in_specs=[col(OFF_KC), col(OFF_VC), col(OFF_VS), col(OFF_VW), w1, w2, pe, w1, w2, pe],
        out_specs=[
            pl.BlockSpec((None, None, n_chunk, HEAD_DIM), lambda b, g: (b, g, 0, 0)),
            pl.BlockSpec((None, None, HEAD_DIM, n_chunk), lambda b, g: (b, g, 0, 0)),
            pl.BlockSpec((None, None, n_tile, HEAD_DIM, HEAD_DIM), lambda b, g: (b, g, 0, 0, 0)),
            pl.BlockSpec((None, None, n_tile, HEAD_DIM, HEAD_DIM), lambda b, g: (b, g, 0, 0, 0)),
        ],
        out_shape=[
            jax.ShapeDtypeStruct((bsz, N_KV, n_chunk, HEAD_DIM), F32),
            jax.ShapeDtypeStruct((bsz, N_KV, HEAD_DIM, n_chunk), F32),
            jax.ShapeDtypeStruct((bsz, N_KV, n_tile, HEAD_DIM, HEAD_DIM), F32),
            jax.ShapeDtypeStruct((bsz, N_KV, n_tile, HEAD_DIM, HEAD_DIM), F32),
        ],
        compiler_params=_cparams(("parallel", "parallel")),
        name="prep_prompt",
    )(z, z, z, z, cw["w1k"], cw["w2k"], cw["pek"], cw["w1v"], cw["w2v"], cw["pev"])


def _online_softmax_step(s, mask, vt_bf, carry):
    m, l, acc = carry
    m_new = jnp.maximum(m, jnp.max(jnp.where(mask, s, NEG), axis=0, keepdims=True))
    alpha = jnp.exp(m - m_new)
    p = jnp.where(mask, jnp.exp(s - m_new), 0.0)
    l = alpha * l + jnp.sum(p, axis=0, keepdims=True)
    acc = alpha * acc + _dot(vt_bf, p.astype(BF16))
    return m_new, l, acc


def _select_blocks(score, n_blocks):
    jb = lax.broadcasted_iota(jnp.int32, score.shape, 0)
    sel = jnp.zeros(score.shape, F32)
    for _ in range(min(SEL_TOPK, n_blocks)):
        mx = jnp.max(score, axis=0, keepdims=True)
        idx = jnp.min(jnp.where(score == mx, jb, n_blocks), axis=0, keepdims=True)
        pick = jnp.logical_and(jb == idx, mx > -jnp.inf)
        sel = jnp.where(pick, 1.0, sel)
        score = jnp.where(pick, -jnp.inf, score)
    return sel


def _nsa_prompt_kernel(q_ref, gl_ref, kcs_ref, vcst_ref, ks_ref, vst_ref, kw_ref, vwt_ref, o_ref,
                       imp_ref, sel_ref, gt_ref, *, n_chunk, n_sel):
    g = pl.program_id(1)
    i = pl.program_id(2)
    qw = HPG * Q_BLOCK
    scale = HEAD_DIM ** -0.5
    q = q_ref[...]
    qt = jnp.concatenate([(q[:, h * HEAD_DIM:(h + 1) * HEAD_DIM] * scale).T for h in range(HPG)],
                         axis=1).astype(BF16)
    lane = lax.broadcasted_iota(jnp.int32, (1, qw), 1)
    tpos4 = i * Q_BLOCK + jnp.bitwise_and(lane, Q_BLOCK - 1)
    tpos = i * Q_BLOCK + lax.broadcasted_iota(jnp.int32, (1, Q_BLOCK), 1)

    sc = _dot(kcs_ref[...].astype(BF16), qt)
    c_end = lax.broadcasted_iota(jnp.int32, (n_chunk, 1), 0) * CMP_STRIDE + (CMP_LEN - 1)
    mask_c = jnp.logical_and(c_end <= tpos4, c_end < n_chunk * CMP_STRIDE)
    m = jnp.max(jnp.where(mask_c, sc, NEG), axis=0, keepdims=True)
    e = jnp.where(mask_c, jnp.exp(sc - m), 0.0)
    d = jnp.sum(e, axis=0, keepdims=True)
    pc = e * (1.0 / jnp.where(d > 0, d, 1.0))
    oc = _dot(vcst_ref[...].astype(BF16), pc.astype(BF16))
    imp = pc[:, 0:Q_BLOCK]
    for h in range(1, HPG):
        imp = imp + pc[:, h * Q_BLOCK:(h + 1) * Q_BLOCK]

    imp_ref[0:SUBLANES, :] = jnp.zeros((SUBLANES, Q_BLOCK), F32)
    imp_ref[SUBLANES:SUBLANES + n_chunk, :] = imp
    imp_ref[SUBLANES + n_chunk:, :] = jnp.zeros((SUBLANES, Q_BLOCK), F32)
    p_slc = None
    for mm in range(SEL_RATIO):
        for nn in range(CMP_RATIO):
            r = imp_ref[pl.ds(SUBLANES + mm - nn, n_sel, stride=SEL_RATIO), :]
            p_slc = r if p_slc is None else p_slc + r
    jb = lax.broadcasted_iota(jnp.int32, (n_sel, 1), 0)
    blk_valid = jb * SEL_LEN <= tpos
    cur = lax.shift_right_logical(tpos, int(np.log2(SEL_LEN)))
    forced = jnp.logical_or(jb == 0, jnp.logical_and(jb <= cur, jb > cur - N_LOCAL_SEL))
    score = jnp.where(blk_valid, jnp.where(forced, jnp.inf, p_slc), -jnp.inf)
    sel = _select_blocks(score, n_sel)
    sel_ref[...] = jnp.concatenate([sel] * HPG, axis=1)

    row = lax.broadcasted_iota(jnp.int32, (Q_BLOCK, 1), 0)
    init = (jnp.full((1, qw), NEG, F32), jnp.zeros((1, qw), F32), jnp.zeros((HEAD_DIM, qw), F32))
    blocks_per_tile = Q_BLOCK // SEL_LEN

    def sel_body(kt, carry):
        rows = pl.ds(pl.multiple_of(kt * Q_BLOCK, Q_BLOCK), Q_BLOCK)
        s = _dot(ks_ref[rows, :].astype(BF16), qt)
        kpos = kt * Q_BLOCK + row
        selm = sel_ref[pl.ds(kt * blocks_per_tile, 1), :]
        for bb in range(1, blocks_per_tile):
            selm = jnp.where(row < bb * SEL_LEN, selm, sel_ref[pl.ds(kt * blocks_per_tile + bb, 1), :])
        mask = jnp.logical_and(selm > 0.5, kpos <= tpos4)
        return _online_softmax_step(s, mask, vst_ref[kt].astype(BF16), carry)

    _, l_s, a_s = lax.fori_loop(0, i + 1, sel_body, init)

    carry = init
    n_back = WINDOW // Q_BLOCK
    for w in range(n_back + 1):
        kb = i - n_back + w
        kbc = jnp.maximum(kb, 0)
        rows = pl.ds(pl.multiple_of(kbc * Q_BLOCK, Q_BLOCK), Q_BLOCK)
        s = _dot(kw_ref[rows, :].astype(BF16), qt)
        kpos = kb * Q_BLOCK + row
        dpos = tpos4 - kpos
        mask = jnp.logical_and(jnp.logical_and(dpos >= 0, dpos < WINDOW), kpos >= 0)
        carry = _online_softmax_step(s, mask, vwt_ref[kbc].astype(BF16), carry)
    _, l_w, a_w = carry

    gt_ref[...] = jax.nn.sigmoid(gl_ref[...]).T
    gate = lambda c: jnp.concatenate(
        [gt_ref[pl.ds((g * HPG + h) * 3 + c, 1), :] for h in range(HPG)], axis=1)
    ot = gate(0) * oc + gate(1) * (a_s * (1.0 / l_s)) + gate(2) * (a_w * (1.0 / l_w))
    for h in range(HPG):
        o_ref[:, h * HEAD_DIM:(h + 1) * HEAD_DIM] = ot[:, h * Q_BLOCK:(h + 1) * Q_BLOCK].T


def _nsa_prompt(z, kcs, vcst, vst, vwt, bsz, t):
    n_chunk = t // CMP_STRIDE
    n_sel = t // SEL_LEN
    n_tile = t // Q_BLOCK
    nqb = t // Q_BLOCK
    qw = HPG * HEAD_DIM
    return pl.pallas_call(
        functools.partial(_nsa_prompt_kernel, n_chunk=n_chunk, n_sel=n_sel),
        grid=(bsz, N_KV, nqb),
        in_specs=[
            pl.BlockSpec((Q_BLOCK, qw), lambda b, g, i: (b * nqb + i, OFF_NQ // qw + g)),
            pl.BlockSpec((Q_BLOCK, LANES), lambda b, g, i: (b * nqb + i, OFF_GL // LANES)),
            pl.BlockSpec((None, None, n_chunk, HEAD_DIM), lambda b, g, i: (b, g, 0, 0)),
            pl.BlockSpec((None, None, HEAD_DIM, n_chunk), lambda b, g, i: (b, g, 0, 0)),
            pl.BlockSpec((t, HEAD_DIM), lambda b, g, i: (b, OFF_KS // HEAD_DIM + g)),
            pl.BlockSpec((None, None, n_tile, HEAD_DIM, HEAD_DIM), lambda b, g, i: (b, g, 0, 0, 0)),
            pl.BlockSpec((t, HEAD_DIM), lambda b, g, i: (b, OFF_KW // HEAD_DIM + g)),
            pl.BlockSpec((None, None, n_tile, HEAD_DIM, HEAD_DIM), lambda b, g, i: (b, g, 0, 0, 0)),
        ],
        out_specs=pl.BlockSpec((Q_BLOCK, qw), lambda b, g, i: (b * nqb + i, g)),
        out_shape=jax.ShapeDtypeStruct((bsz * t, NSA_W), F32),
        scratch_shapes=[
            pltpu.VMEM((n_chunk + 2 * SUBLANES, Q_BLOCK), F32),
            pltpu.VMEM((n_sel, HPG * Q_BLOCK), F32),
            pltpu.VMEM((LANES, Q_BLOCK), F32),
        ],
        compiler_params=_cparams(("parallel", "parallel", "arbitrary")),
        name="nsa_prompt",
    )(z, z, kcs, vcst, z, vst, z, vwt)


SAMPLE_TB = 8


def _state_sample_kernel(lg_ref, q_ref, k_ref, v_ref, g_ref, u_ref, sr_ref, sp_ref, wp_ref, ps_ref,
                         or_ref, sn_ref, op_ref, pn_ref, *, pos):
    tb = SAMPLE_TB
    pad = jnp.zeros((SUBLANES - 1, HEAD_DIM), F32)
    for h in range(R_HEADS):
        gamma = jnp.exp(jnp.full((1, HEAD_DIM), lg_ref[h], F32))
        sl = slice(h * HEAD_DIM, (h + 1) * HEAD_DIM)
        rows = []
        for b in range(tb):
            q = q_ref[b:b + 1, sl]
            k = k_ref[b:b + 1, sl]
            v = v_ref[b:b + 1, sl]
            state = sr_ref[b, h]
            k8 = jnp.concatenate([k, pad], axis=0).astype(BF16)
            v8 = jnp.concatenate([v, pad], axis=0).astype(BF16)
            qk = jnp.sum(q * k, axis=1, keepdims=True)
            qs = _dot(jnp.concatenate([q * gamma, pad], axis=0).astype(BF16), state.astype(BF16))[0:1]
            rows.append(qk * v + qs)
            sn_ref[b, h] = gamma[:, 0:1] * state + _dot_tn(k8, v8)
        o = jnp.concatenate(rows, axis=0)
        o = o * lax.rsqrt(jnp.mean(o * o, axis=-1, keepdims=True) + EPS)
        or_ref[:, sl] = o * _silu(g_ref[:, sl])
    u = u_ref[...]
    for gi, w in enumerate(POOL_WINDOWS):
        sl = slice(gi * POOL_GW, (gi + 1) * POOL_GW)
        s = u[:, sl] + jnp.sum(sp_ref[:, POOL_BUF - (w - 1):POOL_BUF, sl], axis=1)
        p = s / float(min(pos + 1, w)) - u[:, sl]
        op_ref[:, sl] = _dot(p.astype(BF16), wp_ref[gi].astype(BF16)) * ps_ref[:, sl]
    pn_ref[:, 0:POOL_BUF - 1, :] = sp_ref[:, 1:POOL_BUF, :]
    for b in range(tb):
        pn_ref[b, POOL_BUF - 1:POOL_BUF, :] = u[b:b + 1, :]


def _state_sample(zs, log_g, state_ret, state_pool, w_pool, pool_scale, layer, dbs, pos):
    tb = SAMPLE_TB
    col = lambda off: pl.BlockSpec((tb, RET_W), lambda i, lg: (i, off // RET_W))
    return pl.pallas_call(
        functools.partial(_state_sample_kernel, pos=pos),
        grid_spec=pltpu.PrefetchScalarGridSpec(
            num_scalar_prefetch=1,
            grid=(dbs // tb,),
            in_specs=[
                col(OFF_RQ), col(OFF_RK), col(OFF_RV), col(OFF_RG), col(OFF_PU),
                pl.BlockSpec((None, tb, R_HEADS, HEAD_DIM, HEAD_DIM), lambda i, lg: (layer, i, 0, 0, 0)),
                pl.BlockSpec((None, tb, POOL_BUF, POOL_W), lambda i, lg: (layer, i, 0, 0)),
                pl.BlockSpec((None, len(POOL_WINDOWS), POOL_GW, POOL_GW), lambda i, lg: (layer, 0, 0, 0)),
                pl.BlockSpec((None, 1, POOL_W), lambda i, lg: (layer, 0, 0)),
            ],
            out_specs=[
                pl.BlockSpec((tb, RET_W), lambda i, lg: (i, 0)),
                pl.BlockSpec((tb, R_HEADS, HEAD_DIM, HEAD_DIM), lambda i, lg: (i, 0, 0, 0)),
                pl.BlockSpec((tb, POOL_W), lambda i, lg: (i, 0)),
                pl.BlockSpec((tb, POOL_BUF, POOL_W), lambda i, lg: (i, 0, 0)),
            ],
        ),
        out_shape=[
            jax.ShapeDtypeStruct((dbs, RET_W), F32),
            jax.ShapeDtypeStruct((dbs, R_HEADS, HEAD_DIM, HEAD_DIM), F32),
            jax.ShapeDtypeStruct((dbs, POOL_W), F32),
            jax.ShapeDtypeStruct((dbs, POOL_BUF, POOL_W), F32),
        ],
        compiler_params=_cparams(("parallel",)),
        name="state_sample",
    )(log_g, zs, zs, zs, zs, zs, state_ret, state_pool, w_pool, pool_scale)


def _masked_softmax_lanes(s, mask, s_new):
    m = jnp.maximum(jnp.max(jnp.where(mask, s, NEG), axis=1, keepdims=True), s_new)
    p = jnp.where(mask, jnp.exp(s - m), 0.0)
    p_new = jnp.exp(s_new - m)
    inv = 1.0 / (jnp.sum(p, axis=1, keepdims=True) + p_new)
    return p * inv, p_new * inv


def _nsa_sample_kernel(pt_ref, q0_ref, q1_ref, ksn_ref, vsn_ref, kwn_ref, vwn_ref, gl_ref, wsel_ref,
                       w1k_ref, w2k_ref, pek_ref, w1v_ref, w2v_ref, pev_ref, wk_ref, wv_ref, *rest,
                       n_pages, past, w_buf):
    del pt_ref
    pages = rest[:4 * n_pages]
    ck, cv, sk, sv = (pages[c * n_pages:(c + 1) * n_pages] for c in range(4))
    o_ref, wko_ref, wvo_ref = rest[4 * n_pages:]
    scale = HEAD_DIM ** -0.5
    chunks_per_page = PAGE_SIZE // CMP_STRIDE
    n_chunk = n_pages * chunks_per_page
    n_sel = -(-(past + 1) // SEL_LEN)
    assert n_chunk <= LANES and n_sel <= LANES and past % SEL_LEN == 0

    def rows_of(prefs):
        def rows_at(s):
            return jnp.concatenate(
                [p[pl.ds(N_KV * s + g, chunks_per_page, stride=N_KV * CMP_STRIDE), :]
                 for g in range(N_KV) for p in prefs], axis=0)
        return rows_at

    ksum = _compress_core(rows_of(ck), w1k_ref, w2k_ref, pek_ref, N_KV * n_chunk)
    vsum = _compress_core(rows_of(cv), w1v_ref, w2v_ref, pev_ref, N_KV * n_chunk)

    lane = lax.broadcasted_iota(jnp.int32, (1, LANES), 1)
    q_refs = (q0_ref, q1_ref)
    zpad = jnp.zeros((SUBLANES - HPG, HEAD_DIM), F32)
    q8, oc, imps = [], [], []
    for g in range(N_KV):
        qg = jnp.concatenate([q_refs[g][:, h * HEAD_DIM:(h + 1) * HEAD_DIM] for h in range(HPG)] + [zpad],
                             axis=0) * scale
        q8.append(qg.astype(BF16))
        kcs = ksum[g * n_chunk:(g + 1) * n_chunk]
        vcs = vsum[g * n_chunk:(g + 1) * n_chunk]
        s_c = _dot_nt(q8[g], kcs.astype(BF16))
        c_end = lane[:, :n_chunk] * CMP_STRIDE + (CMP_LEN - 1)
        mask_c = jnp.logical_and(c_end <= past, c_end < n_chunk * CMP_STRIDE)
        m = jnp.max(jnp.where(mask_c, s_c, NEG), axis=1, keepdims=True)
        e = jnp.where(mask_c, jnp.exp(s_c - m), 0.0)
        d = jnp.sum(e, axis=1, keepdims=True)
        pc = e * (1.0 / jnp.where(d > 0, d, 1.0))
        oc.append(_dot(pc.astype(BF16), vcs.astype(BF16)))
        imps.append(jnp.sum(pc[0:HPG], axis=0, keepdims=True))

    imp = jnp.concatenate(imps + [jnp.zeros((SUBLANES - N_KV, n_chunk), F32)], axis=0)
    if n_chunk < LANES:
        imp = jnp.concatenate([imp, jnp.zeros((SUBLANES, LANES - n_chunk), F32)], axis=1)
    hi = imp.astype(BF16)
    r1 = imp - hi.astype(F32)
    mid = r1.astype(BF16)
    lo = (r1 - mid.astype(F32)).astype(BF16)
    wsel = wsel_ref[...]
    p_slc = _dot(hi, wsel) + _dot(mid, wsel) + _dot(lo, wsel)
    cur = past // SEL_LEN
    blk_valid = jnp.logical_and(lane * SEL_LEN <= past, lane < n_sel)
    forced = jnp.logical_or(lane == 0, jnp.logical_and(lane <= cur, lane > cur - N_LOCAL_SEL))
    score = jnp.where(blk_valid, jnp.where(forced, jnp.inf, p_slc), -jnp.inf)
    sel = jnp.zeros((SUBLANES, LANES), F32)
    for _ in range(min(SEL_TOPK, n_sel)):
        mx = jnp.max(score, axis=1, keepdims=True)
        idx = jnp.min(jnp.where(score == mx, lane, LANES), axis=1, keepdims=True)
        pick = jnp.logical_and(lane == idx, mx > -jnp.inf)
        sel = jnp.where(pick, 1.0, sel)
        score = jnp.where(pick, -jnp.inf, score)

    gate = jax.nn.sigmoid(gl_ref[...])
    blocks_per_page = PAGE_SIZE // SEL_LEN
    for g in range(N_KV):
        kv_new = slice(g * HEAD_DIM, (g + 1) * HEAD_DIM)
        s_pages, masks = [], []
        for p in range(n_pages):
            kp = sk[p][pl.ds(g, PAGE_SIZE, stride=N_KV), :].astype(BF16)
            s_pages.append(_dot_nt(q8[g], kp))
            selm = sel[g:g + 1, p * blocks_per_page:p * blocks_per_page + 1]
            for bb in range(1, blocks_per_page):
                nxt = sel[g:g + 1, p * blocks_per_page + bb:p * blocks_per_page + bb + 1]
                selm = jnp.where(lane < bb * SEL_LEN, selm, nxt)
            masks.append(jnp.broadcast_to(selm > 0.5, (SUBLANES, PAGE_SIZE)))
        s_all = jnp.concatenate(s_pages, axis=1)
        mask_all = jnp.concatenate(masks, axis=1)
        q_f = q8[g].astype(F32)
        s_new = jnp.sum(q_f * ksn_ref[:, kv_new].astype(BF16).astype(F32), axis=1, keepdims=True)
        p_all, p_new = _masked_softmax_lanes(s_all, mask_all, s_new)
        o_s = p_new.astype(BF16).astype(F32) * vsn_ref[:, kv_new].astype(BF16).astype(F32)
        for p in range(n_pages):
            vp = sv[p][pl.ds(g, PAGE_SIZE, stride=N_KV), :].astype(BF16)
            o_s = o_s + _dot(p_all[:, p * PAGE_SIZE:(p + 1) * PAGE_SIZE].astype(BF16), vp)
        kwc = wk_ref[pl.ds(g, w_buf, stride=N_KV), :].astype(BF16)
        vwc = wv_ref[pl.ds(g, w_buf, stride=N_KV), :].astype(BF16)
        s_w = _dot_nt(q8[g], kwc)
        wl = lax.broadcasted_iota(jnp.int32, (1, w_buf), 1)
        dpos = w_buf - wl
        mask_w = jnp.logical_and(jnp.logical_and(dpos >= 0, dpos < WINDOW), past - dpos >= 0)
        s_wn = jnp.sum(q_f * kwn_ref[:, kv_new].astype(BF16).astype(F32), axis=1, keepdims=True)
        p_w, p_wn = _masked_softmax_lanes(s_w, jnp.broadcast_to(mask_w, s_w.shape), s_wn)
        o_w = p_wn.astype(BF16).astype(F32) * vwn_ref[:, kv_new].astype(BF16).astype(F32) \
            + _dot(p_w.astype(BF16), vwc)
        for h in range(HPG):
            c0 = (g * HPG + h) * 3
            og = (gate[:, c0:c0 + 1] * oc[g][h:h + 1] + gate[:, c0 + 1:c0 + 2] * o_s[h:h + 1]
                  + gate[:, c0 + 2:c0 + 3] * o_w[h:h + 1])
            o_ref[:, (g * HPG + h) * HEAD_DIM:(g * HPG + h + 1) * HEAD_DIM] = og

    keep = N_KV * (w_buf - 1)
    wko_ref[0:keep, :] = wk_ref[N_KV:N_KV * w_buf, :]
    wvo_ref[0:keep, :] = wv_ref[N_KV:N_KV * w_buf, :]
    for g in range(N_KV):
        wko_ref[keep + g:keep + g + 1, :] = kwn_ref[:, g * HEAD_DIM:(g + 1) * HEAD_DIM]
        wvo_ref[keep + g:keep + g + 1, :] = vwn_ref[:, g * HEAD_DIM:(g + 1) * HEAD_DIM]


def _nsa_sample(zs3, page_table, caches, win_k, win_v, cw, wsel, layer, dbs, past, w_buf):
    n_pages = past // PAGE_SIZE
    rowblk = lambda width, off: pl.BlockSpec((None, 1, width), lambda b, pt: (b, 0, off // width))
    const3 = lambda shp: pl.BlockSpec((None,) + shp, lambda b, pt: (layer,) + (0,) * len(shp))
    page = lambda p: pl.BlockSpec((None, None, PAGE_SIZE * N_KV, HEAD_DIM),
                                  lambda b, pt: (layer, pt[b * n_pages + p], 0, 0))
    win = pl.BlockSpec((None, None, w_buf * N_KV, HEAD_DIM), lambda b, pt: (layer, b, 0, 0))
    in_specs = [
        rowblk(HPG * HEAD_DIM, OFF_NQ), rowblk(HPG * HEAD_DIM, OFF_NQ + HPG * HEAD_DIM),
        rowblk(KV_W, OFF_KS), rowblk(KV_W, OFF_VS), rowblk(KV_W, OFF_KW), rowblk(KV_W, OFF_VW),
        rowblk(LANES, OFF_GL),
        pl.BlockSpec((LANES, LANES), lambda b, pt: (0, 0)),
        const3((CMP_STRIDE, HEAD_DIM, 2 * HEAD_DIM)), const3((HEAD_DIM, HEAD_DIM)), const3((CMP_LEN, HEAD_DIM)),
        const3((CMP_STRIDE, HEAD_DIM, 2 * HEAD_DIM)), const3((HEAD_DIM, HEAD_DIM)), const3((CMP_LEN, HEAD_DIM)),
        win, win,
    ] + [page(p) for _ in range(4) for p in range(n_pages)]
    operands = [zs3] * 7 + [wsel, cw["w1k"], cw["w2k"], cw["pek"], cw["w1v"], cw["w2v"], cw["pev"], win_k, win_v]
    for c in caches:
        operands += [c] * n_pages
    return pl.pallas_call(
        functools.partial(_nsa_sample_kernel, n_pages=n_pages, past=past, w_buf=w_buf),
        grid_spec=pltpu.PrefetchScalarGridSpec(
            num_scalar_prefetch=1,
            grid=(dbs,),
            in_specs=in_specs,
            out_specs=[
                pl.BlockSpec((None, 1, NSA_W), lambda b, pt: (b, 0, 0)),
                pl.BlockSpec((None, w_buf * N_KV, HEAD_DIM), lambda b, pt: (b, 0, 0)),
                pl.BlockSpec((None, w_buf * N_KV, HEAD_DIM), lambda b, pt: (b, 0, 0)),
            ],
        ),
        out_shape=[
            jax.ShapeDtypeStruct((dbs, 1, NSA_W), F32),
            jax.ShapeDtypeStruct((dbs, w_buf * N_KV, HEAD_DIM), F32),
            jax.ShapeDtypeStruct((dbs, w_buf * N_KV, HEAD_DIM), F32),
        ],
        compiler_params=_cparams(("parallel",)),
        name="nsa_sample",
    )(page_table.reshape(-1), *operands)


def _rope_tables(pos):
    half = HEAD_DIM // 2
    inv = ROPE_THETA ** (-jnp.arange(half, dtype=F32) / half)
    ang = pos.astype(F32)[:, None] * inv[None, :]
    cos, sin = jnp.cos(ang), jnp.sin(ang)
    return jnp.concatenate([cos, cos], axis=-1), jnp.concatenate([-sin, sin], axis=-1)


def _block_sum_matrix():
    w = np.zeros((LANES, LANES), np.float32)
    for j in range(LANES):
        for mm in range(SEL_RATIO):
            for nn in range(CMP_RATIO):
                c = SEL_RATIO * j + mm - nn
                if 0 <= c < LANES:
                    w[c, j] += 1.0
    return w


def _row_tile(m, cap):
    tm = min(m, cap)
    while m % tm:
        tm //= 2
    return tm


def kernel(x_prompt, x_sample, state_ret, state_pool, cache_cmp_k, cache_cmp_v, cache_slc_k, cache_slc_v,
           cache_win_k, cache_win_v, page_table, norm1, w_in, w_pool, pool_scale, cmp_w1_k, cmp_w2_k, cmp_pe_k,
           cmp_w1_v, cmp_w2_v, cmp_pe_v, w_out, norm2, w_up, w_down, norm_f):
    bsz, t, _ = x_prompt.shape
    dbs, dec_seq, _ = x_sample.shape
    depth = w_in.shape[0]
    assert dec_seq == 1 and t % 512 == 0 and dbs % SAMPLE_TB == 0
    n_pages = page_table.shape[1]
    past = n_pages * PAGE_SIZE
    n_pool = cache_cmp_k.shape[1]
    w_buf = cache_win_k.shape[2]
    wlen = min(WINDOW, t)

    w_in_bf = jnp.pad(w_in, ((0, 0), (0, 0), (0, N_PAD - D_IN))).astype(BF16)
    w_out_bf, w_up_bf, w_down_bf = w_out.astype(BF16), w_up.astype(BF16), w_down.astype(BF16)
    w1cat = lambda w1: jnp.concatenate([w1[:, :CMP_STRIDE], w1[:, CMP_STRIDE:]], axis=-1).astype(BF16)
    cw = dict(w1k=w1cat(cmp_w1_k), w2k=cmp_w2_k.astype(BF16), pek=cmp_pe_k,
              w1v=w1cat(cmp_w1_v), w2v=cmp_w2_v.astype(BF16), pev=cmp_pe_v)
    norm1_3, norm2_3 = norm1[:, None, :], norm2[:, None, :]
    pool_scale_3 = pool_scale[:, None, :]
    norm_f2 = norm_f[None, :]
    log_g = jnp.log1p(-jnp.exp2(-5.0 - jnp.arange(R_HEADS, dtype=F32)))
    cos_p, sin_p = _rope_tables(jnp.arange(t))
    cos_s, sin_s = _rope_tables(jnp.full((dbs,), past, jnp.int32))
    wsel = jnp.asarray(_block_sum_matrix(), BF16)
    page_view = lambda c: c.reshape(depth, n_pool, PAGE_SIZE * N_KV, HEAD_DIM)
    caches = [page_view(c) for c in (cache_cmp_k, cache_cmp_v, cache_slc_k, cache_slc_v)]
    win_k = cache_win_k.reshape(depth, dbs, w_buf * N_KV, HEAD_DIM)
    win_v = cache_win_v.reshape(depth, dbs, w_buf * N_KV, HEAD_DIM)

    mp = bsz * t
    tm_p = _row_tile(t, 1024)
    tm_mlp = _row_tile(t, 512)
    xp = x_prompt.reshape(mp, D_MODEL)
    xs = x_sample.reshape(dbs, D_MODEL)
    outs = {k: [] for k in ("ret_p", "ret_s", "pool_p", "pool_s", "ck_p", "ck_s", "cv_p", "cv_s", "sk_p", "sk_s",
                            "sv_p", "sv_s", "wk_p", "wk_s", "wv_p", "wv_s")}
    for l in range(depth):
        last = l == depth - 1
        z = _inproj(xp, norm1_3, w_in_bf, l, cos_p, sin_p, tm_p)
        o_r, s_new = _ret_prompt(z, log_g, bsz, t)
        o_p = _pool_prompt(z, w_pool, pool_scale_3, l, bsz, t)
        kcs, vcst, vst, vwt = _prep_prompt(z, cw, l, bsz, t)
        o_n = _nsa_prompt(z, kcs, vcst, vst, vwt, bsz, t)
        xp = _outproj(o_r, o_p, o_n, w_out_bf, l, xp, tm_p)
        xp = _mlp(xp, norm2_3, w_up_bf, w_down_bf, l, norm_f2, last, tm_mlp)
        z3 = z.reshape(bsz, t, N_PAD)
        kv = lambda off: z3[:, :, off:off + KV_W].reshape(bsz, t, N_KV, HEAD_DIM)
        outs["ret_p"].append(s_new)
        outs["pool_p"].append(z3[:, t - POOL_BUF:, OFF_PU:OFF_PU + POOL_W])
        outs["ck_p"].append(kv(OFF_KC))
        outs["cv_p"].append(kv(OFF_VC))
        outs["sk_p"].append(kv(OFF_KS))
        outs["sv_p"].append(kv(OFF_VS))
        outs["wk_p"].append(kv(OFF_KW)[:, t - wlen:])
        outs["wv_p"].append(kv(OFF_VW)[:, t - wlen:])
        zs = _inproj(xs, norm1_3, w_in_bf, l, cos_s, sin_s, dbs)
        o_r, s_new, o_p, buf_new = _state_sample(zs, log_g, state_ret, state_pool, w_pool, pool_scale_3, l, dbs, past)
        o_n, nwk, nwv = _nsa_sample(zs.reshape(dbs, 1, N_PAD), page_table, caches, win_k, win_v, cw, wsel,
                                    l, dbs, past, w_buf)
        xs = _outproj(o_r, o_p, o_n.reshape(dbs, NSA_W), w_out_bf, l, xs, dbs)
        xs = _mlp(xs, norm2_3, w_up_bf, w_down_bf, l, norm_f2, last, dbs)
        kvs = lambda off: zs[:, off:off + KV_W].reshape(dbs, 1, N_KV, HEAD_DIM)
        outs["ret_s"].append(s_new)
        outs["pool_s"].append(buf_new)
        outs["ck_s"].append(kvs(OFF_KC))
        outs["cv_s"].append(kvs(OFF_VC))
        outs["sk_s"].append(kvs(OFF_KS))
        outs["sv_s"].append(kvs(OFF_VS))
        outs["wk_s"].append(nwk.reshape(dbs, w_buf, N_KV, HEAD_DIM))
        outs["wv_s"].append(nwv.reshape(dbs, w_buf, N_KV, HEAD_DIM))
    st = lambda k: jnp.stack(outs[k])
    return (xp.reshape(bsz, t, D_MODEL), xs.reshape(dbs, 1, D_MODEL),
            st("ret_p"), st("ret_s"), st("pool_p"), st("pool_s"),
            st("ck_p"), st("ck_s"), st("cv_p"), st("cv_s"), st("sk_p"), st("sk_s"), st("sv_p"), st("sv_s"),
            st("wk_p"), st("wk_s"), st("wv_p"), st("wv_s"))
```

```python
import functools

import numpy as np
import jax
import jax.numpy as jnp
from jax import lax
from jax.experimental import pallas as pl
from jax.experimental.pallas import tpu as pltpu

F32 = jnp.float32
BF16 = jnp.bfloat16

D_MODEL = 2048
HEAD_DIM = 128
PAGE_SIZE = 128
RET_W = D_MODEL // 4
POOL_W = D_MODEL // 4
NSA_W = D_MODEL - RET_W - POOL_W
R_HEADS = RET_W // HEAD_DIM
N_HEADS = NSA_W // HEAD_DIM
N_KV = max(1, N_HEADS // 4)
HPG = N_HEADS // N_KV
KV_W = N_KV * HEAD_DIM
POOL_WINDOWS = (2, 4, 8, 16)
POOL_GW = POOL_W // len(POOL_WINDOWS)
POOL_BUF = max(POOL_WINDOWS) - 1
RET_CHUNK = 128
CMP_LEN = 32
CMP_STRIDE = 16
CMP_RATIO = CMP_LEN // CMP_STRIDE
SEL_LEN = 64
SEL_RATIO = SEL_LEN // CMP_STRIDE
SEL_TOPK = 16
N_LOCAL_SEL = 2
WINDOW = 512
Q_BLOCK = 128
D_FF = 4 * D_MODEL
ROPE_THETA = 10000.0
EPS = 1e-6
D_IN = 4 * RET_W + POOL_W + NSA_W + 6 * KV_W + 3 * N_HEADS

OFF_RQ = 0
OFF_RK = OFF_RQ + RET_W
OFF_RV = OFF_RK + RET_W
OFF_RG = OFF_RV + RET_W
OFF_PU = OFF_RG + RET_W
OFF_NQ = OFF_PU + POOL_W
OFF_KC = OFF_NQ + NSA_W
OFF_VC = OFF_KC + KV_W
OFF_KS = OFF_VC + KV_W
OFF_VS = OFF_KS + KV_W
OFF_KW = OFF_VS + KV_W
OFF_VW = OFF_KW + KV_W
OFF_GL = OFF_VW + KV_W
PROJ_TN = 512
N_PAD = -(-D_IN // PROJ_TN) * PROJ_TN
ROPE_FULL_TILES = (OFF_RQ // PROJ_TN, OFF_RK // PROJ_TN, OFF_NQ // PROJ_TN, OFF_NQ // PROJ_TN + 1)
ROPE_HALF_TILES = (OFF_KC // PROJ_TN, OFF_KS // PROJ_TN, OFF_KW // PROJ_TN)
RK_TILE = OFF_RK // PROJ_TN

NEG = -1e30
VMEM_LIMIT = 48 * 1024 * 1024
LANES = 128
SUBLANES = 8


def _cparams(sem):
    return pltpu.CompilerParams(dimension_semantics=sem, vmem_limit_bytes=VMEM_LIMIT)


def _dot(a, b):
    return jnp.dot(a, b, preferred_element_type=F32)


def _dot_nt(a, b):
    return lax.dot_general(a, b, (((1,), (1,)), ((), ())), preferred_element_type=F32)


def _dot_tn(a, b):
    return lax.dot_general(a, b, (((0,), (0,)), ((), ())), preferred_element_type=F32)


def _rms(x, g):
    return x * lax.rsqrt(jnp.mean(x * x, axis=-1, keepdims=True) + EPS) * g


def _silu(x):
    return x * jax.nn.sigmoid(x)


def _inproj_kernel(x_ref, g_ref, w_ref, cos_ref, sin_ref, z_ref, hn_ref):
    j = pl.program_id(1)

    @pl.when(j == 0)
    def _():
        hn_ref[...] = _rms(x_ref[...], g_ref[...]).astype(BF16)

    z = _dot(hn_ref[...], w_ref[...])
    n_slab = PROJ_TN // HEAD_DIM

    def rope_slab(s):
        zs = z[:, s * HEAD_DIM:(s + 1) * HEAD_DIM]
        return zs * cos_ref[...] + pltpu.roll(zs, HEAD_DIM // 2, 1) * sin_ref[...]

    is_full = functools.reduce(jnp.logical_or, [j == t for t in ROPE_FULL_TILES])
    is_half = functools.reduce(jnp.logical_or, [j == t for t in ROPE_HALF_TILES])

    @pl.when(is_full)
    def _():
        scale = jnp.where(j == RK_TILE, HEAD_DIM ** -0.5, 1.0).astype(F32)
        for s in range(n_slab):
            z_ref[:, s * HEAD_DIM:(s + 1) * HEAD_DIM] = rope_slab(s) * scale

    @pl.when(is_half)
    def _():
        for s in range(n_slab):
            sl = slice(s * HEAD_DIM, (s + 1) * HEAD_DIM)
            z_ref[:, sl] = rope_slab(s) if s < KV_W // HEAD_DIM else z[:, sl]

    @pl.when(jnp.logical_not(jnp.logical_or(is_full, is_half)))
    def _():
        z_ref[...] = z


def _inproj(x, gamma, w_in_bf, layer, cos_t, sin_t, tm):
    m = x.shape[0]
    n_row_tiles = cos_t.shape[0] // tm
    return pl.pallas_call(
        _inproj_kernel,
        grid=(m // tm, N_PAD // PROJ_TN),
        in_specs=[
            pl.BlockSpec((tm, D_MODEL), lambda i, j: (i, 0)),
            pl.BlockSpec((None, 1, D_MODEL), lambda i, j: (layer, 0, 0)),
            pl.BlockSpec((None, D_MODEL, PROJ_TN), lambda i, j: (layer, 0, j)),
            pl.BlockSpec((tm, HEAD_DIM), lambda i, j: (i % n_row_tiles, 0)),
            pl.BlockSpec((tm, HEAD_DIM), lambda i, j: (i % n_row_tiles, 0)),
        ],
        out_specs=pl.BlockSpec((tm, PROJ_TN), lambda i, j: (i, j)),
        out_shape=jax.ShapeDtypeStruct((m, N_PAD), F32),
        scratch_shapes=[pltpu.VMEM((tm, D_MODEL), BF16)],
        compiler_params=_cparams(("parallel", "arbitrary")),
        name="inproj",
    )(x, gamma, w_in_bf, cos_t, sin_t)


def _outproj_kernel(r_ref, p_ref, n_ref, w_ref, x_ref, o_ref):
    acc = _dot(r_ref[...].astype(BF16), w_ref[0:RET_W, :])
    acc += _dot(p_ref[...].astype(BF16), w_ref[RET_W:RET_W + POOL_W, :])
    acc += _dot(n_ref[...].astype(BF16), w_ref[RET_W + POOL_W:, :])
    o_ref[...] = x_ref[...] + acc


def _outproj(o_r, o_p, o_n, w_out_bf, layer, x, tm, tn=1024):
    m = x.shape[0]
    return pl.pallas_call(
        _outproj_kernel,
        grid=(m // tm, D_MODEL // tn),
        in_specs=[
            pl.BlockSpec((tm, RET_W), lambda i, j: (i, 0)),
            pl.BlockSpec((tm, POOL_W), lambda i, j: (i, 0)),
            pl.BlockSpec((tm, NSA_W), lambda i, j: (i, 0)),
            pl.BlockSpec((None, D_MODEL, tn), lambda i, j: (layer, 0, j)),
            pl.BlockSpec((tm, tn), lambda i, j: (i, j)),
        ],
        out_specs=pl.BlockSpec((tm, tn), lambda i, j: (i, j)),
        out_shape=jax.ShapeDtypeStruct((m, D_MODEL), F32),
        compiler_params=_cparams(("parallel", "parallel")),
        name="outproj",
    )(o_r, o_p, o_n, w_out_bf, x)


def _mlp_kernel(x_ref, g_ref, wu_ref, wd_ref, gf_ref, o_ref, hn_ref, acc_ref, *, final_norm):
    f = pl.program_id(1)

    @pl.when(f == 0)
    def _():
        hn_ref[...] = _rms(x_ref[...], g_ref[...]).astype(BF16)
        acc_ref[...] = jnp.zeros_like(acc_ref)

    u = jnp.maximum(_dot(hn_ref[...], wu_ref[...]), 0.0)
    acc_ref[...] += _dot((u * u).astype(BF16), wd_ref[...])

    @pl.when(f == pl.num_programs(1) - 1)
    def _():
        y = x_ref[...] + acc_ref[...]
        o_ref[...] = _rms(y, gf_ref[...]) if final_norm else y


def _mlp(x, gamma, w_up_bf, w_down_bf, layer, gamma_f, final_norm, tm, tf=512):
    m = x.shape[0]
    return pl.pallas_call(
        functools.partial(_mlp_kernel, final_norm=final_norm),
        grid=(m // tm, D_FF // tf),
        in_specs=[
            pl.BlockSpec((tm, D_MODEL), lambda i, f: (i, 0)),
            pl.BlockSpec((None, 1, D_MODEL), lambda i, f: (layer, 0, 0)),
            pl.BlockSpec((None, D_MODEL, tf), lambda i, f: (layer, 0, f)),
            pl.BlockSpec((None, tf, D_MODEL), lambda i, f: (layer, f, 0)),
            pl.BlockSpec((1, D_MODEL), lambda i, f: (0, 0)),
        ],
        out_specs=pl.BlockSpec((tm, D_MODEL), lambda i, f: (i, 0)),
        out_shape=jax.ShapeDtypeStruct((m, D_MODEL), F32),
        scratch_shapes=[pltpu.VMEM((tm, D_MODEL), BF16), pltpu.VMEM((tm, D_MODEL), F32)],
        compiler_params=_cparams(("parallel", "arbitrary")),
        name="mlp",
    )(x, gamma, w_up_bf, w_down_bf, gamma_f)


def _ret_prompt_kernel(lg_ref, q_ref, k_ref, v_ref, g_ref, o_ref, s_ref, *, n_chunk):
    lg = lg_ref[pl.program_id(1)]
    L = RET_CHUNK
    ri = lax.broadcasted_iota(jnp.int32, (L, L), 0)
    ci = lax.broadcasted_iota(jnp.int32, (L, L), 1)
    diff = (ri - ci).astype(F32)
    causal = diff >= 0
    decay = jnp.where(causal, jnp.exp(jnp.where(causal, diff, 0.0) * lg), 0.0)
    idx = lax.broadcasted_iota(jnp.int32, (L, 1), 0).astype(F32)
    q_dec = jnp.exp((idx + 1.0) * lg)
    k_dec = jnp.exp((L - 1.0 - idx) * lg)
    s_dec = jnp.exp(jnp.full((1, 1), float(L), F32) * lg)

    def body(c, state):
        rows = pl.ds(pl.multiple_of(c * L, L), L)
        q, k, v = q_ref[rows, :], k_ref[rows, :], v_ref[rows, :]
        vb = v.astype(BF16)
        att = _dot_nt(q.astype(BF16), k.astype(BF16)) * decay
        o = _dot(att.astype(BF16), vb) + _dot((q * q_dec).astype(BF16), state.astype(BF16))
        state = s_dec * state + _dot_tn((k * k_dec).astype(BF16), vb)
        o = o * lax.rsqrt(jnp.mean(o * o, axis=-1, keepdims=True) + EPS)
        o_ref[rows, :] = o * _silu(g_ref[rows, :])
        return state

    s_ref[...] = lax.fori_loop(0, n_chunk, body, jnp.zeros((HEAD_DIM, HEAD_DIM), F32))


def _ret_prompt(z, log_g, bsz, t):
    cb = lambda off: (lambda b, h, lg: (b, off // HEAD_DIM + h))
    blk = lambda off: pl.BlockSpec((t, HEAD_DIM), cb(off))
    return pl.pallas_call(
        functools.partial(_ret_prompt_kernel, n_chunk=t // RET_CHUNK),
        grid_spec=pltpu.PrefetchScalarGridSpec(
            num_scalar_prefetch=1,
            grid=(bsz, R_HEADS),
            in_specs=[blk(OFF_RQ), blk(OFF_RK), blk(OFF_RV), blk(OFF_RG)],
            out_specs=[
                pl.BlockSpec((t, HEAD_DIM), lambda b, h, lg: (b, h)),
                pl.BlockSpec((None, None, HEAD_DIM, HEAD_DIM), lambda b, h, lg: (b, h, 0, 0)),
            ],
        ),
        out_shape=[
            jax.ShapeDtypeStruct((bsz * t, RET_W), F32),
            jax.ShapeDtypeStruct((bsz, R_HEADS, HEAD_DIM, HEAD_DIM), F32),
        ],
        compiler_params=_cparams(("parallel", "parallel")),
        name="ret_prompt",
    )(log_g, z, z, z, z)


POOL_HALO = 16


def _pool_prompt_kernel(u_ref, wp_ref, ps_ref, o_ref, ext_ref, *, tt):
    t = pl.program_id(1)

    @pl.when(t == 0)
    def _():
        ext_ref[0:POOL_HALO, :] = jnp.zeros((POOL_HALO, POOL_W), F32)

    u = u_ref[...]
    ext_ref[POOL_HALO:POOL_HALO + tt, :] = u
    pos = t * tt + lax.broadcasted_iota(jnp.int32, (tt, 1), 0)
    for g, w in enumerate(POOL_WINDOWS):
        sl = slice(g * POOL_GW, (g + 1) * POOL_GW)
        s = u[:, sl]
        for i in range(1, w):
            s = s + ext_ref[pl.ds(POOL_HALO - i, tt), sl]
        cnt = jnp.minimum(pos + 1, w).astype(F32)
        p = s / cnt - u[:, sl]
        o_ref[:, sl] = _dot(p.astype(BF16), wp_ref[g].astype(BF16)) * ps_ref[:, sl]
    ext_ref[0:POOL_HALO, :] = ext_ref[tt:tt + POOL_HALO, :]


def _pool_prompt(z, w_pool, pool_scale, layer, bsz, t, tt=512):
    nt = t // tt
    return pl.pallas_call(
        functools.partial(_pool_prompt_kernel, tt=tt),
        grid=(bsz, nt),
        in_specs=[
            pl.BlockSpec((tt, POOL_W), lambda b, i: (b * nt + i, OFF_PU // POOL_W)),
            pl.BlockSpec((None, len(POOL_WINDOWS), POOL_GW, POOL_GW), lambda b, i: (layer, 0, 0, 0)),
            pl.BlockSpec((None, 1, POOL_W), lambda b, i: (layer, 0, 0)),
        ],
        out_specs=pl.BlockSpec((tt, POOL_W), lambda b, i: (b * nt + i, 0)),
        out_shape=jax.ShapeDtypeStruct((bsz * t, POOL_W), F32),
        scratch_shapes=[pltpu.VMEM((POOL_HALO + tt, POOL_W), F32)],
        compiler_params=_cparams(("parallel", "arbitrary")),
        name="pool_prompt",
    )(z, w_pool, pool_scale)


CHUNK_W = CMP_STRIDE * HEAD_DIM


def _compress_core(chunks, w1_ref, w2_ref, pe_ref):
    n_rows = chunks.shape[0]
    acc = _dot(jnp.concatenate([chunks, pe_ref[...]], axis=0).astype(BF16), w1_ref[...])
    first = acc[:n_rows, :HEAD_DIM]
    second = acc[:n_rows, HEAD_DIM:]
    pe_term = acc[n_rows:n_rows + 1, :HEAD_DIM] + acc[n_rows + 1:n_rows + 2, HEAD_DIM:]
    h = first + pltpu.roll(second, n_rows - 1, 0) + pe_term
    return _dot(_silu(h).astype(BF16), w2_ref[...])


def _prep_prompt_kernel(kc_ref, vc_ref, vs_ref, vw_ref, w1k_ref, w2k_ref, pek_ref, w1v_ref, w2v_ref, pev_ref,
                        kcs_ref, vcst_ref, vst_ref, vwt_ref, *, n_chunk, n_tile):
    chunks_of = lambda r: jnp.concatenate(
        [r[pl.ds(s, n_chunk, stride=CMP_STRIDE), :] for s in range(CMP_STRIDE)], axis=1)
    kcs_ref[...] = _compress_core(chunks_of(kc_ref), w1k_ref, w2k_ref, pek_ref).astype(BF16)
    vcst_ref[...] = _compress_core(chunks_of(vc_ref), w1v_ref, w2v_ref, pev_ref).T.astype(BF16)

    def tbody(i, carry):
        rows = pl.ds(pl.multiple_of(i * HEAD_DIM, HEAD_DIM), HEAD_DIM)
        vst_ref[i] = vs_ref[rows, :].T.astype(BF16)
        vwt_ref[i] = vw_ref[rows, :].T.astype(BF16)
        return carry

    lax.fori_loop(0, n_tile, tbody, 0)


def _prep_prompt(z, cw, layer, bsz, t):
    n_chunk = t // CMP_STRIDE
    n_tile = t // HEAD_DIM
    col = lambda off: pl.BlockSpec((t, HEAD_DIM), lambda b, g: (b, off // HEAD_DIM + g))
    w1 = pl.BlockSpec((None, CHUNK_W, 2 * HEAD_DIM), lambda b, g: (layer, 0, 0))
    w2 = pl.BlockSpec((None, HEAD_DIM, HEAD_DIM), lambda b, g: (layer, 0, 0))
    pe = pl.BlockSpec((None, SUBLANES, CHUNK_W), lambda b, g: (layer, 0, 0))
    return pl.pallas_call(
        functools.partial(_prep_prompt_kernel, n_chunk=n_chunk, n_tile=n_tile),
        grid=(bsz, N_KV),
        in_specs=[col(OFF_KC), col(OFF_VC), col(OFF_VS), col(OFF_VW), w1, w2, pe, w1, w2, pe],
        out_specs=[
            pl.BlockSpec((None, None, n_chunk, HEAD_DIM), lambda b, g: (b, g, 0, 0)),
            pl.BlockSpec((None, None, HEAD_DIM, n_chunk), lambda b, g: (b, g, 0, 0)),
            pl.BlockSpec((None, None, n_tile, HEAD_DIM, HEAD_DIM), lambda b, g: (b, g, 0, 0, 0)),
            pl.BlockSpec((None, None, n_tile, HEAD_DIM, HEAD_DIM), lambda b, g: (b, g, 0, 0, 0)),
        ],
        out_shape=[
            jax.ShapeDtypeStruct((bsz, N_KV, n_chunk, HEAD_DIM), BF16),
            jax.ShapeDtypeStruct((bsz, N_KV, HEAD_DIM, n_chunk), BF16),
            jax.ShapeDtypeStruct((bsz, N_KV, n_tile, HEAD_DIM, HEAD_DIM), BF16),
            jax.ShapeDtypeStruct((bsz, N_KV, n_tile, HEAD_DIM, HEAD_DIM), BF16),
        ],
        compiler_params=_cparams(("parallel", "parallel")),
        name="prep_prompt",
    )(z, z, z, z, cw["w1k"], cw["w2k"], cw["pek"], cw["w1v"], cw["w2v"], cw["pev"])


LOG2E = 1.4426950408889634


def _softmax_tiles(tiles, carry):
    m, l, acc = carry
    m_new = m
    for s, _ in tiles:
        m_new = jnp.maximum(m_new, jnp.max(s, axis=0, keepdims=True))
    alpha = jnp.exp2(m - m_new)
    l = alpha * l
    acc = alpha * acc
    for s, vt in tiles:
        p = jnp.exp2(s - m_new)
        l = l + jnp.sum(p, axis=0, keepdims=True)
        acc = acc + _dot(vt, p.astype(BF16))
    return m_new, l, acc


def _select_blocks(score, score_ref):
    n_blocks = score.shape[0]
    score_ref[...] = score
    groups = [score[v * SUBLANES:(v + 1) * SUBLANES] for v in range(n_blocks // SUBLANES)]
    jb = lax.broadcasted_iota(jnp.int32, groups[0].shape, 0)
    ranks = [jnp.zeros(gv.shape, F32) for gv in groups]
    for k in range(n_blocks):
        rk = score_ref[pl.ds(k, 1), :]
        for v, gv in enumerate(groups):
            if v * SUBLANES > k:
                beats = rk >= gv
            elif (v + 1) * SUBLANES - 1 <= k:
                beats = rk > gv
            else:
                beats = jnp.logical_or(rk > gv, jnp.logical_and(jb + v * SUBLANES > k, rk == gv))
            ranks[v] = ranks[v] + jnp.where(beats, 1.0, 0.0)
    rank = jnp.concatenate(ranks, axis=0)
    return jnp.logical_and(rank < SEL_TOPK, score > -jnp.inf)


def _nsa_prompt_kernel(q_ref, gl_ref, kcs_ref, vcst_ref, ks_ref, vst_ref, kw_ref, vwt_ref, o_ref,
                       imp_ref, selb_ref, gt_ref, score_ref, *, n_chunk, n_sel):
    g = pl.program_id(1)
    i = pl.program_id(2)
    qw = HPG * Q_BLOCK
    qscale = HEAD_DIM ** -0.5 * LOG2E
    q = q_ref[...]
    qt = jnp.concatenate([(q[:, h * HEAD_DIM:(h + 1) * HEAD_DIM] * qscale).T for h in range(HPG)],
                         axis=1).astype(BF16)
    lane = lax.broadcasted_iota(jnp.int32, (1, qw), 1)
    tl4 = jnp.bitwise_and(lane, Q_BLOCK - 1)
    tpos4 = i * Q_BLOCK + tl4
    tpos = i * Q_BLOCK + lax.broadcasted_iota(jnp.int32, (1, Q_BLOCK), 1)

    sc = _dot(kcs_ref[...], qt)
    c_end = lax.broadcasted_iota(jnp.int32, (n_chunk, 1), 0) * CMP_STRIDE + (CMP_LEN - 1)
    mask_c = jnp.logical_and(c_end <= tpos4, c_end < n_chunk * CMP_STRIDE)
    m = jnp.max(jnp.where(mask_c, sc, NEG), axis=0, keepdims=True)
    e = jnp.where(mask_c, jnp.exp2(sc - m), 0.0)
    d = jnp.sum(e, axis=0, keepdims=True)
    pc = e * (1.0 / jnp.where(d > 0, d, 1.0))
    oc = _dot(vcst_ref[...], pc.astype(BF16))
    imp = pc[:, 0:Q_BLOCK]
    for h in range(1, HPG):
        imp = imp + pc[:, h * Q_BLOCK:(h + 1) * Q_BLOCK]

    imp_ref[0:SUBLANES, :] = jnp.zeros((SUBLANES, Q_BLOCK), F32)
    imp_ref[SUBLANES:SUBLANES + n_chunk, :] = imp
    imp_ref[SUBLANES + n_chunk:, :] = jnp.zeros((SUBLANES, Q_BLOCK), F32)
    p_slc = None
    for mm in range(SEL_RATIO):
        for nn in range(CMP_RATIO):
            r = imp_ref[pl.ds(SUBLANES + mm - nn, n_sel, stride=SEL_RATIO), :]
            p_slc = r if p_slc is None else p_slc + r
    jb = lax.broadcasted_iota(jnp.int32, (n_sel, 1), 0)
    blk_valid = jb * SEL_LEN <= tpos
    cur = lax.shift_right_logical(tpos, int(np.log2(SEL_LEN)))
    forced = jnp.logical_or(jb == 0, jnp.logical_and(jb <= cur, jb > cur - N_LOCAL_SEL))
    score = jnp.where(blk_valid, jnp.where(forced, jnp.inf, p_slc), -jnp.inf)
    sel = _select_blocks(score, score_ref)
    selb_ref[...] = jnp.concatenate([jnp.where(sel, 0.0, NEG)] * HPG, axis=1)

    row = lax.broadcasted_iota(jnp.int32, (Q_BLOCK, 1), 0)
    causal_bias = jnp.where(row <= tl4, 0.0, NEG)
    init = (jnp.full((1, qw), NEG, F32), jnp.zeros((1, qw), F32), jnp.zeros((HEAD_DIM, qw), F32))
    blocks_per_tile = Q_BLOCK // SEL_LEN

    n_back = WINDOW // Q_BLOCK
    win_tiles = []
    for w in range(n_back + 1):
        kb = i - n_back + w
        kbc = jnp.maximum(kb, 0)
        rows = pl.ds(pl.multiple_of(kbc * Q_BLOCK, Q_BLOCK), Q_BLOCK)
        s = _dot(kw_ref[rows, :].astype(BF16), qt)
        if w == n_back:
            s = s + causal_bias
        else:
            before_start = jnp.where(kb >= 0, 0.0, NEG).astype(F32)
            s = s + (jnp.where(row > tl4, before_start, NEG) if w == 0 else before_start)
        win_tiles.append((s, vwt_ref[kbc]))
    _, l_w, a_w = _softmax_tiles(win_tiles, init)

    n_pair = lax.shift_right_logical(i, 1)
    odd_bias = jnp.where(jnp.bitwise_and(i, 1) == 1, 0.0, NEG).astype(F32)

    def sel_scores(kt, tile_bias):
        rows = pl.ds(pl.multiple_of(kt * Q_BLOCK, Q_BLOCK), Q_BLOCK)
        s = _dot(ks_ref[rows, :].astype(BF16), qt)
        parts = []
        for bb in range(blocks_per_tile):
            bias = selb_ref[pl.ds(kt * blocks_per_tile + bb, 1), :] + tile_bias
            parts.append(s[bb * SEL_LEN:(bb + 1) * SEL_LEN] + bias)
        return jnp.concatenate(parts, axis=0)

    def update_scores(u):
        is_last = u == n_pair
        s_a = sel_scores(jnp.where(is_last, jnp.maximum(i - 1, 0), 2 * u), jnp.where(is_last, odd_bias, 0.0))
        s_b = sel_scores(jnp.where(is_last, i, 2 * u + 1), 0.0) + jnp.where(is_last, causal_bias, 0.0)
        return s_a, s_b

    def sel_body(u, carry):
        m, l, acc, s_a, s_b = carry
        nxt = update_scores(u + 1)
        m, l, acc = _softmax_tiles([(s_a, vst_ref[2 * u]), (s_b, vst_ref[2 * u + 1])], (m, l, acc))
        return (m, l, acc) + nxt

    m_s, l_s, a_s, s_a, s_b = lax.fori_loop(0, n_pair, sel_body, init + update_scores(0))
    _, l_s, a_s = _softmax_tiles([(s_a, vst_ref[jnp.maximum(i - 1, 0)]), (s_b, vst_ref[i])], (m_s, l_s, a_s))

    gt_ref[...] = jax.nn.sigmoid(gl_ref[...]).T
    gate = lambda c: jnp.concatenate(
        [gt_ref[pl.ds((g * HPG + h) * 3 + c, 1), :] for h in range(HPG)], axis=1)
    ot = gate(0) * oc + gate(1) * (a_s * (1.0 / l_s)) + gate(2) * (a_w * (1.0 / l_w))
    for h in range(HPG):
        o_ref[:, h * HEAD_DIM:(h + 1) * HEAD_DIM] = ot[:, h * Q_BLOCK:(h + 1) * Q_BLOCK].T


def _nsa_prompt(z, kcs, vcst, vst, vwt, bsz, t):
    n_chunk = t // CMP_STRIDE
    n_sel = t // SEL_LEN
    n_tile = t // Q_BLOCK
    nqb = t // Q_BLOCK
    qw = HPG * HEAD_DIM
    return pl.pallas_call(
        functools.partial(_nsa_prompt_kernel, n_chunk=n_chunk, n_sel=n_sel),
        grid=(bsz, N_KV, nqb),
        in_specs=[
            pl.BlockSpec((Q_BLOCK, qw), lambda b, g, i: (b * nqb + i, OFF_NQ // qw + g)),
            pl.BlockSpec((Q_BLOCK, LANES), lambda b, g, i: (b * nqb + i, OFF_GL // LANES)),
            pl.BlockSpec((None, None, n_chunk, HEAD_DIM), lambda b, g, i: (b, g, 0, 0)),
            pl.BlockSpec((None, None, HEAD_DIM, n_chunk), lambda b, g, i: (b, g, 0, 0)),
            pl.BlockSpec((t, HEAD_DIM), lambda b, g, i: (b, OFF_KS // HEAD_DIM + g)),
            pl.BlockSpec((None, None, n_tile, HEAD_DIM, HEAD_DIM), lambda b, g, i: (b, g, 0, 0, 0)),
            pl.BlockSpec((t, HEAD_DIM), lambda b, g, i: (b, OFF_KW // HEAD_DIM + g)),
            pl.BlockSpec((None, None, n_tile, HEAD_DIM, HEAD_DIM), lambda b, g, i: (b, g, 0, 0, 0)),
        ],
        out_specs=pl.BlockSpec((Q_BLOCK, qw), lambda b, g, i: (b * nqb + i, g)),
        out_shape=jax.ShapeDtypeStruct((bsz * t, NSA_W), F32),
        scratch_shapes=[
            pltpu.VMEM((n_chunk + 2 * SUBLANES, Q_BLOCK), F32),
            pltpu.VMEM((n_sel, HPG * Q_BLOCK), F32),
            pltpu.VMEM((LANES, Q_BLOCK), F32),
            pltpu.VMEM((n_sel, Q_BLOCK), F32),
        ],
        compiler_params=_cparams(("parallel", "parallel", "arbitrary")),
        name="nsa_prompt",
    )(z, z, kcs, vcst, z, vst, z, vwt)


SAMPLE_TB = 8


def _state_sample_kernel(lg_ref, q_ref, k_ref, v_ref, g_ref, u_ref, sr_ref, sp_ref, wp_ref, ps_ref,
                         or_ref, sn_ref, op_ref, pn_ref, *, pos):
    tb = SAMPLE_TB
    pad = jnp.zeros((SUBLANES - 1, HEAD_DIM), F32)
    for h in range(R_HEADS):
        gamma = jnp.exp(jnp.full((1, HEAD_DIM), lg_ref[h], F32))
        sl = slice(h * HEAD_DIM, (h + 1) * HEAD_DIM)
        rows = []
        for b in range(tb):
            q = q_ref[b:b + 1, sl]
            k = k_ref[b:b + 1, sl]
            v = v_ref[b:b + 1, sl]
            state = sr_ref[b, h]
            k8 = jnp.concatenate([k, pad], axis=0).astype(BF16)
            v8 = jnp.concatenate([v, pad], axis=0).astype(BF16)
            qk = jnp.sum(q * k, axis=1, keepdims=True)
            qs = _dot(jnp.concatenate([q * gamma, pad], axis=0).astype(BF16), state.astype(BF16))[0:1]
            rows.append(qk * v + qs)
            sn_ref[b, h] = gamma[:, 0:1] * state + _dot_tn(k8, v8)
        o = jnp.concatenate(rows, axis=0)
        o = o * lax.rsqrt(jnp.mean(o * o, axis=-1, keepdims=True) + EPS)
        or_ref[:, sl] = o * _silu(g_ref[:, sl])
    u = u_ref[...]
    for gi, w in enumerate(POOL_WINDOWS):
        sl = slice(gi * POOL_GW, (gi + 1) * POOL_GW)
        s = u[:, sl] + jnp.sum(sp_ref[:, POOL_BUF - (w - 1):POOL_BUF, sl], axis=1)
        p = s / float(min(pos + 1, w)) - u[:, sl]
        op_ref[:, sl] = _dot(p.astype(BF16), wp_ref[gi].astype(BF16)) * ps_ref[:, sl]
    pn_ref[:, 0:POOL_BUF - 1, :] = sp_ref[:, 1:POOL_BUF, :]
    for b in range(tb):
        pn_ref[b, POOL_BUF - 1:POOL_BUF, :] = u[b:b + 1, :]


def _state_sample(zs, log_g, state_ret, state_pool, w_pool, pool_scale, layer, dbs, pos):
    tb = SAMPLE_TB
    col = lambda off: pl.BlockSpec((tb, RET_W), lambda i, lg: (i, off // RET_W))
    return pl.pallas_call(
        functools.partial(_state_sample_kernel, pos=pos),
        grid_spec=pltpu.PrefetchScalarGridSpec(
            num_scalar_prefetch=1,
            grid=(dbs // tb,),
            in_specs=[
                col(OFF_RQ), col(OFF_RK), col(OFF_RV), col(OFF_RG), col(OFF_PU),
                pl.BlockSpec((None, tb, R_HEADS, HEAD_DIM, HEAD_DIM), lambda i, lg: (layer, i, 0, 0, 0)),
                pl.BlockSpec((None, tb, POOL_BUF, POOL_W), lambda i, lg: (layer, i, 0, 0)),
                pl.BlockSpec((None, len(POOL_WINDOWS), POOL_GW, POOL_GW), lambda i, lg: (layer, 0, 0, 0)),
                pl.BlockSpec((None, 1, POOL_W), lambda i, lg: (layer, 0, 0)),
            ],
            out_specs=[
                pl.BlockSpec((tb, RET_W), lambda i, lg: (i, 0)),
                pl.BlockSpec((tb, R_HEADS, HEAD_DIM, HEAD_DIM), lambda i, lg: (i, 0, 0, 0)),
                pl.BlockSpec((tb, POOL_W), lambda i, lg: (i, 0)),
                pl.BlockSpec((tb, POOL_BUF, POOL_W), lambda i, lg: (i, 0, 0)),
            ],
        ),
        out_shape=[
            jax.ShapeDtypeStruct((dbs, RET_W), F32),
            jax.ShapeDtypeStruct((dbs, R_HEADS, HEAD_DIM, HEAD_DIM), F32),
            jax.ShapeDtypeStruct((dbs, POOL_W), F32),
            jax.ShapeDtypeStruct((dbs, POOL_BUF, POOL_W), F32),
        ],
        compiler_params=_cparams(("parallel",)),
        name="state_sample",
    )(log_g, zs, zs, zs, zs, zs, state_ret, state_pool, w_pool, pool_scale)


def _softmax_lanes_with_new(s, s_new):
    m = jnp.maximum(jnp.max(s, axis=1, keepdims=True), s_new)
    p = jnp.exp(s - m)
    p_new = jnp.exp(s_new - m)
    inv = 1.0 / (jnp.sum(p, axis=1, keepdims=True) + p_new)
    return p * inv, p_new * inv


def _rank_select(srow):
    across = jnp.broadcast_to(srow, (LANES, LANES))
    down = across.T
    r_i = lax.broadcasted_iota(jnp.int32, (LANES, LANES), 0)
    c_i = lax.broadcasted_iota(jnp.int32, (LANES, LANES), 1)
    beats = jnp.logical_or(down > across, jnp.logical_and(down == across, r_i < c_i))
    rank = jnp.sum(jnp.where(beats, 1.0, 0.0), axis=0, keepdims=True)
    return jnp.logical_and(rank < SEL_TOPK, srow > -jnp.inf)


def _nsa_sample_kernel(pt_ref, q0_ref, q1_ref, ksn_ref, vsn_ref, kwn_ref, vwn_ref, gl_ref, wsel_ref,
                       w1k_ref, w2k_ref, pek_ref, w1v_ref, w2v_ref, pev_ref, wk_ref, wv_ref, *rest,
                       n_pages, past, w_buf, layer, n_older):
    del pt_ref
    pages = rest[:4 * n_pages]
    ck, cv, sk, sv = (pages[c * n_pages:(c + 1) * n_pages] for c in range(4))
    older = rest[4 * n_pages:4 * n_pages + 4 * n_older]
    outs = rest[4 * n_pages + 4 * n_older:]
    o_ref = outs[0]
    scale = HEAD_DIM ** -0.5
    chunks_per_page = PAGE_SIZE // CMP_STRIDE
    n_chunk = n_pages * chunks_per_page
    n_sel = -(-(past + 1) // SEL_LEN)
    assert n_chunk <= LANES and n_sel <= LANES and past % SEL_LEN == 0

    def chunks_of(prefs):
        return jnp.concatenate(
            [jnp.concatenate([p[:, (s * N_KV + g) * HEAD_DIM:(s * N_KV + g + 1) * HEAD_DIM]
                              for s in range(CMP_STRIDE)], axis=1)
             for g in range(N_KV) for p in prefs], axis=0)

    ksum = _compress_core(chunks_of(ck), w1k_ref, w2k_ref, pek_ref)
    vsum = _compress_core(chunks_of(cv), w1v_ref, w2v_ref, pev_ref)

    lane = lax.broadcasted_iota(jnp.int32, (1, LANES), 1)
    q_refs = (q0_ref, q1_ref)
    zpad = jnp.zeros((SUBLANES - HPG, HEAD_DIM), F32)
    q8, oc, imps = [], [], []
    for g in range(N_KV):
        qg = jnp.concatenate([q_refs[g][:, h * HEAD_DIM:(h + 1) * HEAD_DIM] for h in range(HPG)] + [zpad],
                             axis=0) * scale
        q8.append(qg.astype(BF16))
        kcs = ksum[g * n_chunk:(g + 1) * n_chunk]
        vcs = vsum[g * n_chunk:(g + 1) * n_chunk]
        s_c = _dot_nt(q8[g], kcs.astype(BF16))
        c_end = lane[:, :n_chunk] * CMP_STRIDE + (CMP_LEN - 1)
        mask_c = jnp.logical_and(c_end <= past, c_end < n_chunk * CMP_STRIDE)
        m = jnp.max(jnp.where(mask_c, s_c, NEG), axis=1, keepdims=True)
        e = jnp.where(mask_c, jnp.exp(s_c - m), 0.0)
        d = jnp.sum(e, axis=1, keepdims=True)
        pc = e * (1.0 / jnp.where(d > 0, d, 1.0))
        oc.append(_dot(pc.astype(BF16), vcs.astype(BF16)))
        imps.append(jnp.sum(pc[0:HPG], axis=0, keepdims=True))

    imp = jnp.concatenate(imps + [jnp.zeros((SUBLANES - N_KV, n_chunk), F32)], axis=0)
    if n_chunk < LANES:
        imp = jnp.concatenate([imp, jnp.zeros((SUBLANES, LANES - n_chunk), F32)], axis=1)
    hi = imp.astype(BF16)
    r1 = imp - hi.astype(F32)
    mid = r1.astype(BF16)
    lo = (r1 - mid.astype(F32)).astype(BF16)
    wsel = wsel_ref[...]
    p_slc = _dot(hi, wsel) + _dot(mid, wsel) + _dot(lo, wsel)
    cur = past // SEL_LEN
    blk_valid = jnp.logical_and(lane * SEL_LEN <= past, lane < n_sel)
    forced = jnp.logical_or(lane == 0, jnp.logical_and(lane <= cur, lane > cur - N_LOCAL_SEL))
    score = jnp.where(blk_valid, jnp.where(forced, jnp.inf, p_slc), -jnp.inf)

    gate = jax.nn.sigmoid(gl_ref[...])
    blocks_per_page = PAGE_SIZE // SEL_LEN
    wl = lax.broadcasted_iota(jnp.int32, (1, w_buf), 1)
    dpos = w_buf - wl
    bias_w = jnp.where(jnp.logical_and(jnp.logical_and(dpos >= 0, dpos < WINDOW), past - dpos >= 0), 0.0, NEG)
    for g in range(N_KV):
        kv_new = slice(g * HEAD_DIM, (g + 1) * HEAD_DIM)
        q_f = q8[g].astype(F32)
        selb = jnp.where(_rank_select(score[g:g + 1]), 0.0, NEG)
        bias_pages = []
        for p in range(n_pages):
            bias = selb[:, p * blocks_per_page:p * blocks_per_page + 1]
            for bb in range(1, blocks_per_page):
                bias = jnp.where(lane < bb * SEL_LEN, bias, selb[:, p * blocks_per_page + bb:p * blocks_per_page + bb + 1])
            bias_pages.append(bias)
        k_all = jnp.concatenate([p[:, kv_new] for p in sk], axis=0).astype(BF16)
        v_all = jnp.concatenate([p[:, kv_new] for p in sv], axis=0).astype(BF16)
        s_all = _dot_nt(q8[g], k_all) + jnp.concatenate(bias_pages, axis=1)
        s_new = jnp.sum(q_f * ksn_ref[:, kv_new].astype(BF16).astype(F32), axis=1, keepdims=True)
        p_all, p_new = _softmax_lanes_with_new(s_all, s_new)
        o_s = p_new * vsn_ref[:, kv_new] + _dot(p_all.astype(BF16), v_all)
        s_w = _dot_nt(q8[g], wk_ref[:, kv_new].astype(BF16)) + bias_w
        s_wn = jnp.sum(q_f * kwn_ref[:, kv_new].astype(BF16).astype(F32), axis=1, keepdims=True)
        p_w, p_wn = _softmax_lanes_with_new(s_w, s_wn)
        o_w = p_wn * vwn_ref[:, kv_new] + _dot(p_w.astype(BF16), wv_ref[:, kv_new].astype(BF16))
        for h in range(HPG):
            c0 = (g * HPG + h) * 3
            og = (gate[:, c0:c0 + 1] * oc[g][h:h + 1] + gate[:, c0 + 1:c0 + 2] * o_s[h:h + 1]
                  + gate[:, c0 + 2:c0 + 3] * o_w[h:h + 1])
            o_ref[:, (g * HPG + h) * HEAD_DIM:(g * HPG + h + 1) * HEAD_DIM] = og

    if len(outs) > 1:
        wko_ref, wvo_ref = outs[1:]
        shifts = [(layer, kwn_ref, vwn_ref, wk_ref, wv_ref)]
        shifts += [(lo,) + tuple(older[4 * lo:4 * lo + 4]) for lo in range(n_older)]
        for lo, kn, vn, wk, wv in shifts:
            wko_ref[lo, 0:w_buf - 1, :] = wk[1:w_buf, :]
            wvo_ref[lo, 0:w_buf - 1, :] = wv[1:w_buf, :]
            wko_ref[lo, w_buf - 1:w_buf, :] = kn[...]
            wvo_ref[lo, w_buf - 1:w_buf, :] = vn[...]


def _nsa_sample(zs3, page_table, cmp_pages, slc_pages, win_k, win_v, cw, wsel, layer, dbs, past, w_buf, older_zs3):
    n_pages = past // PAGE_SIZE
    depth = win_k.shape[0]
    n_older = 0 if older_zs3 is None else len(older_zs3)
    assert older_zs3 is None or n_older == layer == depth - 1
    rowblk = lambda width, off: pl.BlockSpec((None, 1, width), lambda b, pt: (b, 0, off // width))
    const3 = lambda shp: pl.BlockSpec((None,) + shp, lambda b, pt: (layer,) + (0,) * len(shp))
    page = lambda shp, p: pl.BlockSpec((None, None) + shp, lambda b, pt: (layer, pt[b * n_pages + p], 0, 0))
    cmp_shape = (PAGE_SIZE // CMP_STRIDE, CMP_STRIDE * KV_W)
    slc_shape = (PAGE_SIZE, KV_W)
    win = pl.BlockSpec((None, None, w_buf, KV_W), lambda b, pt: (layer, b, 0, 0))
    in_specs = [
        rowblk(HPG * HEAD_DIM, OFF_NQ), rowblk(HPG * HEAD_DIM, OFF_NQ + HPG * HEAD_DIM),
        rowblk(KV_W, OFF_KS), rowblk(KV_W, OFF_VS), rowblk(KV_W, OFF_KW), rowblk(KV_W, OFF_VW),
        rowblk(LANES, OFF_GL),
        pl.BlockSpec((LANES, LANES), lambda b, pt: (0, 0)),
        const3((CHUNK_W, 2 * HEAD_DIM)), const3((HEAD_DIM, HEAD_DIM)), const3((SUBLANES, CHUNK_W)),
        const3((CHUNK_W, 2 * HEAD_DIM)), const3((HEAD_DIM, HEAD_DIM)), const3((SUBLANES, CHUNK_W)),
        win, win,
    ] + [page(cmp_shape, p) for _ in range(2) for p in range(n_pages)] \
      + [page(slc_shape, p) for _ in range(2) for p in range(n_pages)]
    operands = [zs3] * 7 + [wsel, cw["w1k"], cw["w2k"], cw["pek"], cw["w1v"], cw["w2v"], cw["pev"], win_k, win_v]
    for c in list(cmp_pages) + list(slc_pages):
        operands += [c] * n_pages
    for lo in range(n_older):
        win_lo = pl.BlockSpec((None, None, w_buf, KV_W), lambda b, pt, lo=lo: (lo, b, 0, 0))
        operands += [older_zs3[lo], older_zs3[lo], win_k, win_v]
        in_specs += [rowblk(KV_W, OFF_KW), rowblk(KV_W, OFF_VW), win_lo, win_lo]
    out_specs = [pl.BlockSpec((None, 1, NSA_W), lambda b, pt: (b, 0, 0))]
    out_shape = [jax.ShapeDtypeStruct((dbs, 1, NSA_W), F32)]
    if older_zs3 is not None:
        out_specs += [pl.BlockSpec((depth, None, w_buf, KV_W), lambda b, pt: (0, b, 0, 0))] * 2
        out_shape += [jax.ShapeDtypeStruct((depth, dbs, w_buf, KV_W), F32)] * 2
    return pl.pallas_call(
        functools.partial(_nsa_sample_kernel, n_pages=n_pages, past=past, w_buf=w_buf, layer=layer,
                          n_older=n_older),
        grid_spec=pltpu.PrefetchScalarGridSpec(
            num_scalar_prefetch=1,
            grid=(dbs,),
            in_specs=in_specs,
            out_specs=out_specs,
        ),
        out_shape=out_shape,
        compiler_params=_cparams(("parallel",)),
        name="nsa_sample",
    )(page_table.reshape(-1), *operands)


def _rope_tables(pos):
    half = HEAD_DIM // 2
    inv = ROPE_THETA ** (-jnp.arange(half, dtype=F32) / half)
    ang = pos.astype(F32)[:, None] * inv[None, :]
    cos, sin = jnp.cos(ang), jnp.sin(ang)
    return jnp.concatenate([cos, cos], axis=-1), jnp.concatenate([-sin, sin], axis=-1)


def _block_sum_matrix():
    w = np.zeros((LANES, LANES), np.float32)
    for j in range(LANES):
        for mm in range(SEL_RATIO):
            for nn in range(CMP_RATIO):
                c = SEL_RATIO * j + mm - nn
                if 0 <= c < LANES:
                    w[c, j] += 1.0
    return w


def _row_tile(m, cap):
    tm = min(m, cap)
    while m % tm:
        tm //= 2
    return tm


def kernel(x_prompt, x_sample, state_ret, state_pool, cache_cmp_k, cache_cmp_v, cache_slc_k, cache_slc_v,
           cache_win_k, cache_win_v, page_table, norm1, w_in, w_pool, pool_scale, cmp_w1_k, cmp_w2_k, cmp_pe_k,
           cmp_w1_v, cmp_w2_v, cmp_pe_v, w_out, norm2, w_up, w_down, norm_f):
    bsz, t, _ = x_prompt.shape
    dbs, dec_seq, _ = x_sample.shape
    depth = w_in.shape[0]
    assert dec_seq == 1 and t % 512 == 0 and dbs % SAMPLE_TB == 0
    n_pages = page_table.shape[1]
    past = n_pages * PAGE_SIZE
    n_pool = cache_cmp_k.shape[1]
    w_buf = cache_win_k.shape[2]
    wlen = min(WINDOW, t)

    w_in_bf = jnp.pad(w_in, ((0, 0), (0, 0), (0, N_PAD - D_IN))).astype(BF16)
    w_out_bf, w_up_bf, w_down_bf = w_out.astype(BF16), w_up.astype(BF16), w_down.astype(BF16)
    w1cat = lambda w1: jnp.concatenate([w1[:, :CMP_STRIDE], w1[:, CMP_STRIDE:]], axis=-1).astype(BF16).reshape(
        depth, CHUNK_W, 2 * HEAD_DIM)
    pe_rows = lambda pe: jnp.pad(pe.reshape(depth, CMP_RATIO, CHUNK_W), ((0, 0), (0, SUBLANES - CMP_RATIO), (0, 0)))
    cw = dict(w1k=w1cat(cmp_w1_k), w2k=cmp_w2_k.astype(BF16), pek=pe_rows(cmp_pe_k),
              w1v=w1cat(cmp_w1_v), w2v=cmp_w2_v.astype(BF16), pev=pe_rows(cmp_pe_v))
    norm1_3, norm2_3 = norm1[:, None, :], norm2[:, None, :]
    pool_scale_3 = pool_scale[:, None, :]
    norm_f2 = norm_f[None, :]
    log_g = jnp.log1p(-jnp.exp2(-5.0 - jnp.arange(R_HEADS, dtype=F32)))
    cos_p, sin_p = _rope_tables(jnp.arange(t))
    cos_s, sin_s = _rope_tables(jnp.full((dbs,), past, jnp.int32))
    wsel = jnp.asarray(_block_sum_matrix(), BF16)
    cmp_pages = [c.reshape(depth, n_pool, PAGE_SIZE // CMP_STRIDE, CMP_STRIDE * KV_W)
                 for c in (cache_cmp_k, cache_cmp_v)]
    slc_pages = [c.reshape(depth, n_pool, PAGE_SIZE, KV_W) for c in (cache_slc_k, cache_slc_v)]
    win_k = cache_win_k.reshape(depth, dbs, w_buf, KV_W)
    win_v = cache_win_v.reshape(depth, dbs, w_buf, KV_W)

    mp = bsz * t
    tm_p = _row_tile(t, 1024)
    tm_mlp = _row_tile(t, 512)
    xp = x_prompt.reshape(mp, D_MODEL)
    xs = x_sample.reshape(dbs, D_MODEL)
    outs = {k: [] for k in ("ret_p", "ret_s", "pool_p", "pool_s", "ck_p", "ck_s", "cv_p", "cv_s", "sk_p", "sk_s",
                            "sv_p", "sv_s", "wk_p", "wv_p")}
    zs3_all = []
    for l in range(depth):
        last = l == depth - 1
        z = _inproj(xp, norm1_3, w_in_bf, l, cos_p, sin_p, tm_p)
        o_r, s_new = _ret_prompt(z, log_g, bsz, t)
        o_p = _pool_prompt(z, w_pool, pool_scale_3, l, bsz, t)
        kcs, vcst, vst, vwt = _prep_prompt(z, cw, l, bsz, t)
        o_n = _nsa_prompt(z, kcs, vcst, vst, vwt, bsz, t)
        xp = _outproj(o_r, o_p, o_n, w_out_bf, l, xp, tm_p)
        xp = _mlp(xp, norm2_3, w_up_bf, w_down_bf, l, norm_f2, last, tm_mlp)
        z3 = z.reshape(bsz, t, N_PAD)
        kv = lambda off: z3[:, :, off:off + KV_W].reshape(bsz, t, N_KV, HEAD_DIM)
        outs["ret_p"].append(s_new)
        outs["pool_p"].append(z3[:, t - POOL_BUF:, OFF_PU:OFF_PU + POOL_W])
        outs["ck_p"].append(kv(OFF_KC))
        outs["cv_p"].append(kv(OFF_VC))
        outs["sk_p"].append(kv(OFF_KS))
        outs["sv_p"].append(kv(OFF_VS))
        outs["wk_p"].append(kv(OFF_KW)[:, t - wlen:])
        outs["wv_p"].append(kv(OFF_VW)[:, t - wlen:])
        zs = _inproj(xs, norm1_3, w_in_bf, l, cos_s, sin_s, dbs)
        o_r, s_new, o_p, buf_new = _state_sample(zs, log_g, state_ret, state_pool, w_pool, pool_scale_3, l, dbs, past)
        zs3 = zs.reshape(dbs, 1, N_PAD)
        res = _nsa_sample(zs3, page_table, cmp_pages, slc_pages, win_k, win_v, cw, wsel, l, dbs, past, w_buf,
                          zs3_all if last else None)
        o_n = res[0]
        zs3_all.append(zs3)
        if last:
            wk_s, wv_s = res[1:]
        xs = _outproj(o_r, o_p, o_n.reshape(dbs, NSA_W), w_out_bf, l, xs, dbs)
        xs = _mlp(xs, norm2_3, w_up_bf, w_down_bf, l, norm_f2, last, dbs)
        kvs = lambda off: zs[:, off:off + KV_W].reshape(dbs, 1, N_KV, HEAD_DIM)
        outs["ret_s"].append(s_new)
        outs["pool_s"].append(buf_new)
        outs["ck_s"].append(kvs(OFF_KC))
        outs["cv_s"].append(kvs(OFF_VC))
        outs["sk_s"].append(kvs(OFF_KS))
        outs["sv_s"].append(kvs(OFF_VS))
    st = lambda k: jnp.stack(outs[k])
    win_out = lambda w: w.reshape(depth, dbs, w_buf, N_KV, HEAD_DIM)
    return (xp.reshape(bsz, t, D_MODEL), xs.reshape(dbs, 1, D_MODEL),
            st("ret_p"), st("ret_s"), st("pool_p"), st("pool_s"),
            st("ck_p"), st("ck_s"), st("cv_p"), st("cv_s"), st("sk_p"), st("sk_s"), st("sv_p"), st("sv_s"),
            st("wk_p"), win_out(wk_s), st("wv_p"), win_out(wv_s))
```

```python
import functools

import numpy as np
import jax
import jax.numpy as jnp
from jax import lax
from jax.experimental import pallas as pl
from jax.experimental.pallas import tpu as pltpu

F32 = jnp.float32
BF16 = jnp.bfloat16

D_MODEL = 2048
HEAD_DIM = 128
PAGE_SIZE = 128
RET_W = D_MODEL // 4
POOL_W = D_MODEL // 4
NSA_W = D_MODEL - RET_W - POOL_W
R_HEADS = RET_W // HEAD_DIM
N_HEADS = NSA_W // HEAD_DIM
N_KV = max(1, N_HEADS // 4)
HPG = N_HEADS // N_KV
KV_W = N_KV * HEAD_DIM
POOL_WINDOWS = (2, 4, 8, 16)
POOL_GW = POOL_W // len(POOL_WINDOWS)
POOL_BUF = max(POOL_WINDOWS) - 1
RET_CHUNK = 128
CMP_LEN = 32
CMP_STRIDE = 16
CMP_RATIO = CMP_LEN // CMP_STRIDE
SEL_LEN = 64
SEL_RATIO = SEL_LEN // CMP_STRIDE
SEL_TOPK = 16
N_LOCAL_SEL = 2
WINDOW = 512
Q_BLOCK = 128
D_FF = 4 * D_MODEL
ROPE_THETA = 10000.0
EPS = 1e-6
D_IN = 4 * RET_W + POOL_W + NSA_W + 6 * KV_W + 3 * N_HEADS

OFF_RQ = 0
OFF_RK = OFF_RQ + RET_W
OFF_RV = OFF_RK + RET_W
OFF_RG = OFF_RV + RET_W
OFF_PU = OFF_RG + RET_W
OFF_NQ = OFF_PU + POOL_W
OFF_KC = OFF_NQ + NSA_W
OFF_VC = OFF_KC + KV_W
OFF_KS = OFF_VC + KV_W
OFF_VS = OFF_KS + KV_W
OFF_KW = OFF_VS + KV_W
OFF_VW = OFF_KW + KV_W
OFF_GL = OFF_VW + KV_W
PROJ_TN = 512
PROJ_ROW_CHUNK = 256
N_PAD = -(-D_IN // PROJ_TN) * PROJ_TN
ROPE_FULL_TILES = (OFF_RQ // PROJ_TN, OFF_RK // PROJ_TN, OFF_NQ // PROJ_TN, OFF_NQ // PROJ_TN + 1)
ROPE_HALF_TILES = (OFF_KC // PROJ_TN, OFF_KS // PROJ_TN, OFF_KW // PROJ_TN)
RK_TILE = OFF_RK // PROJ_TN

NEG = -1e30
VMEM_LIMIT = 48 * 1024 * 1024
LANES = 128
SUBLANES = 8


def _cparams(sem):
    return pltpu.CompilerParams(dimension_semantics=sem, vmem_limit_bytes=VMEM_LIMIT)


def _dot(a, b):
    return jnp.dot(a, b, preferred_element_type=F32)


def _dot_nt(a, b):
    return lax.dot_general(a, b, (((1,), (1,)), ((), ())), preferred_element_type=F32)


def _dot_tn(a, b):
    return lax.dot_general(a, b, (((0,), (0,)), ((), ())), preferred_element_type=F32)


def _rms(x, g):
    return x * lax.rsqrt(jnp.mean(x * x, axis=-1, keepdims=True) + EPS) * g


def _silu(x):
    return x * jax.nn.sigmoid(x)


def _inproj_kernel(x_ref, g_ref, w_ref, cos_ref, sin_ref, z_ref, hn_ref):
    j = pl.program_id(1)

    @pl.when(j == 0)
    def _():
        hn_ref[...] = _rms(x_ref[...], g_ref[...]).astype(BF16)

    is_full = functools.reduce(jnp.logical_or, [j == t for t in ROPE_FULL_TILES])
    is_half = functools.reduce(jnp.logical_or, [j == t for t in ROPE_HALF_TILES])
    scale = jnp.where(j == RK_TILE, HEAD_DIM ** -0.5, 1.0).astype(F32)
    is_any = jnp.logical_or(is_full, is_half)
    tm = x_ref.shape[0]
    rc = min(tm, PROJ_ROW_CHUNK)
    for r0 in range(0, tm, rc):
        rows = slice(r0, r0 + rc)
        z = _dot(hn_ref[rows, :], w_ref[...])
        cos_t, sin_t = cos_ref[rows, :] * scale, sin_ref[rows, :] * scale
        tab_lead = (jnp.where(is_any, cos_t, 1.0), jnp.where(is_any, sin_t, 0.0))
        tab_rest = (jnp.where(is_full, cos_t, 1.0), jnp.where(is_full, sin_t, 0.0))
        for s in range(PROJ_TN // HEAD_DIM):
            sl = slice(s * HEAD_DIM, (s + 1) * HEAD_DIM)
            c, sn = tab_lead if s < KV_W // HEAD_DIM else tab_rest
            zs = z[:, sl]
            z_ref[rows, sl] = zs * c + pltpu.roll(zs, HEAD_DIM // 2, 1) * sn


def _inproj(x, gamma, w_in_bf, layer, cos_t, sin_t, tm):
    m = x.shape[0]
    n_row_tiles = cos_t.shape[0] // tm
    return pl.pallas_call(
        _inproj_kernel,
        grid=(m // tm, N_PAD // PROJ_TN),
        in_specs=[
            pl.BlockSpec((tm, D_MODEL), lambda i, j: (i, 0)),
            pl.BlockSpec((None, 1, D_MODEL), lambda i, j: (layer, 0, 0)),
            pl.BlockSpec((None, D_MODEL, PROJ_TN), lambda i, j: (layer, 0, j)),
            pl.BlockSpec((tm, HEAD_DIM), lambda i, j: (i % n_row_tiles, 0)),
            pl.BlockSpec((tm, HEAD_DIM), lambda i, j: (i % n_row_tiles, 0)),
        ],
        out_specs=pl.BlockSpec((tm, PROJ_TN), lambda i, j: (i, j)),
        out_shape=jax.ShapeDtypeStruct((m, N_PAD), F32),
        scratch_shapes=[pltpu.VMEM((tm, D_MODEL), BF16)],
        compiler_params=_cparams(("parallel", "arbitrary")),
        name="inproj",
    )(x, gamma, w_in_bf, cos_t, sin_t)


def _outproj_kernel(r_ref, p_ref, n_ref, w_ref, x_ref, o_ref):
    acc = _dot(r_ref[...].astype(BF16), w_ref[0:RET_W, :])
    acc += _dot(p_ref[...].astype(BF16), w_ref[RET_W:RET_W + POOL_W, :])
    acc += _dot(n_ref[...].astype(BF16), w_ref[RET_W + POOL_W:, :])
    o_ref[...] = x_ref[...] + acc


def _outproj(o_r, o_p, o_n, w_out_bf, layer, x, tm, tn=1024):
    m = x.shape[0]
    return pl.pallas_call(
        _outproj_kernel,
        grid=(m // tm, D_MODEL // tn),
        in_specs=[
            pl.BlockSpec((tm, RET_W), lambda i, j: (i, 0)),
            pl.BlockSpec((tm, POOL_W), lambda i, j: (i, 0)),
            pl.BlockSpec((tm, NSA_W), lambda i, j: (i, 0)),
            pl.BlockSpec((None, D_MODEL, tn), lambda i, j: (layer, 0, j)),
            pl.BlockSpec((tm, tn), lambda i, j: (i, j)),
        ],
        out_specs=pl.BlockSpec((tm, tn), lambda i, j: (i, j)),
        out_shape=jax.ShapeDtypeStruct((m, D_MODEL), F32),
        compiler_params=_cparams(("parallel", "parallel")),
        name="outproj",
    )(o_r, o_p, o_n, w_out_bf, x)


def _mlp_kernel(x_ref, g_ref, wu_ref, wd_ref, gf_ref, o_ref, hn_ref, *, final_norm):
    f = pl.program_id(1)

    @pl.when(f == 0)
    def _():
        hn_ref[...] = _rms(x_ref[...], g_ref[...]).astype(BF16)
        o_ref[...] = jnp.zeros_like(o_ref)

    u = jnp.maximum(_dot(hn_ref[...], wu_ref[...]), 0.0)
    o_ref[...] += _dot((u * u).astype(BF16), wd_ref[...])

    @pl.when(f == pl.num_programs(1) - 1)
    def _():
        y = x_ref[...] + o_ref[...]
        o_ref[...] = _rms(y, gf_ref[...]) if final_norm else y


MLP_VMEM_LIMIT = 58 * 1024 * 1024


def _mlp(x, gamma, w_up_bf, w_down_bf, layer, gamma_f, final_norm, tm, tf=512):
    m = x.shape[0]
    return pl.pallas_call(
        functools.partial(_mlp_kernel, final_norm=final_norm),
        grid=(m // tm, D_FF // tf),
        in_specs=[
            pl.BlockSpec((tm, D_MODEL), lambda i, f: (i, 0)),
            pl.BlockSpec((None, 1, D_MODEL), lambda i, f: (layer, 0, 0)),
            pl.BlockSpec((None, D_MODEL, tf), lambda i, f: (layer, 0, f)),
            pl.BlockSpec((None, tf, D_MODEL), lambda i, f: (layer, f, 0)),
            pl.BlockSpec((1, D_MODEL), lambda i, f: (0, 0)),
        ],
        out_specs=pl.BlockSpec((tm, D_MODEL), lambda i, f: (i, 0)),
        out_shape=jax.ShapeDtypeStruct((m, D_MODEL), F32),
        scratch_shapes=[pltpu.VMEM((tm, D_MODEL), BF16)],
        compiler_params=pltpu.CompilerParams(dimension_semantics=("parallel", "arbitrary"),
                                             vmem_limit_bytes=MLP_VMEM_LIMIT),
        name="mlp",
    )(x, gamma, w_up_bf, w_down_bf, gamma_f)


def _ret_prompt_kernel(lg_ref, q_ref, k_ref, v_ref, g_ref, o_ref, s_ref, *, n_chunk):
    lg = lg_ref[pl.program_id(1)]
    L = RET_CHUNK
    ri = lax.broadcasted_iota(jnp.int32, (L, L), 0)
    ci = lax.broadcasted_iota(jnp.int32, (L, L), 1)
    diff = (ri - ci).astype(F32)
    causal = diff >= 0
    decay = jnp.where(causal, jnp.exp(jnp.where(causal, diff, 0.0) * lg), 0.0)
    idx = lax.broadcasted_iota(jnp.int32, (L, 1), 0).astype(F32)
    q_dec = jnp.exp((idx + 1.0) * lg)
    k_dec = jnp.exp((L - 1.0 - idx) * lg)
    s_dec = jnp.exp(jnp.full((1, 1), float(L), F32) * lg)

    def body(c, state):
        rows = pl.ds(pl.multiple_of(c * L, L), L)
        q, k, v = q_ref[rows, :], k_ref[rows, :], v_ref[rows, :]
        vb = v.astype(BF16)
        att = _dot_nt(q.astype(BF16), k.astype(BF16)) * decay
        o = _dot(att.astype(BF16), vb) + _dot((q * q_dec).astype(BF16), state.astype(BF16))
        state = s_dec * state + _dot_tn((k * k_dec).astype(BF16), vb)
        o = o * lax.rsqrt(jnp.mean(o * o, axis=-1, keepdims=True) + EPS)
        o_ref[rows, :] = o * _silu(g_ref[rows, :])
        return state

    s_ref[...] = lax.fori_loop(0, n_chunk, body, jnp.zeros((HEAD_DIM, HEAD_DIM), F32))


def _ret_prompt(z, log_g, bsz, t):
    cb = lambda off: (lambda b, h, lg: (b, off // HEAD_DIM + h))
    blk = lambda off: pl.BlockSpec((t, HEAD_DIM), cb(off))
    return pl.pallas_call(
        functools.partial(_ret_prompt_kernel, n_chunk=t // RET_CHUNK),
        grid_spec=pltpu.PrefetchScalarGridSpec(
            num_scalar_prefetch=1,
            grid=(bsz, R_HEADS),
            in_specs=[blk(OFF_RQ), blk(OFF_RK), blk(OFF_RV), blk(OFF_RG)],
            out_specs=[
                pl.BlockSpec((t, HEAD_DIM), lambda b, h, lg: (b, h)),
                pl.BlockSpec((None, None, HEAD_DIM, HEAD_DIM), lambda b, h, lg: (b, h, 0, 0)),
            ],
        ),
        out_shape=[
            jax.ShapeDtypeStruct((bsz * t, RET_W), F32),
            jax.ShapeDtypeStruct((bsz, R_HEADS, HEAD_DIM, HEAD_DIM), F32),
        ],
        compiler_params=_cparams(("parallel", "parallel")),
        name="ret_prompt",
    )(log_g, z, z, z, z)


POOL_HALO = 16


def _pool_prompt_kernel(u_ref, wp_ref, ps_ref, o_ref, ext_ref, *, tt):
    t = pl.program_id(1)

    @pl.when(t == 0)
    def _():
        ext_ref[0:POOL_HALO, :] = jnp.zeros((POOL_HALO, POOL_W), F32)

    u = u_ref[...]
    ext_ref[POOL_HALO:POOL_HALO + tt, :] = u
    pos = t * tt + lax.broadcasted_iota(jnp.int32, (tt, 1), 0)
    for g, w in enumerate(POOL_WINDOWS):
        sl = slice(g * POOL_GW, (g + 1) * POOL_GW)
        s = u[:, sl]
        for i in range(1, w):
            s = s + ext_ref[pl.ds(POOL_HALO - i, tt), sl]
        cnt = jnp.minimum(pos + 1, w).astype(F32)
        p = s / cnt - u[:, sl]
        o_ref[:, sl] = _dot(p.astype(BF16), wp_ref[g].astype(BF16)) * ps_ref[:, sl]
    ext_ref[0:POOL_HALO, :] = ext_ref[tt:tt + POOL_HALO, :]


def _pool_prompt(z, w_pool, pool_scale, layer, bsz, t, tt=512):
    nt = t // tt
    return pl.pallas_call(
        functools.partial(_pool_prompt_kernel, tt=tt),
        grid=(bsz, nt),
        in_specs=[
            pl.BlockSpec((tt, POOL_W), lambda b, i: (b * nt + i, OFF_PU // POOL_W)),
            pl.BlockSpec((None, len(POOL_WINDOWS), POOL_GW, POOL_GW), lambda b, i: (layer, 0, 0, 0)),
            pl.BlockSpec((None, 1, POOL_W), lambda b, i: (layer, 0, 0)),
        ],
        out_specs=pl.BlockSpec((tt, POOL_W), lambda b, i: (b * nt + i, 0)),
        out_shape=jax.ShapeDtypeStruct((bsz * t, POOL_W), F32),
        scratch_shapes=[pltpu.VMEM((POOL_HALO + tt, POOL_W), F32)],
        compiler_params=_cparams(("parallel", "arbitrary")),
        name="pool_prompt",
    )(z, w_pool, pool_scale)


CHUNK_W = CMP_STRIDE * HEAD_DIM


def _compress_core(chunks, w1_ref, w2_ref, pe_ref):
    n_rows = chunks.shape[0]
    acc = _dot(jnp.concatenate([chunks, pe_ref[...]], axis=0).astype(BF16), w1_ref[...])
    first = acc[:n_rows, :HEAD_DIM]
    second = acc[:n_rows, HEAD_DIM:]
    pe_term = acc[n_rows:n_rows + 1, :HEAD_DIM] + acc[n_rows + 1:n_rows + 2, HEAD_DIM:]
    h = first + pltpu.roll(second, n_rows - 1, 0) + pe_term
    return _dot(_silu(h).astype(BF16), w2_ref[...])


def _prep_prompt_kernel(kc_ref, vc_ref, vs_ref, vw_ref, w1k_ref, w2k_ref, pek_ref, w1v_ref, w2v_ref, pev_ref,
                        kcs_ref, vcst_ref, vst_ref, vwt_ref, *, n_chunk, n_tile):
    chunks_of = lambda r: jnp.concatenate(
        [r[pl.ds(s, n_chunk, stride=CMP_STRIDE), :] for s in range(CMP_STRIDE)], axis=1)
    kcs_ref[...] = _compress_core(chunks_of(kc_ref), w1k_ref, w2k_ref, pek_ref).astype(BF16)
    vcst_ref[...] = _compress_core(chunks_of(vc_ref), w1v_ref, w2v_ref, pev_ref).T.astype(BF16)

    def tbody(i, carry):
        rows = pl.ds(pl.multiple_of(i * HEAD_DIM, HEAD_DIM), HEAD_DIM)
        vst_ref[i] = vs_ref[rows, :].T.astype(BF16)
        vwt_ref[i] = vw_ref[rows, :].T.astype(BF16)
        return carry

    lax.fori_loop(0, n_tile, tbody, 0)


def _prep_prompt(z, cw, layer, bsz, t):
    n_chunk = t // CMP_STRIDE
    n_tile = t // HEAD_DIM
    col = lambda off: pl.BlockSpec((t, HEAD_DIM), lambda b, g: (b, off // HEAD_DIM + g))
    w1 = pl.BlockSpec((None, CHUNK_W, 2 * HEAD_DIM), lambda b, g: (layer, 0, 0))
    w2 = pl.BlockSpec((None, HEAD_DIM, HEAD_DIM), lambda b, g: (layer, 0, 0))
    pe = pl.BlockSpec((None, SUBLANES, CHUNK_W), lambda b, g: (layer, 0, 0))
    return pl.pallas_call(
        functools.partial(_prep_prompt_kernel, n_chunk=n_chunk, n_tile=n_tile),
        grid=(bsz, N_KV),
        in_specs=[col(OFF_KC), col(OFF_VC), col(OFF_VS), col(OFF_VW), w1, w2, pe, w1, w2, pe],
        out_specs=[
            pl.BlockSpec((None, None, n_chunk, HEAD_DIM), lambda b, g: (b, g, 0, 0)),
            pl.BlockSpec((None, None, HEAD_DIM, n_chunk), lambda b, g: (b, g, 0, 0)),
            pl.BlockSpec((None, None, n_tile, HEAD_DIM, HEAD_DIM), lambda b, g: (b, g, 0, 0, 0)),
            pl.BlockSpec((None, None, n_tile, HEAD_DIM, HEAD_DIM), lambda b, g: (b, g, 0, 0, 0)),
        ],
        out_shape=[
            jax.ShapeDtypeStruct((bsz, N_KV, n_chunk, HEAD_DIM), BF16),
            jax.ShapeDtypeStruct((bsz, N_KV, HEAD_DIM, n_chunk), BF16),
            jax.ShapeDtypeStruct((bsz, N_KV, n_tile, HEAD_DIM, HEAD_DIM), BF16),
            jax.ShapeDtypeStruct((bsz, N_KV, n_tile, HEAD_DIM, HEAD_DIM), BF16),
        ],
        compiler_params=_cparams(("parallel", "parallel")),
        name="prep_prompt",
    )(z, z, z, z, cw["w1k"], cw["w2k"], cw["pek"], cw["w1v"], cw["w2v"], cw["pev"])


LOG2E = 1.4426950408889634


def _softmax_tiles(tiles, carry):
    m, l, acc = carry
    m_new = m
    for s, _ in tiles:
        m_new = jnp.maximum(m_new, jnp.max(s, axis=0, keepdims=True))
    alpha = jnp.exp2(m - m_new)
    l = alpha * l
    acc = alpha * acc
    for s, vt in tiles:
        p = jnp.exp2(s - m_new)
        l = l + jnp.sum(p, axis=0, keepdims=True)
        acc = acc + _dot(vt, p.astype(BF16))
    return m_new, l, acc


def _select_blocks(score, score_ref):
    n_blocks = score.shape[0]
    score_ref[...] = score
    groups = [score[v * SUBLANES:(v + 1) * SUBLANES] for v in range(n_blocks // SUBLANES)]
    jb = lax.broadcasted_iota(jnp.int32, groups[0].shape, 0)
    ranks = [jnp.zeros(gv.shape, F32) for gv in groups]
    for k in range(n_blocks):
        rk = score_ref[pl.ds(k, 1), :]
        for v, gv in enumerate(groups):
            if v * SUBLANES > k:
                beats = rk >= gv
            elif (v + 1) * SUBLANES - 1 <= k:
                beats = rk > gv
            else:
                beats = jnp.logical_or(rk > gv, jnp.logical_and(jb + v * SUBLANES > k, rk == gv))
            ranks[v] = ranks[v] + jnp.where(beats, 1.0, 0.0)
    rank = jnp.concatenate(ranks, axis=0)
    return jnp.logical_and(rank < SEL_TOPK, score > -jnp.inf)


def _nsa_prompt_kernel(q_ref, gl_ref, kcs_ref, vcst_ref, ks_ref, vst_ref, kw_ref, vwt_ref, o_ref,
                       imp_ref, selb_ref, gt_ref, score_ref, *, n_chunk, n_sel):
    g = pl.program_id(1)
    i = pl.program_id(2)
    qw = HPG * Q_BLOCK
    qscale = HEAD_DIM ** -0.5 * LOG2E
    q = q_ref[...]
    qt = jnp.concatenate([(q[:, h * HEAD_DIM:(h + 1) * HEAD_DIM] * qscale).T for h in range(HPG)],
                         axis=1).astype(BF16)
    lane = lax.broadcasted_iota(jnp.int32, (1, qw), 1)
    tl4 = jnp.bitwise_and(lane, Q_BLOCK - 1)
    tpos4 = i * Q_BLOCK + tl4
    tpos = i * Q_BLOCK + lax.broadcasted_iota(jnp.int32, (1, Q_BLOCK), 1)

    sc = _dot(kcs_ref[...], qt)
    c_end = lax.broadcasted_iota(jnp.int32, (n_chunk, 1), 0) * CMP_STRIDE + (CMP_LEN - 1)
    mask_c = jnp.logical_and(c_end <= tpos4, c_end < n_chunk * CMP_STRIDE)
    m = jnp.max(jnp.where(mask_c, sc, NEG), axis=0, keepdims=True)
    e = jnp.where(mask_c, jnp.exp2(sc - m), 0.0)
    d = jnp.sum(e, axis=0, keepdims=True)
    pc = e * (1.0 / jnp.where(d > 0, d, 1.0))
    oc = _dot(vcst_ref[...], pc.astype(BF16))
    imp = pc[:, 0:Q_BLOCK]
    for h in range(1, HPG):
        imp = imp + pc[:, h * Q_BLOCK:(h + 1) * Q_BLOCK]

    imp_ref[0:SUBLANES, :] = jnp.zeros((SUBLANES, Q_BLOCK), F32)
    imp_ref[SUBLANES:SUBLANES + n_chunk, :] = imp
    imp_ref[SUBLANES + n_chunk:, :] = jnp.zeros((SUBLANES, Q_BLOCK), F32)
    p_slc = None
    for mm in range(SEL_RATIO):
        for nn in range(CMP_RATIO):
            r = imp_ref[pl.ds(SUBLANES + mm - nn, n_sel, stride=SEL_RATIO), :]
            p_slc = r if p_slc is None else p_slc + r
    jb = lax.broadcasted_iota(jnp.int32, (n_sel, 1), 0)
    blk_valid = jb * SEL_LEN <= tpos
    cur = lax.shift_right_logical(tpos, int(np.log2(SEL_LEN)))
    forced = jnp.logical_or(jb == 0, jnp.logical_and(jb <= cur, jb > cur - N_LOCAL_SEL))
    score = jnp.where(blk_valid, jnp.where(forced, jnp.inf, p_slc), -jnp.inf)
    sel = _select_blocks(score, score_ref)
    selb_ref[...] = jnp.concatenate([jnp.where(sel, 0.0, NEG)] * HPG, axis=1)

    row = lax.broadcasted_iota(jnp.int32, (Q_BLOCK, 1), 0)
    causal_bias = jnp.where(row <= tl4, 0.0, NEG)
    init = (jnp.full((1, qw), NEG, F32), jnp.zeros((1, qw), F32), jnp.zeros((HEAD_DIM, qw), F32))
    blocks_per_tile = Q_BLOCK // SEL_LEN

    n_back = WINDOW // Q_BLOCK
    win_tiles = []
    for w in range(n_back + 1):
        kb = i - n_back + w
        kbc = jnp.maximum(kb, 0)
        rows = pl.ds(pl.multiple_of(kbc * Q_BLOCK, Q_BLOCK), Q_BLOCK)
        s = _dot(kw_ref[rows, :].astype(BF16), qt)
        if w == n_back:
            s = s + causal_bias
        else:
            before_start = jnp.where(kb >= 0, 0.0, NEG).astype(F32)
            s = s + (jnp.where(row > tl4, before_start, NEG) if w == 0 else before_start)
        win_tiles.append((s, vwt_ref[kbc]))
    _, l_w, a_w = _softmax_tiles(win_tiles, init)

    n_pair = lax.shift_right_logical(i, 1)
    odd_bias = jnp.where(jnp.bitwise_and(i, 1) == 1, 0.0, NEG).astype(F32)

    def sel_scores(kt, tile_bias):
        rows = pl.ds(pl.multiple_of(kt * Q_BLOCK, Q_BLOCK), Q_BLOCK)
        s = _dot(ks_ref[rows, :].astype(BF16), qt)
        parts = []
        for bb in range(blocks_per_tile):
            bias = selb_ref[pl.ds(kt * blocks_per_tile + bb, 1), :] + tile_bias
            parts.append(s[bb * SEL_LEN:(bb + 1) * SEL_LEN] + bias)
        return jnp.concatenate(parts, axis=0)

    def update_scores(u):
        is_last = u == n_pair
        s_a = sel_scores(jnp.where(is_last, jnp.maximum(i - 1, 0), 2 * u), jnp.where(is_last, odd_bias, 0.0))
        s_b = sel_scores(jnp.where(is_last, i, 2 * u + 1), 0.0) + jnp.where(is_last, causal_bias, 0.0)
        return s_a, s_b

    def sel_body(u, carry):
        m, l, acc, s_a, s_b = carry
        nxt = update_scores(u + 1)
        m, l, acc = _softmax_tiles([(s_a, vst_ref[2 * u]), (s_b, vst_ref[2 * u + 1])], (m, l, acc))
        return (m, l, acc) + nxt

    m_s, l_s, a_s, s_a, s_b = lax.fori_loop(0, n_pair, sel_body, init + update_scores(0))
    _, l_s, a_s = _softmax_tiles([(s_a, vst_ref[jnp.maximum(i - 1, 0)]), (s_b, vst_ref[i])], (m_s, l_s, a_s))

    gt_ref[...] = jax.nn.sigmoid(gl_ref[...]).T
    gate = lambda c: jnp.concatenate(
        [gt_ref[pl.ds((g * HPG + h) * 3 + c, 1), :] for h in range(HPG)], axis=1)
    ot = gate(0) * oc + gate(1) * (a_s * (1.0 / l_s)) + gate(2) * (a_w * (1.0 / l_w))
    for h in range(HPG):
        o_ref[:, h * HEAD_DIM:(h + 1) * HEAD_DIM] = ot[:, h * Q_BLOCK:(h + 1) * Q_BLOCK].T


def _nsa_prompt(z, kcs, vcst, vst, vwt, bsz, t):
    n_chunk = t // CMP_STRIDE
    n_sel = t // SEL_LEN
    n_tile = t // Q_BLOCK
    nqb = t // Q_BLOCK
    qw = HPG * HEAD_DIM
    return pl.pallas_call(
        functools.partial(_nsa_prompt_kernel, n_chunk=n_chunk, n_sel=n_sel),
        grid=(bsz, N_KV, nqb),
        in_specs=[
            pl.BlockSpec((Q_BLOCK, qw), lambda b, g, i: (b * nqb + i, OFF_NQ // qw + g)),
            pl.BlockSpec((Q_BLOCK, LANES), lambda b, g, i: (b * nqb + i, OFF_GL // LANES)),
            pl.BlockSpec((None, None, n_chunk, HEAD_DIM), lambda b, g, i: (b, g, 0, 0)),
            pl.BlockSpec((None, None, HEAD_DIM, n_chunk), lambda b, g, i: (b, g, 0, 0)),
            pl.BlockSpec((t, HEAD_DIM), lambda b, g, i: (b, OFF_KS // HEAD_DIM + g)),
            pl.BlockSpec((None, None, n_tile, HEAD_DIM, HEAD_DIM), lambda b, g, i: (b, g, 0, 0, 0)),
            pl.BlockSpec((t, HEAD_DIM), lambda b, g, i: (b, OFF_KW // HEAD_DIM + g)),
            pl.BlockSpec((None, None, n_tile, HEAD_DIM, HEAD_DIM), lambda b, g, i: (b, g, 0, 0, 0)),
        ],
        out_specs=pl.BlockSpec((Q_BLOCK, qw), lambda b, g, i: (b * nqb + i, g)),
        out_shape=jax.ShapeDtypeStruct((bsz * t, NSA_W), F32),
        scratch_shapes=[
            pltpu.VMEM((n_chunk + 2 * SUBLANES, Q_BLOCK), F32),
            pltpu.VMEM((n_sel, HPG * Q_BLOCK), F32),
            pltpu.VMEM((LANES, Q_BLOCK), F32),
            pltpu.VMEM((n_sel, Q_BLOCK), F32),
        ],
        compiler_params=_cparams(("parallel", "parallel", "arbitrary")),
        name="nsa_prompt",
    )(z, z, kcs, vcst, z, vst, z, vwt)


SAMPLE_TB = 8


def _state_sample_kernel(lg_ref, q_ref, k_ref, v_ref, g_ref, u_ref, sr_ref, sp_ref, wp_ref, ps_ref,
                         or_ref, sn_ref, op_ref, pn_ref, *, pos):
    tb = SAMPLE_TB
    pad = jnp.zeros((SUBLANES - 1, HEAD_DIM), F32)
    for h in range(R_HEADS):
        gamma = jnp.exp(jnp.full((1, HEAD_DIM), lg_ref[h], F32))
        sl = slice(h * HEAD_DIM, (h + 1) * HEAD_DIM)
        rows = []
        for b in range(tb):
            q = q_ref[b:b + 1, sl]
            k = k_ref[b:b + 1, sl]
            v = v_ref[b:b + 1, sl]
            state = sr_ref[b, h]
            k8 = jnp.concatenate([k, pad], axis=0).astype(BF16)
            v8 = jnp.concatenate([v, pad], axis=0).astype(BF16)
            qk = jnp.sum(q * k, axis=1, keepdims=True)
            qs = _dot(jnp.concatenate([q * gamma, pad], axis=0).astype(BF16), state.astype(BF16))[0:1]
            rows.append(qk * v + qs)
            sn_ref[b, h] = gamma[:, 0:1] * state + _dot_tn(k8, v8)
        o = jnp.concatenate(rows, axis=0)
        o = o * lax.rsqrt(jnp.mean(o * o, axis=-1, keepdims=True) + EPS)
        or_ref[:, sl] = o * _silu(g_ref[:, sl])
    u = u_ref[...]
    for gi, w in enumerate(POOL_WINDOWS):
        sl = slice(gi * POOL_GW, (gi + 1) * POOL_GW)
        s = u[:, sl] + jnp.sum(sp_ref[:, POOL_BUF - (w - 1):POOL_BUF, sl], axis=1)
        p = s / float(min(pos + 1, w)) - u[:, sl]
        op_ref[:, sl] = _dot(p.astype(BF16), wp_ref[gi].astype(BF16)) * ps_ref[:, sl]
    pn_ref[:, 0:POOL_BUF - 1, :] = sp_ref[:, 1:POOL_BUF, :]
    for b in range(tb):
        pn_ref[b, POOL_BUF - 1:POOL_BUF, :] = u[b:b + 1, :]


def _state_sample(zs, log_g, state_ret, state_pool, w_pool, pool_scale, layer, dbs, pos):
    tb = SAMPLE_TB
    col = lambda off: pl.BlockSpec((tb, RET_W), lambda i, lg: (i, off // RET_W))
    return pl.pallas_call(
        functools.partial(_state_sample_kernel, pos=pos),
        grid_spec=pltpu.PrefetchScalarGridSpec(
            num_scalar_prefetch=1,
            grid=(dbs // tb,),
            in_specs=[
                col(OFF_RQ), col(OFF_RK), col(OFF_RV), col(OFF_RG), col(OFF_PU),
                pl.BlockSpec((None, tb, R_HEADS, HEAD_DIM, HEAD_DIM), lambda i, lg: (layer, i, 0, 0, 0)),
                pl.BlockSpec((None, tb, POOL_BUF, POOL_W), lambda i, lg: (layer, i, 0, 0)),
                pl.BlockSpec((None, len(POOL_WINDOWS), POOL_GW, POOL_GW), lambda i, lg: (layer, 0, 0, 0)),
                pl.BlockSpec((None, 1, POOL_W), lambda i, lg: (layer, 0, 0)),
            ],
            out_specs=[
                pl.BlockSpec((tb, RET_W), lambda i, lg: (i, 0)),
                pl.BlockSpec((tb, R_HEADS, HEAD_DIM, HEAD_DIM), lambda i, lg: (i, 0, 0, 0)),
                pl.BlockSpec((tb, POOL_W), lambda i, lg: (i, 0)),
                pl.BlockSpec((tb, POOL_BUF, POOL_W), lambda i, lg: (i, 0, 0)),
            ],
        ),
        out_shape=[
            jax.ShapeDtypeStruct((dbs, RET_W), F32),
            jax.ShapeDtypeStruct((dbs, R_HEADS, HEAD_DIM, HEAD_DIM), F32),
            jax.ShapeDtypeStruct((dbs, POOL_W), F32),
            jax.ShapeDtypeStruct((dbs, POOL_BUF, POOL_W), F32),
        ],
        compiler_params=_cparams(("parallel",)),
        name="state_sample",
    )(log_g, zs, zs, zs, zs, zs, state_ret, state_pool, w_pool, pool_scale)


def _softmax_lanes_with_new(s, s_new):
    m = jnp.maximum(jnp.max(s, axis=1, keepdims=True), s_new)
    p = jnp.exp(s - m)
    p_new = jnp.exp(s_new - m)
    inv = 1.0 / (jnp.sum(p, axis=1, keepdims=True) + p_new)
    return p * inv, p_new * inv


def _rank_select(srow):
    across = jnp.broadcast_to(srow, (LANES, LANES))
    down = across.T
    r_i = lax.broadcasted_iota(jnp.int32, (LANES, LANES), 0)
    c_i = lax.broadcasted_iota(jnp.int32, (LANES, LANES), 1)
    beats = jnp.logical_or(down > across, jnp.logical_and(down == across, r_i < c_i))
    rank = jnp.sum(jnp.where(beats, 1.0, 0.0), axis=0, keepdims=True)
    return jnp.logical_and(rank < SEL_TOPK, srow > -jnp.inf)


def _nsa_sample_kernel(pt_ref, q0_ref, q1_ref, ksn_ref, vsn_ref, kwn_ref, vwn_ref, gl_ref, wsel_ref,
                       w1k_ref, w2k_ref, pek_ref, w1v_ref, w2v_ref, pev_ref, wk_ref, wv_ref, *rest,
                       n_pages, past, w_buf, layer, n_older):
    del pt_ref
    pages = rest[:4 * n_pages]
    ck, cv, sk, sv = (pages[c * n_pages:(c + 1) * n_pages] for c in range(4))
    older = rest[4 * n_pages:4 * n_pages + 4 * n_older]
    outs = rest[4 * n_pages + 4 * n_older:]
    o_ref = outs[0]
    scale = HEAD_DIM ** -0.5
    chunks_per_page = PAGE_SIZE // CMP_STRIDE
    n_chunk = n_pages * chunks_per_page
    n_sel = -(-(past + 1) // SEL_LEN)
    assert n_chunk <= LANES and n_sel <= LANES and past % SEL_LEN == 0

    group_rows = lambda ref, g, n: ref[pl.ds(g, n, stride=N_KV), :]

    def chunks_of(prefs):
        return jnp.concatenate(
            [group_rows(p, g, PAGE_SIZE).reshape(chunks_per_page, CHUNK_W) for g in range(N_KV) for p in prefs],
            axis=0)

    ksum = _compress_core(chunks_of(ck), w1k_ref, w2k_ref, pek_ref)
    vsum = _compress_core(chunks_of(cv), w1v_ref, w2v_ref, pev_ref)

    lane = lax.broadcasted_iota(jnp.int32, (1, LANES), 1)
    q_refs = (q0_ref, q1_ref)
    zpad = jnp.zeros((SUBLANES - HPG, HEAD_DIM), F32)
    q8, oc, imps = [], [], []
    for g in range(N_KV):
        qg = jnp.concatenate([q_refs[g][:, h * HEAD_DIM:(h + 1) * HEAD_DIM] for h in range(HPG)] + [zpad],
                             axis=0) * scale
        q8.append(qg.astype(BF16))
        kcs = ksum[g * n_chunk:(g + 1) * n_chunk]
        vcs = vsum[g * n_chunk:(g + 1) * n_chunk]
        s_c = _dot_nt(q8[g], kcs.astype(BF16))
        c_end = lane[:, :n_chunk] * CMP_STRIDE + (CMP_LEN - 1)
        mask_c = jnp.logical_and(c_end <= past, c_end < n_chunk * CMP_STRIDE)
        m = jnp.max(jnp.where(mask_c, s_c, NEG), axis=1, keepdims=True)
        e = jnp.where(mask_c, jnp.exp(s_c - m), 0.0)
        d = jnp.sum(e, axis=1, keepdims=True)
        pc = e * (1.0 / jnp.where(d > 0, d, 1.0))
        oc.append(_dot(pc.astype(BF16), vcs.astype(BF16)))
        imps.append(jnp.sum(pc[0:HPG], axis=0, keepdims=True))

    imp = jnp.concatenate(imps + [jnp.zeros((SUBLANES - N_KV, n_chunk), F32)], axis=0)
    if n_chunk < LANES:
        imp = jnp.concatenate([imp, jnp.zeros((SUBLANES, LANES - n_chunk), F32)], axis=1)
    hi = imp.astype(BF16)
    r1 = imp - hi.astype(F32)
    mid = r1.astype(BF16)
    lo = (r1 - mid.astype(F32)).astype(BF16)
    wsel = wsel_ref[...]
    p_slc = _dot(hi, wsel) + _dot(mid, wsel) + _dot(lo, wsel)
    cur = past // SEL_LEN
    blk_valid = jnp.logical_and(lane * SEL_LEN <= past, lane < n_sel)
    forced = jnp.logical_or(lane == 0, jnp.logical_and(lane <= cur, lane > cur - N_LOCAL_SEL))
    score = jnp.where(blk_valid, jnp.where(forced, jnp.inf, p_slc), -jnp.inf)

    gate = jax.nn.sigmoid(gl_ref[...])
    blocks_per_page = PAGE_SIZE // SEL_LEN
    wl = lax.broadcasted_iota(jnp.int32, (1, w_buf), 1)
    dpos = w_buf - wl
    bias_w = jnp.where(jnp.logical_and(jnp.logical_and(dpos >= 0, dpos < WINDOW), past - dpos >= 0), 0.0, NEG)
    for g in range(N_KV):
        kv_new = slice(g * HEAD_DIM, (g + 1) * HEAD_DIM)
        q_f = q8[g].astype(F32)
        selb = jnp.where(_rank_select(score[g:g + 1]), 0.0, NEG)
        bias_pages = []
        for p in range(n_pages):
            bias = selb[:, p * blocks_per_page:p * blocks_per_page + 1]
            for bb in range(1, blocks_per_page):
                bias = jnp.where(lane < bb * SEL_LEN, bias, selb[:, p * blocks_per_page + bb:p * blocks_per_page + bb + 1])
            bias_pages.append(bias)
        k_all = jnp.concatenate([group_rows(p, g, PAGE_SIZE) for p in sk], axis=0).astype(BF16)
        v_all = jnp.concatenate([group_rows(p, g, PAGE_SIZE) for p in sv], axis=0).astype(BF16)
        s_all = _dot_nt(q8[g], k_all) + jnp.concatenate(bias_pages, axis=1)
        s_new = jnp.sum(q_f * ksn_ref[:, kv_new].astype(BF16).astype(F32), axis=1, keepdims=True)
        p_all, p_new = _softmax_lanes_with_new(s_all, s_new)
        o_s = p_new * vsn_ref[:, kv_new] + _dot(p_all.astype(BF16), v_all)
        s_w = _dot_nt(q8[g], group_rows(wk_ref, g, w_buf).astype(BF16)) + bias_w
        s_wn = jnp.sum(q_f * kwn_ref[:, kv_new].astype(BF16).astype(F32), axis=1, keepdims=True)
        p_w, p_wn = _softmax_lanes_with_new(s_w, s_wn)
        o_w = p_wn * vwn_ref[:, kv_new] + _dot(p_w.astype(BF16), group_rows(wv_ref, g, w_buf).astype(BF16))
        for h in range(HPG):
            c0 = (g * HPG + h) * 3
            og = (gate[:, c0:c0 + 1] * oc[g][h:h + 1] + gate[:, c0 + 1:c0 + 2] * o_s[h:h + 1]
                  + gate[:, c0 + 2:c0 + 3] * o_w[h:h + 1])
            o_ref[:, (g * HPG + h) * HEAD_DIM:(g * HPG + h + 1) * HEAD_DIM] = og

    if len(outs) > 1:
        wko_ref, wvo_ref = outs[1:]
        shifts = [(layer, kwn_ref, vwn_ref, wk_ref, wv_ref)]
        shifts += [(lo,) + tuple(older[4 * lo:4 * lo + 4]) for lo in range(n_older)]
        keep = N_KV * (w_buf - 1)
        for lo, kn, vn, wk, wv in shifts:
            wko_ref[lo, 0:keep, :] = wk[N_KV:N_KV * w_buf, :]
            wvo_ref[lo, 0:keep, :] = wv[N_KV:N_KV * w_buf, :]
            for g in range(N_KV):
                wko_ref[lo, keep + g:keep + g + 1, :] = kn[:, g * HEAD_DIM:(g + 1) * HEAD_DIM]
                wvo_ref[lo, keep + g:keep + g + 1, :] = vn[:, g * HEAD_DIM:(g + 1) * HEAD_DIM]


def _nsa_sample(zs3, page_table, cmp_pages, slc_pages, win_k, win_v, cw, wsel, layer, dbs, past, w_buf, older_zs3):
    n_pages = past // PAGE_SIZE
    depth = win_k.shape[0]
    n_older = 0 if older_zs3 is None else len(older_zs3)
    assert older_zs3 is None or n_older == layer == depth - 1
    rowblk = lambda width, off: pl.BlockSpec((None, 1, width), lambda b, pt: (b, 0, off // width))
    const3 = lambda shp: pl.BlockSpec((None,) + shp, lambda b, pt: (layer,) + (0,) * len(shp))
    page = lambda shp, p: pl.BlockSpec((None, None) + shp, lambda b, pt: (layer, pt[b * n_pages + p], 0, 0))
    cmp_shape = slc_shape = (PAGE_SIZE * N_KV, HEAD_DIM)
    win_shape = (w_buf * N_KV, HEAD_DIM)
    win = pl.BlockSpec((None, None) + win_shape, lambda b, pt: (layer, b, 0, 0))
    in_specs = [
        rowblk(HPG * HEAD_DIM, OFF_NQ), rowblk(HPG * HEAD_DIM, OFF_NQ + HPG * HEAD_DIM),
        rowblk(KV_W, OFF_KS), rowblk(KV_W, OFF_VS), rowblk(KV_W, OFF_KW), rowblk(KV_W, OFF_VW),
        rowblk(LANES, OFF_GL),
        pl.BlockSpec((LANES, LANES), lambda b, pt: (0, 0)),
        const3((CHUNK_W, 2 * HEAD_DIM)), const3((HEAD_DIM, HEAD_DIM)), const3((SUBLANES, CHUNK_W)),
        const3((CHUNK_W, 2 * HEAD_DIM)), const3((HEAD_DIM, HEAD_DIM)), const3((SUBLANES, CHUNK_W)),
        win, win,
    ] + [page(cmp_shape, p) for _ in range(2) for p in range(n_pages)] \
      + [page(slc_shape, p) for _ in range(2) for p in range(n_pages)]
    operands = [zs3] * 7 + [wsel, cw["w1k"], cw["w2k"], cw["pek"], cw["w1v"], cw["w2v"], cw["pev"], win_k, win_v]
    for c in list(cmp_pages) + list(slc_pages):
        operands += [c] * n_pages
    for lo in range(n_older):
        win_lo = pl.BlockSpec((None, None) + win_shape, lambda b, pt, lo=lo: (lo, b, 0, 0))
        operands += [older_zs3[lo], older_zs3[lo], win_k, win_v]
        in_specs += [rowblk(KV_W, OFF_KW), rowblk(KV_W, OFF_VW), win_lo, win_lo]
    out_specs = [pl.BlockSpec((None, 1, NSA_W), lambda b, pt: (b, 0, 0))]
    out_shape = [jax.ShapeDtypeStruct((dbs, 1, NSA_W), F32)]
    if older_zs3 is not None:
        out_specs += [pl.BlockSpec((depth, None) + win_shape, lambda b, pt: (0, b, 0, 0))] * 2
        out_shape += [jax.ShapeDtypeStruct((depth, dbs) + win_shape, F32)] * 2
    return pl.pallas_call(
        functools.partial(_nsa_sample_kernel, n_pages=n_pages, past=past, w_buf=w_buf, layer=layer,
                          n_older=n_older),
        grid_spec=pltpu.PrefetchScalarGridSpec(
            num_scalar_prefetch=1,
            grid=(dbs,),
            in_specs=in_specs,
            out_specs=out_specs,
        ),
        out_shape=out_shape,
        compiler_params=_cparams(("parallel",)),
        name="nsa_sample",
    )(page_table.reshape(-1), *operands)


def _rope_tables(pos):
    half = HEAD_DIM // 2
    inv = ROPE_THETA ** (-jnp.arange(half, dtype=F32) / half)
    ang = pos.astype(F32)[:, None] * inv[None, :]
    cos, sin = jnp.cos(ang), jnp.sin(ang)
    return jnp.concatenate([cos, cos], axis=-1), jnp.concatenate([-sin, sin], axis=-1)


def _block_sum_matrix():
    w = np.zeros((LANES, LANES), np.float32)
    for j in range(LANES):
        for mm in range(SEL_RATIO):
            for nn in range(CMP_RATIO):
                c = SEL_RATIO * j + mm - nn
                if 0 <= c < LANES:
                    w[c, j] += 1.0
    return w


def _row_tile(m, cap):
    tm = min(m, cap)
    while m % tm:
        tm //= 2
    return tm


def kernel(x_prompt, x_sample, state_ret, state_pool, cache_cmp_k, cache_cmp_v, cache_slc_k, cache_slc_v,
           cache_win_k, cache_win_v, page_table, norm1, w_in, w_pool, pool_scale, cmp_w1_k, cmp_w2_k, cmp_pe_k,
           cmp_w1_v, cmp_w2_v, cmp_pe_v, w_out, norm2, w_up, w_down, norm_f):
    bsz, t, _ = x_prompt.shape
    dbs, dec_seq, _ = x_sample.shape
    depth = w_in.shape[0]
    assert dec_seq == 1 and t % 512 == 0 and dbs % SAMPLE_TB == 0
    n_pages = page_table.shape[1]
    past = n_pages * PAGE_SIZE
    n_pool = cache_cmp_k.shape[1]
    w_buf = cache_win_k.shape[2]
    wlen = min(WINDOW, t)

    w_in_bf = jnp.pad(w_in, ((0, 0), (0, 0), (0, N_PAD - D_IN))).astype(BF16)
    w_out_bf, w_up_bf, w_down_bf = w_out.astype(BF16), w_up.astype(BF16), w_down.astype(BF16)
    w1cat = lambda w1: jnp.concatenate([w1[:, :CMP_STRIDE], w1[:, CMP_STRIDE:]], axis=-1).astype(BF16).reshape(
        depth, CHUNK_W, 2 * HEAD_DIM)
    pe_rows = lambda pe: jnp.pad(pe.reshape(depth, CMP_RATIO, CHUNK_W), ((0, 0), (0, SUBLANES - CMP_RATIO), (0, 0)))
    cw = dict(w1k=w1cat(cmp_w1_k), w2k=cmp_w2_k.astype(BF16), pek=pe_rows(cmp_pe_k),
              w1v=w1cat(cmp_w1_v), w2v=cmp_w2_v.astype(BF16), pev=pe_rows(cmp_pe_v))
    norm1_3, norm2_3 = norm1[:, None, :], norm2[:, None, :]
    pool_scale_3 = pool_scale[:, None, :]
    norm_f2 = norm_f[None, :]
    log_g = jnp.log1p(-jnp.exp2(-5.0 - jnp.arange(R_HEADS, dtype=F32)))
    cos_p, sin_p = _rope_tables(jnp.arange(t))
    cos_s, sin_s = _rope_tables(jnp.full((dbs,), past, jnp.int32))
    wsel = jnp.asarray(_block_sum_matrix(), BF16)
    page_view = lambda c: c.reshape(depth, n_pool, PAGE_SIZE * N_KV, HEAD_DIM)
    cmp_pages = [page_view(c) for c in (cache_cmp_k, cache_cmp_v)]
    slc_pages = [page_view(c) for c in (cache_slc_k, cache_slc_v)]
    win_k = cache_win_k.reshape(depth, dbs, w_buf * N_KV, HEAD_DIM)
    win_v = cache_win_v.reshape(depth, dbs, w_buf * N_KV, HEAD_DIM)

    mp = bsz * t
    tm_p = _row_tile(t, 1024)
    tm_mlp = _row_tile(t, 1024)
    xp = x_prompt.reshape(mp, D_MODEL)
    xs = x_sample.reshape(dbs, D_MODEL)
    outs = {k: [] for k in ("ret_p", "ret_s", "pool_p", "pool_s", "ck_p", "ck_s", "cv_p", "cv_s", "sk_p", "sk_s",
                            "sv_p", "sv_s", "wk_p", "wv_p")}
    zs3_all = []
    for l in range(depth):
        last = l == depth - 1
        z = _inproj(xp, norm1_3, w_in_bf, l, cos_p, sin_p, tm_p)
        o_r, s_new = _ret_prompt(z, log_g, bsz, t)
        o_p = _pool_prompt(z, w_pool, pool_scale_3, l, bsz, t)
        kcs, vcst, vst, vwt = _prep_prompt(z, cw, l, bsz, t)
        o_n = _nsa_prompt(z, kcs, vcst, vst, vwt, bsz, t)
        xp = _outproj(o_r, o_p, o_n, w_out_bf, l, xp, tm_p)
        xp = _mlp(xp, norm2_3, w_up_bf, w_down_bf, l, norm_f2, last, tm_mlp)
        z3 = z.reshape(bsz, t, N_PAD)
        kv = lambda off: z3[:, :, off:off + KV_W].reshape(bsz, t, N_KV, HEAD_DIM)
        outs["ret_p"].append(s_new)
        outs["pool_p"].append(z3[:, t - POOL_BUF:, OFF_PU:OFF_PU + POOL_W])
        outs["ck_p"].append(kv(OFF_KC))
        outs["cv_p"].append(kv(OFF_VC))
        outs["sk_p"].append(kv(OFF_KS))
        outs["sv_p"].append(kv(OFF_VS))
        outs["wk_p"].append(kv(OFF_KW)[:, t - wlen:])
        outs["wv_p"].append(kv(OFF_VW)[:, t - wlen:])
        zs = _inproj(xs, norm1_3, w_in_bf, l, cos_s, sin_s, dbs)
        o_r, s_new, o_p, buf_new = _state_sample(zs, log_g, state_ret, state_pool, w_pool, pool_scale_3, l, dbs, past)
        zs3 = zs.reshape(dbs, 1, N_PAD)
        res = _nsa_sample(zs3, page_table, cmp_pages, slc_pages, win_k, win_v, cw, wsel, l, dbs, past, w_buf,
                          zs3_all if last else None)
        o_n = res[0]
        zs3_all.append(zs3)
        if last:
            wk_s, wv_s = res[1:]
        xs = _outproj(o_r, o_p, o_n.reshape(dbs, NSA_W), w_out_bf, l, xs, dbs)
        xs = _mlp(xs, norm2_3, w_up_bf, w_down_bf, l, norm_f2, last, dbs)
        kvs = lambda off: zs[:, off:off + KV_W].reshape(dbs, 1, N_KV, HEAD_DIM)
        outs["ret_s"].append(s_new)
        outs["pool_s"].append(buf_new)
        outs["ck_s"].append(kvs(OFF_KC))
        outs["cv_s"].append(kvs(OFF_VC))
        outs["sk_s"].append(kvs(OFF_KS))
        outs["sv_s"].append(kvs(OFF_VS))
    st = lambda k: jnp.stack(outs[k])
    win_out = lambda w: w.reshape(depth, dbs, w_buf, N_KV, HEAD_DIM)
    return (xp.reshape(bsz, t, D_MODEL), xs.reshape(dbs, 1, D_MODEL),
            st("ret_p"), st("ret_s"), st("pool_p"), st("pool_s"),
            st("ck_p"), st("ck_s"), st("cv_p"), st("cv_s"), st("sk_p"), st("sk_s"), st("sv_p"), st("sv_s"),
            st("wk_p"), win_out(wk_s), st("wv_p"), win_out(wv_s))
```

```python
import functools

import numpy as np
import jax
import jax.numpy as jnp
from jax import lax
from jax.experimental import pallas as pl
from jax.experimental.pallas import tpu as pltpu

F32 = jnp.float32
BF16 = jnp.bfloat16

D_MODEL = 2048
HEAD_DIM = 128
PAGE_SIZE = 128
RET_W = D_MODEL // 4
POOL_W = D_MODEL // 4
NSA_W = D_MODEL - RET_W - POOL_W
R_HEADS = RET_W // HEAD_DIM
N_HEADS = NSA_W // HEAD_DIM
N_KV = max(1, N_HEADS // 4)
HPG = N_HEADS // N_KV
KV_W = N_KV * HEAD_DIM
POOL_WINDOWS = (2, 4, 8, 16)
POOL_GW = POOL_W // len(POOL_WINDOWS)
POOL_BUF = max(POOL_WINDOWS) - 1
RET_CHUNK = 128
CMP_LEN = 32
CMP_STRIDE = 16
CMP_RATIO = CMP_LEN // CMP_STRIDE
SEL_LEN = 64
SEL_RATIO = SEL_LEN // CMP_STRIDE
SEL_TOPK = 16
N_LOCAL_SEL = 2
WINDOW = 512
Q_BLOCK = 128
D_FF = 4 * D_MODEL
ROPE_THETA = 10000.0
EPS = 1e-6
D_IN = 4 * RET_W + POOL_W + NSA_W + 6 * KV_W + 3 * N_HEADS

OFF_RQ = 0
OFF_RK = OFF_RQ + RET_W
OFF_RV = OFF_RK + RET_W
OFF_RG = OFF_RV + RET_W
OFF_PU = OFF_RG + RET_W
OFF_NQ = OFF_PU + POOL_W
OFF_KC = OFF_NQ + NSA_W
OFF_VC = OFF_KC + KV_W
OFF_KS = OFF_VC + KV_W
OFF_VS = OFF_KS + KV_W
OFF_KW = OFF_VS + KV_W
OFF_VW = OFF_KW + KV_W
OFF_GL = OFF_VW + KV_W
PROJ_TN = 512
PROJ_ROW_CHUNK = 256
N_PAD = -(-D_IN // PROJ_TN) * PROJ_TN
ROPE_FULL_TILES = (OFF_RQ // PROJ_TN, OFF_RK // PROJ_TN, OFF_NQ // PROJ_TN, OFF_NQ // PROJ_TN + 1)
ROPE_HALF_TILES = (OFF_KC // PROJ_TN, OFF_KS // PROJ_TN, OFF_KW // PROJ_TN)
RK_TILE = OFF_RK // PROJ_TN

NEG = -1e30
VMEM_LIMIT = 48 * 1024 * 1024
LANES = 128
SUBLANES = 8


def _cparams(sem):
    return pltpu.CompilerParams(dimension_semantics=sem, vmem_limit_bytes=VMEM_LIMIT)


def _dot(a, b):
    return jnp.dot(a, b, preferred_element_type=F32)


def _dot_nt(a, b):
    return lax.dot_general(a, b, (((1,), (1,)), ((), ())), preferred_element_type=F32)


def _dot_tn(a, b):
    return lax.dot_general(a, b, (((0,), (0,)), ((), ())), preferred_element_type=F32)


def _rms(x, g):
    return x * lax.rsqrt(jnp.mean(x * x, axis=-1, keepdims=True) + EPS) * g


def _silu(x):
    return x * jax.nn.sigmoid(x)


def _inproj_kernel(x_ref, g_ref, w_ref, cos_ref, sin_ref, z_ref, hn_ref):
    j = pl.program_id(1)

    @pl.when(j == 0)
    def _():
        hn_ref[...] = _rms(x_ref[...], g_ref[...]).astype(BF16)

    is_full = functools.reduce(jnp.logical_or, [j == t for t in ROPE_FULL_TILES])
    is_half = functools.reduce(jnp.logical_or, [j == t for t in ROPE_HALF_TILES])
    scale = jnp.where(j == RK_TILE, HEAD_DIM ** -0.5, 1.0).astype(F32)
    is_any = jnp.logical_or(is_full, is_half)
    tm = x_ref.shape[0]
    rc = min(tm, PROJ_ROW_CHUNK)
    for r0 in range(0, tm, rc):
        rows = slice(r0, r0 + rc)
        z = _dot(hn_ref[rows, :], w_ref[...])
        cos_t, sin_t = cos_ref[rows, :] * scale, sin_ref[rows, :] * scale
        tab_lead = (jnp.where(is_any, cos_t, 1.0), jnp.where(is_any, sin_t, 0.0))
        tab_rest = (jnp.where(is_full, cos_t, 1.0), jnp.where(is_full, sin_t, 0.0))
        for s in range(PROJ_TN // HEAD_DIM):
            sl = slice(s * HEAD_DIM, (s + 1) * HEAD_DIM)
            c, sn = tab_lead if s < KV_W // HEAD_DIM else tab_rest
            zs = z[:, sl]
            z_ref[rows, sl] = zs * c + pltpu.roll(zs, HEAD_DIM // 2, 1) * sn


def _inproj(x, gamma, w_in_bf, layer, cos_t, sin_t, tm):
    m = x.shape[0]
    n_row_tiles = cos_t.shape[0] // tm
    return pl.pallas_call(
        _inproj_kernel,
        grid=(m // tm, N_PAD // PROJ_TN),
        in_specs=[
            pl.BlockSpec((tm, D_MODEL), lambda i, j: (i, 0)),
            pl.BlockSpec((None, 1, D_MODEL), lambda i, j: (layer, 0, 0)),
            pl.BlockSpec((None, D_MODEL, PROJ_TN), lambda i, j: (layer, 0, j)),
            pl.BlockSpec((tm, HEAD_DIM), lambda i, j: (i % n_row_tiles, 0)),
            pl.BlockSpec((tm, HEAD_DIM), lambda i, j: (i % n_row_tiles, 0)),
        ],
        out_specs=pl.BlockSpec((tm, PROJ_TN), lambda i, j: (i, j)),
        out_shape=jax.ShapeDtypeStruct((m, N_PAD), F32),
        scratch_shapes=[pltpu.VMEM((tm, D_MODEL), BF16)],
        compiler_params=_cparams(("parallel", "arbitrary")),
        name="inproj",
    )(x, gamma, w_in_bf, cos_t, sin_t)


def _outproj_kernel(r_ref, p_ref, n_ref, w_ref, x_ref, o_ref):
    acc = _dot(r_ref[...].astype(BF16), w_ref[0:RET_W, :])
    acc += _dot(p_ref[...].astype(BF16), w_ref[RET_W:RET_W + POOL_W, :])
    acc += _dot(n_ref[...].astype(BF16), w_ref[RET_W + POOL_W:, :])
    o_ref[...] = x_ref[...] + acc


def _outproj(o_r, o_p, o_n, w_out_bf, layer, x, tm, tn=1024):
    m = x.shape[0]
    return pl.pallas_call(
        _outproj_kernel,
        grid=(m // tm, D_MODEL // tn),
        in_specs=[
            pl.BlockSpec((tm, RET_W), lambda i, j: (i, 0)),
            pl.BlockSpec((tm, POOL_W), lambda i, j: (i, 0)),
            pl.BlockSpec((tm, NSA_W), lambda i, j: (i, 0)),
            pl.BlockSpec((None, D_MODEL, tn), lambda i, j: (layer, 0, j)),
            pl.BlockSpec((tm, tn), lambda i, j: (i, j)),
        ],
        out_specs=pl.BlockSpec((tm, tn), lambda i, j: (i, j)),
        out_shape=jax.ShapeDtypeStruct((m, D_MODEL), F32),
        compiler_params=_cparams(("parallel", "parallel")),
        name="outproj",
    )(o_r, o_p, o_n, w_out_bf, x)


def _mlp_kernel(x_ref, g_ref, wu_ref, wd_ref, gf_ref, o_ref, hn_ref, *, final_norm):
    f = pl.program_id(1)

    @pl.when(f == 0)
    def _():
        hn_ref[...] = _rms(x_ref[...], g_ref[...]).astype(BF16)
        o_ref[...] = jnp.zeros_like(o_ref)

    u = jnp.maximum(_dot(hn_ref[...], wu_ref[...]), 0.0)
    o_ref[...] += _dot((u * u).astype(BF16), wd_ref[...])

    @pl.when(f == pl.num_programs(1) - 1)
    def _():
        y = x_ref[...] + o_ref[...]
        o_ref[...] = _rms(y, gf_ref[...]) if final_norm else y


MLP_VMEM_LIMIT = 58 * 1024 * 1024


def _mlp(x, gamma, w_up_bf, w_down_bf, layer, gamma_f, final_norm, tm, tf=512):
    m = x.shape[0]
    return pl.pallas_call(
        functools.partial(_mlp_kernel, final_norm=final_norm),
        grid=(m // tm, D_FF // tf),
        in_specs=[
            pl.BlockSpec((tm, D_MODEL), lambda i, f: (i, 0)),
            pl.BlockSpec((None, 1, D_MODEL), lambda i, f: (layer, 0, 0)),
            pl.BlockSpec((None, D_MODEL, tf), lambda i, f: (layer, 0, f)),
            pl.BlockSpec((None, tf, D_MODEL), lambda i, f: (layer, f, 0)),
            pl.BlockSpec((1, D_MODEL), lambda i, f: (0, 0)),
        ],
        out_specs=pl.BlockSpec((tm, D_MODEL), lambda i, f: (i, 0)),
        out_shape=jax.ShapeDtypeStruct((m, D_MODEL), F32),
        scratch_shapes=[pltpu.VMEM((tm, D_MODEL), BF16)],
        compiler_params=pltpu.CompilerParams(dimension_semantics=("parallel", "arbitrary"),
                                             vmem_limit_bytes=MLP_VMEM_LIMIT),
        name="mlp",
    )(x, gamma, w_up_bf, w_down_bf, gamma_f)


def _ret_prompt_kernel(lg_ref, q_ref, k_ref, v_ref, g_ref, o_ref, s_ref, *, n_chunk):
    ti = pl.program_id(1)
    L = RET_CHUNK
    ri = lax.broadcasted_iota(jnp.int32, (L, L), 0)
    ci = lax.broadcasted_iota(jnp.int32, (L, L), 1)
    diff = (ri - ci).astype(F32)
    causal = diff >= 0
    idx = lax.broadcasted_iota(jnp.int32, (L, 1), 0).astype(F32)
    consts = []
    for h in range(R_HEADS):
        lg = lg_ref[h]
        decay = jnp.where(causal, jnp.exp(jnp.where(causal, diff, 0.0) * lg), 0.0)
        q_dec = jnp.exp((idx + 1.0) * lg)
        k_dec = jnp.exp((L - 1.0 - idx) * lg)
        s_dec = jnp.exp(jnp.full((1, 1), float(L), F32) * lg)
        consts.append((decay, q_dec, k_dec, s_dec))

    @pl.when(ti == 0)
    def _():
        s_ref[...] = jnp.zeros_like(s_ref)

    def body(c, states):
        rows = pl.ds(pl.multiple_of(c * L, L), L)
        new_states = []
        for h, (decay, q_dec, k_dec, s_dec) in enumerate(consts):
            sl = slice(h * HEAD_DIM, (h + 1) * HEAD_DIM)
            q, k, v = q_ref[rows, sl], k_ref[rows, sl], v_ref[rows, sl]
            vb = v.astype(BF16)
            att = _dot_nt(q.astype(BF16), k.astype(BF16)) * decay
            o = _dot(att.astype(BF16), vb) + _dot((q * q_dec).astype(BF16), states[h].astype(BF16))
            new_states.append(s_dec * states[h] + _dot_tn((k * k_dec).astype(BF16), vb))
            o = o * lax.rsqrt(jnp.mean(o * o, axis=-1, keepdims=True) + EPS)
            o_ref[rows, sl] = o * _silu(g_ref[rows, sl])
        return tuple(new_states)

    states = lax.fori_loop(0, n_chunk, body, tuple(s_ref[h] for h in range(R_HEADS)))
    for h in range(R_HEADS):
        s_ref[h] = states[h]


def _ret_prompt(z, log_g, bsz, t, tt=1024):
    tt = min(tt, t)
    nt = t // tt
    blk = lambda off: pl.BlockSpec((tt, RET_W), lambda b, i, lg: (b * nt + i, off // RET_W))
    return pl.pallas_call(
        functools.partial(_ret_prompt_kernel, n_chunk=tt // RET_CHUNK),
        grid_spec=pltpu.PrefetchScalarGridSpec(
            num_scalar_prefetch=1,
            grid=(bsz, nt),
            in_specs=[blk(OFF_RQ), blk(OFF_RK), blk(OFF_RV), blk(OFF_RG)],
            out_specs=[
                pl.BlockSpec((tt, RET_W), lambda b, i, lg: (b * nt + i, 0)),
                pl.BlockSpec((None, R_HEADS, HEAD_DIM, HEAD_DIM), lambda b, i, lg: (b, 0, 0, 0)),
            ],
        ),
        out_shape=[
            jax.ShapeDtypeStruct((bsz * t, RET_W), F32),
            jax.ShapeDtypeStruct((bsz, R_HEADS, HEAD_DIM, HEAD_DIM), F32),
        ],
        compiler_params=_cparams(("parallel", "arbitrary")),
        name="ret_prompt",
    )(log_g, z, z, z, z)


POOL_HALO = 16


def _pool_prompt_kernel(u_ref, wp_ref, ps_ref, o_ref, ext_ref, *, tt):
    t = pl.program_id(1)

    @pl.when(t == 0)
    def _():
        ext_ref[0:POOL_HALO, :] = jnp.zeros((POOL_HALO, POOL_W), F32)

    u = u_ref[...]
    ext_ref[POOL_HALO:POOL_HALO + tt, :] = u
    pos = t * tt + lax.broadcasted_iota(jnp.int32, (tt, 1), 0)
    for g, w in enumerate(POOL_WINDOWS):
        sl = slice(g * POOL_GW, (g + 1) * POOL_GW)
        s = u[:, sl]
        for i in range(1, w):
            s = s + ext_ref[pl.ds(POOL_HALO - i, tt), sl]
        cnt = jnp.minimum(pos + 1, w).astype(F32)
        p = s / cnt - u[:, sl]
        o_ref[:, sl] = _dot(p.astype(BF16), wp_ref[g].astype(BF16)) * ps_ref[:, sl]
    ext_ref[0:POOL_HALO, :] = ext_ref[tt:tt + POOL_HALO, :]


def _pool_prompt(z, w_pool, pool_scale, layer, bsz, t, tt=512):
    nt = t // tt
    return pl.pallas_call(
        functools.partial(_pool_prompt_kernel, tt=tt),
        grid=(bsz, nt),
        in_specs=[
            pl.BlockSpec((tt, POOL_W), lambda b, i: (b * nt + i, OFF_PU // POOL_W)),
            pl.BlockSpec((None, len(POOL_WINDOWS), POOL_GW, POOL_GW), lambda b, i: (layer, 0, 0, 0)),
            pl.BlockSpec((None, 1, POOL_W), lambda b, i: (layer, 0, 0)),
        ],
        out_specs=pl.BlockSpec((tt, POOL_W), lambda b, i: (b * nt + i, 0)),
        out_shape=jax.ShapeDtypeStruct((bsz * t, POOL_W), F32),
        scratch_shapes=[pltpu.VMEM((POOL_HALO + tt, POOL_W), F32)],
        compiler_params=_cparams(("parallel", "arbitrary")),
        name="pool_prompt",
    )(z, w_pool, pool_scale)


CHUNK_W = CMP_STRIDE * HEAD_DIM


def _compress_core(chunks, w1_ref, w2_ref, pe_ref):
    n_rows = chunks.shape[0]
    acc = _dot(jnp.concatenate([chunks, pe_ref[...]], axis=0).astype(BF16), w1_ref[...])
    first = acc[:n_rows, :HEAD_DIM]
    second = acc[:n_rows, HEAD_DIM:]
    pe_term = acc[n_rows:n_rows + 1, :HEAD_DIM] + acc[n_rows + 1:n_rows + 2, HEAD_DIM:]
    h = first + pltpu.roll(second, n_rows - 1, 0) + pe_term
    return _dot(_silu(h).astype(BF16), w2_ref[...])


def _prep_prompt_kernel(kc_ref, vc_ref, vs_ref, vw_ref, w1k_ref, w2k_ref, pek_ref, w1v_ref, w2v_ref, pev_ref,
                        kcs_ref, vcst_ref, vst_ref, vwt_ref, *, n_chunk, n_tile):
    chunks_of = lambda r: jnp.concatenate(
        [r[pl.ds(s, n_chunk, stride=CMP_STRIDE), :] for s in range(CMP_STRIDE)], axis=1)
    kcs_ref[...] = _compress_core(chunks_of(kc_ref), w1k_ref, w2k_ref, pek_ref).astype(BF16)
    vcst_ref[...] = _compress_core(chunks_of(vc_ref), w1v_ref, w2v_ref, pev_ref).T.astype(BF16)

    def tbody(i, carry):
        rows = pl.ds(pl.multiple_of(i * HEAD_DIM, HEAD_DIM), HEAD_DIM)
        vst_ref[i] = vs_ref[rows, :].T.astype(BF16)
        vwt_ref[i] = vw_ref[rows, :].T.astype(BF16)
        return carry

    lax.fori_loop(0, n_tile, tbody, 0)


def _prep_prompt(z, cw, layer, bsz, t):
    n_chunk = t // CMP_STRIDE
    n_tile = t // HEAD_DIM
    col = lambda off: pl.BlockSpec((t, HEAD_DIM), lambda b, g: (b, off // HEAD_DIM + g))
    w1 = pl.BlockSpec((None, CHUNK_W, 2 * HEAD_DIM), lambda b, g: (layer, 0, 0))
    w2 = pl.BlockSpec((None, HEAD_DIM, HEAD_DIM), lambda b, g: (layer, 0, 0))
    pe = pl.BlockSpec((None, SUBLANES, CHUNK_W), lambda b, g: (layer, 0, 0))
    return pl.pallas_call(
        functools.partial(_prep_prompt_kernel, n_chunk=n_chunk, n_tile=n_tile),
        grid=(bsz, N_KV),
        in_specs=[col(OFF_KC), col(OFF_VC), col(OFF_VS), col(OFF_VW), w1, w2, pe, w1, w2, pe],
        out_specs=[
            pl.BlockSpec((None, None, n_chunk, HEAD_DIM), lambda b, g: (b, g, 0, 0)),
            pl.BlockSpec((None, None, HEAD_DIM, n_chunk), lambda b, g: (b, g, 0, 0)),
            pl.BlockSpec((None, None, n_tile, HEAD_DIM, HEAD_DIM), lambda b, g: (b, g, 0, 0, 0)),
            pl.BlockSpec((None, None, n_tile, HEAD_DIM, HEAD_DIM), lambda b, g: (b, g, 0, 0, 0)),
        ],
        out_shape=[
            jax.ShapeDtypeStruct((bsz, N_KV, n_chunk, HEAD_DIM), BF16),
            jax.ShapeDtypeStruct((bsz, N_KV, HEAD_DIM, n_chunk), BF16),
            jax.ShapeDtypeStruct((bsz, N_KV, n_tile, HEAD_DIM, HEAD_DIM), BF16),
            jax.ShapeDtypeStruct((bsz, N_KV, n_tile, HEAD_DIM, HEAD_DIM), BF16),
        ],
        compiler_params=_cparams(("parallel", "parallel")),
        name="prep_prompt",
    )(z, z, z, z, cw["w1k"], cw["w2k"], cw["pek"], cw["w1v"], cw["w2v"], cw["pev"])


LOG2E = 1.4426950408889634


def _softmax_tiles(tiles, carry):
    m, l, acc = carry
    m_new = m
    for s, _ in tiles:
        m_new = jnp.maximum(m_new, jnp.max(s, axis=0, keepdims=True))
    alpha = jnp.exp2(m - m_new)
    l = alpha * l
    acc = alpha * acc
    for s, vt in tiles:
        p = jnp.exp2(s - m_new)
        l = l + jnp.sum(p, axis=0, keepdims=True)
        acc = acc + _dot(vt, p.astype(BF16))
    return m_new, l, acc


def _select_blocks(score, score_ref):
    n_blocks = score.shape[0]
    score_ref[...] = score
    groups = [score[v * SUBLANES:(v + 1) * SUBLANES] for v in range(n_blocks // SUBLANES)]
    jb = lax.broadcasted_iota(jnp.int32, groups[0].shape, 0)
    ranks = [jnp.zeros(gv.shape, F32) for gv in groups]
    for k in range(n_blocks):
        rk = score_ref[pl.ds(k, 1), :]
        for v, gv in enumerate(groups):
            if v * SUBLANES > k:
                beats = rk >= gv
            elif (v + 1) * SUBLANES - 1 <= k:
                beats = rk > gv
            else:
                beats = jnp.logical_or(rk > gv, jnp.logical_and(jb + v * SUBLANES > k, rk == gv))
            ranks[v] = ranks[v] + jnp.where(beats, 1.0, 0.0)
    rank = jnp.concatenate(ranks, axis=0)
    return jnp.logical_and(rank < SEL_TOPK, score > -jnp.inf)


def _nsa_prompt_kernel(q_ref, gl_ref, kcs_ref, vcst_ref, ks_ref, vst_ref, kw_ref, vwt_ref, o_ref,
                       imp_ref, selb_ref, gt_ref, score_ref, *, n_chunk, n_sel):
    g = pl.program_id(1)
    i = pl.program_id(2)
    qw = HPG * Q_BLOCK
    qscale = HEAD_DIM ** -0.5 * LOG2E
    q = q_ref[...]
    qt = jnp.concatenate([(q[:, h * HEAD_DIM:(h + 1) * HEAD_DIM] * qscale).T for h in range(HPG)],
                         axis=1).astype(BF16)
    lane = lax.broadcasted_iota(jnp.int32, (1, qw), 1)
    tl4 = jnp.bitwise_and(lane, Q_BLOCK - 1)
    tpos4 = i * Q_BLOCK + tl4
    tpos = i * Q_BLOCK + lax.broadcasted_iota(jnp.int32, (1, Q_BLOCK), 1)

    sc = _dot(kcs_ref[...], qt)
    c_end = lax.broadcasted_iota(jnp.int32, (n_chunk, 1), 0) * CMP_STRIDE + (CMP_LEN - 1)
    mask_c = jnp.logical_and(c_end <= tpos4, c_end < n_chunk * CMP_STRIDE)
    m = jnp.max(jnp.where(mask_c, sc, NEG), axis=0, keepdims=True)
    e = jnp.where(mask_c, jnp.exp2(sc - m), 0.0)
    d = jnp.sum(e, axis=0, keepdims=True)
    pc = e * (1.0 / jnp.where(d > 0, d, 1.0))
    oc = _dot(vcst_ref[...], pc.astype(BF16))
    imp = pc[:, 0:Q_BLOCK]
    for h in range(1, HPG):
        imp = imp + pc[:, h * Q_BLOCK:(h + 1) * Q_BLOCK]

    imp_ref[0:SUBLANES, :] = jnp.zeros((SUBLANES, Q_BLOCK), F32)
    imp_ref[SUBLANES:SUBLANES + n_chunk, :] = imp
    imp_ref[SUBLANES + n_chunk:, :] = jnp.zeros((SUBLANES, Q_BLOCK), F32)
    p_slc = None
    for mm in range(SEL_RATIO):
        for nn in range(CMP_RATIO):
            r = imp_ref[pl.ds(SUBLANES + mm - nn, n_sel, stride=SEL_RATIO), :]
            p_slc = r if p_slc is None else p_slc + r
    jb = lax.broadcasted_iota(jnp.int32, (n_sel, 1), 0)
    blk_valid = jb * SEL_LEN <= tpos
    cur = lax.shift_right_logical(tpos, int(np.log2(SEL_LEN)))
    forced = jnp.logical_or(jb == 0, jnp.logical_and(jb <= cur, jb > cur - N_LOCAL_SEL))
    score = jnp.where(blk_valid, jnp.where(forced, jnp.inf, p_slc), -jnp.inf)
    sel = _select_blocks(score, score_ref)
    selb_ref[...] = jnp.concatenate([jnp.where(sel, 0.0, NEG)] * HPG, axis=1)

    row = lax.broadcasted_iota(jnp.int32, (Q_BLOCK, 1), 0)
    causal_bias = jnp.where(row <= tl4, 0.0, NEG)
    init = (jnp.full((1, qw), NEG, F32), jnp.zeros((1, qw), F32), jnp.zeros((HEAD_DIM, qw), F32))
    blocks_per_tile = Q_BLOCK // SEL_LEN

    n_back = WINDOW // Q_BLOCK
    win_tiles = []
    for w in range(n_back + 1):
        kb = i - n_back + w
        kbc = jnp.maximum(kb, 0)
        rows = pl.ds(pl.multiple_of(kbc * Q_BLOCK, Q_BLOCK), Q_BLOCK)
        s = _dot(kw_ref[rows, :].astype(BF16), qt)
        if w == n_back:
            s = s + causal_bias
        else:
            before_start = jnp.where(kb >= 0, 0.0, NEG).astype(F32)
            s = s + (jnp.where(row > tl4, before_start, NEG) if w == 0 else before_start)
        win_tiles.append((s, vwt_ref[kbc]))
    _, l_w, a_w = _softmax_tiles(win_tiles, init)

    n_pair = lax.shift_right_logical(i, 1)
    odd_bias = jnp.where(jnp.bitwise_and(i, 1) == 1, 0.0, NEG).astype(F32)

    def sel_scores(kt, tile_bias):
        rows = pl.ds(pl.multiple_of(kt * Q_BLOCK, Q_BLOCK), Q_BLOCK)
        s = _dot(ks_ref[rows, :].astype(BF16), qt)
        parts = []
        for bb in range(blocks_per_tile):
            bias = selb_ref[pl.ds(kt * blocks_per_tile + bb, 1), :] + tile_bias
            parts.append(s[bb * SEL_LEN:(bb + 1) * SEL_LEN] + bias)
        return jnp.concatenate(parts, axis=0)

    def update_scores(u):
        is_last = u == n_pair
        s_a = sel_scores(jnp.where(is_last, jnp.maximum(i - 1, 0), 2 * u), jnp.where(is_last, odd_bias, 0.0))
        s_b = sel_scores(jnp.where(is_last, i, 2 * u + 1), 0.0) + jnp.where(is_last, causal_bias, 0.0)
        return s_a, s_b

    def sel_body(u, carry):
        m, l, acc, s_a, s_b = carry
        nxt = update_scores(u + 1)
        m, l, acc = _softmax_tiles([(s_a, vst_ref[2 * u]), (s_b, vst_ref[2 * u + 1])], (m, l, acc))
        return (m, l, acc) + nxt

    m_s, l_s, a_s, s_a, s_b = lax.fori_loop(0, n_pair, sel_body, init + update_scores(0))
    _, l_s, a_s = _softmax_tiles([(s_a, vst_ref[jnp.maximum(i - 1, 0)]), (s_b, vst_ref[i])], (m_s, l_s, a_s))

    gt_ref[...] = jax.nn.sigmoid(gl_ref[...]).T
    gate = lambda c: jnp.concatenate(
        [gt_ref[pl.ds((g * HPG + h) * 3 + c, 1), :] for h in range(HPG)], axis=1)
    ot = gate(0) * oc + gate(1) * (a_s * (1.0 / l_s)) + gate(2) * (a_w * (1.0 / l_w))
    for h in range(HPG):
        o_ref[:, h * HEAD_DIM:(h + 1) * HEAD_DIM] = ot[:, h * Q_BLOCK:(h + 1) * Q_BLOCK].T


def _nsa_prompt(z, kcs, vcst, vst, vwt, bsz, t):
    n_chunk = t // CMP_STRIDE
    n_sel = t // SEL_LEN
    n_tile = t // Q_BLOCK
    nqb = t // Q_BLOCK
    qw = HPG * HEAD_DIM
    return pl.pallas_call(
        functools.partial(_nsa_prompt_kernel, n_chunk=n_chunk, n_sel=n_sel),
        grid=(bsz, N_KV, nqb),
        in_specs=[
            pl.BlockSpec((Q_BLOCK, qw), lambda b, g, i: (b * nqb + i, OFF_NQ // qw + g)),
            pl.BlockSpec((Q_BLOCK, LANES), lambda b, g, i: (b * nqb + i, OFF_GL // LANES)),
            pl.BlockSpec((None, None, n_chunk, HEAD_DIM), lambda b, g, i: (b, g, 0, 0)),
            pl.BlockSpec((None, None, HEAD_DIM, n_chunk), lambda b, g, i: (b, g, 0, 0)),
            pl.BlockSpec((t, HEAD_DIM), lambda b, g, i: (b, OFF_KS // HEAD_DIM + g)),
            pl.BlockSpec((None, None, n_tile, HEAD_DIM, HEAD_DIM), lambda b, g, i: (b, g, 0, 0, 0)),
            pl.BlockSpec((t, HEAD_DIM), lambda b, g, i: (b, OFF_KW // HEAD_DIM + g)),
            pl.BlockSpec((None, None, n_tile, HEAD_DIM, HEAD_DIM), lambda b, g, i: (b, g, 0, 0, 0)),
        ],
        out_specs=pl.BlockSpec((Q_BLOCK, qw), lambda b, g, i: (b * nqb + i, g)),
        out_shape=jax.ShapeDtypeStruct((bsz * t, NSA_W), F32),
        scratch_shapes=[
            pltpu.VMEM((n_chunk + 2 * SUBLANES, Q_BLOCK), F32),
            pltpu.VMEM((n_sel, HPG * Q_BLOCK), F32),
            pltpu.VMEM((LANES, Q_BLOCK), F32),
            pltpu.VMEM((n_sel, Q_BLOCK), F32),
        ],
        compiler_params=_cparams(("parallel", "parallel", "arbitrary")),
        name="nsa_prompt",
    )(z, z, kcs, vcst, z, vst, z, vwt)


SAMPLE_TB = 8


def _state_sample_kernel(lg_ref, q_ref, k_ref, v_ref, g_ref, u_ref, sr_ref, sp_ref, wp_ref, ps_ref,
                         or_ref, sn_ref, op_ref, pn_ref, *, pos):
    tb = SAMPLE_TB
    pad = jnp.zeros((SUBLANES - 1, HEAD_DIM), F32)
    for h in range(R_HEADS):
        gamma = jnp.exp(jnp.full((1, HEAD_DIM), lg_ref[h], F32))
        sl = slice(h * HEAD_DIM, (h + 1) * HEAD_DIM)
        rows = []
        for b in range(tb):
            q = q_ref[b:b + 1, sl]
            k = k_ref[b:b + 1, sl]
            v = v_ref[b:b + 1, sl]
            state = sr_ref[b, h]
            k8 = jnp.concatenate([k, pad], axis=0).astype(BF16)
            v8 = jnp.concatenate([v, pad], axis=0).astype(BF16)
            qk = jnp.sum(q * k, axis=1, keepdims=True)
            qs = _dot(jnp.concatenate([q * gamma, pad], axis=0).astype(BF16), state.astype(BF16))[0:1]
            rows.append(qk * v + qs)
            sn_ref[b, h] = gamma[:, 0:1] * state + _dot_tn(k8, v8)
        o = jnp.concatenate(rows, axis=0)
        o = o * lax.rsqrt(jnp.mean(o * o, axis=-1, keepdims=True) + EPS)
        or_ref[:, sl] = o * _silu(g_ref[:, sl])
    u = u_ref[...]
    for gi, w in enumerate(POOL_WINDOWS):
        sl = slice(gi * POOL_GW, (gi + 1) * POOL_GW)
        s = u[:, sl] + jnp.sum(sp_ref[:, POOL_BUF - (w - 1):POOL_BUF, sl], axis=1)
        p = s / float(min(pos + 1, w)) - u[:, sl]
        op_ref[:, sl] = _dot(p.astype(BF16), wp_ref[gi].astype(BF16)) * ps_ref[:, sl]
    pn_ref[:, 0:POOL_BUF - 1, :] = sp_ref[:, 1:POOL_BUF, :]
    for b in range(tb):
        pn_ref[b, POOL_BUF - 1:POOL_BUF, :] = u[b:b + 1, :]


def _state_sample(zs, log_g, state_ret, state_pool, w_pool, pool_scale, layer, dbs, pos):
    tb = SAMPLE_TB
    col = lambda off: pl.BlockSpec((tb, RET_W), lambda i, lg: (i, off // RET_W))
    return pl.pallas_call(
        functools.partial(_state_sample_kernel, pos=pos),
        grid_spec=pltpu.PrefetchScalarGridSpec(
            num_scalar_prefetch=1,
            grid=(dbs // tb,),
            in_specs=[
                col(OFF_RQ), col(OFF_RK), col(OFF_RV), col(OFF_RG), col(OFF_PU),
                pl.BlockSpec((None, tb, R_HEADS, HEAD_DIM, HEAD_DIM), lambda i, lg: (layer, i, 0, 0, 0)),
                pl.BlockSpec((None, tb, POOL_BUF, POOL_W), lambda i, lg: (layer, i, 0, 0)),
                pl.BlockSpec((None, len(POOL_WINDOWS), POOL_GW, POOL_GW), lambda i, lg: (layer, 0, 0, 0)),
                pl.BlockSpec((None, 1, POOL_W), lambda i, lg: (layer, 0, 0)),
            ],
            out_specs=[
                pl.BlockSpec((tb, RET_W), lambda i, lg: (i, 0)),
                pl.BlockSpec((tb, R_HEADS, HEAD_DIM, HEAD_DIM), lambda i, lg: (i, 0, 0, 0)),
                pl.BlockSpec((tb, POOL_W), lambda i, lg: (i, 0)),
                pl.BlockSpec((tb, POOL_BUF, POOL_W), lambda i, lg: (i, 0, 0)),
            ],
        ),
        out_shape=[
            jax.ShapeDtypeStruct((dbs, RET_W), F32),
            jax.ShapeDtypeStruct((dbs, R_HEADS, HEAD_DIM, HEAD_DIM), F32),
            jax.ShapeDtypeStruct((dbs, POOL_W), F32),
            jax.ShapeDtypeStruct((dbs, POOL_BUF, POOL_W), F32),
        ],
        compiler_params=_cparams(("parallel",)),
        name="state_sample",
    )(log_g, zs, zs, zs, zs, zs, state_ret, state_pool, w_pool, pool_scale)


N_CACHE = 4


def _softmax_lanes_with_new(s, s_new):
    m = jnp.maximum(jnp.max(s, axis=1, keepdims=True), s_new)
    p = jnp.exp(s - m)
    p_new = jnp.exp(s_new - m)
    inv = 1.0 / (jnp.sum(p, axis=1, keepdims=True) + p_new)
    return p * inv, p_new * inv


def _rank_select(srow):
    across = jnp.broadcast_to(srow, (LANES, LANES))
    down = across.T
    r_i = lax.broadcasted_iota(jnp.int32, (LANES, LANES), 0)
    c_i = lax.broadcasted_iota(jnp.int32, (LANES, LANES), 1)
    beats = jnp.logical_or(down > across, jnp.logical_and(down == across, r_i < c_i))
    rank = jnp.sum(jnp.where(beats, 1.0, 0.0), axis=0, keepdims=True)
    return jnp.logical_and(rank < SEL_TOPK, srow > -jnp.inf)


def _nsa_sample_kernel(pt_ref, q0_ref, q1_ref, ksn_ref, vsn_ref, kwn_ref, vwn_ref, gl_ref, wsel_ref,
                       w1k_ref, w2k_ref, pek_ref, w1v_ref, w2v_ref, pev_ref, wk_ref, wv_ref, *rest,
                       n_pages, past, w_buf, layer, n_older):
    caches = rest[:N_CACHE]
    older = rest[N_CACHE:N_CACHE + 4 * n_older]
    pages_buf, sem = rest[-2:]
    outs = rest[N_CACHE + 4 * n_older:-2]
    o_ref = outs[0]

    b = pl.program_id(0)

    def page_copies(seq, slot):
        return [pltpu.make_async_copy(caches[c].at[layer, pt_ref[seq * n_pages + p]],
                                      pages_buf.at[slot, c, p], sem.at[slot])
                for c in range(N_CACHE) for p in range(n_pages)]

    @pl.when(b == 0)
    def _():
        for cp in page_copies(0, 0):
            cp.start()

    @pl.when(b + 1 < pl.num_programs(0))
    def _():
        for cp in page_copies(b + 1, jnp.bitwise_and(b + 1, 1)):
            cp.start()

    slot = jnp.bitwise_and(b, 1)
    for cp in page_copies(b, slot):
        cp.wait()
    ck, cv, sk, sv = ([pages_buf.at[slot, c, p] for p in range(n_pages)] for c in range(N_CACHE))
    scale = HEAD_DIM ** -0.5
    chunks_per_page = PAGE_SIZE // CMP_STRIDE
    n_chunk = n_pages * chunks_per_page
    n_sel = -(-(past + 1) // SEL_LEN)
    assert n_chunk <= LANES and n_sel <= LANES and past % SEL_LEN == 0

    group_rows = lambda ref, g, n: ref[pl.ds(g, n, stride=N_KV), :]

    def chunks_of(prefs):
        return jnp.concatenate(
            [group_rows(p, g, PAGE_SIZE).reshape(chunks_per_page, CHUNK_W) for g in range(N_KV) for p in prefs],
            axis=0)

    ksum = _compress_core(chunks_of(ck), w1k_ref, w2k_ref, pek_ref)
    vsum = _compress_core(chunks_of(cv), w1v_ref, w2v_ref, pev_ref)

    lane = lax.broadcasted_iota(jnp.int32, (1, LANES), 1)
    q_refs = (q0_ref, q1_ref)
    zpad = jnp.zeros((SUBLANES - HPG, HEAD_DIM), F32)
    q8, oc, imps = [], [], []
    for g in range(N_KV):
        qg = jnp.concatenate([q_refs[g][:, h * HEAD_DIM:(h + 1) * HEAD_DIM] for h in range(HPG)] + [zpad],
                             axis=0) * scale
        q8.append(qg.astype(BF16))
        kcs = ksum[g * n_chunk:(g + 1) * n_chunk]
        vcs = vsum[g * n_chunk:(g + 1) * n_chunk]
        s_c = _dot_nt(q8[g], kcs.astype(BF16))
        c_end = lane[:, :n_chunk] * CMP_STRIDE + (CMP_LEN - 1)
        mask_c = jnp.logical_and(c_end <= past, c_end < n_chunk * CMP_STRIDE)
        m = jnp.max(jnp.where(mask_c, s_c, NEG), axis=1, keepdims=True)
        e = jnp.where(mask_c, jnp.exp(s_c - m), 0.0)
        d = jnp.sum(e, axis=1, keepdims=True)
        pc = e * (1.0 / jnp.where(d > 0, d, 1.0))
        oc.append(_dot(pc.astype(BF16), vcs.astype(BF16)))
        imps.append(jnp.sum(pc[0:HPG], axis=0, keepdims=True))

    imp = jnp.concatenate(imps + [jnp.zeros((SUBLANES - N_KV, n_chunk), F32)], axis=0)
    if n_chunk < LANES:
        imp = jnp.concatenate([imp, jnp.zeros((SUBLANES, LANES - n_chunk), F32)], axis=1)
    hi = imp.astype(BF16)
    r1 = imp - hi.astype(F32)
    mid = r1.astype(BF16)
    lo = (r1 - mid.astype(F32)).astype(BF16)
    wsel = wsel_ref[...]
    p_slc = _dot(hi, wsel) + _dot(mid, wsel) + _dot(lo, wsel)
    cur = past // SEL_LEN
    blk_valid = jnp.logical_and(lane * SEL_LEN <= past, lane < n_sel)
    forced = jnp.logical_or(lane == 0, jnp.logical_and(lane <= cur, lane > cur - N_LOCAL_SEL))
    score = jnp.where(blk_valid, jnp.where(forced, jnp.inf, p_slc), -jnp.inf)

    gate = jax.nn.sigmoid(gl_ref[...])
    blocks_per_page = PAGE_SIZE // SEL_LEN
    wl = lax.broadcasted_iota(jnp.int32, (1, w_buf), 1)
    dpos = w_buf - wl
    bias_w = jnp.where(jnp.logical_and(jnp.logical_and(dpos >= 0, dpos < WINDOW), past - dpos >= 0), 0.0, NEG)
    for g in range(N_KV):
        kv_new = slice(g * HEAD_DIM, (g + 1) * HEAD_DIM)
        q_f = q8[g].astype(F32)
        selb = jnp.where(_rank_select(score[g:g + 1]), 0.0, NEG)
        bias_pages = []
        for p in range(n_pages):
            bias = selb[:, p * blocks_per_page:p * blocks_per_page + 1]
            for bb in range(1, blocks_per_page):
                bias = jnp.where(lane < bb * SEL_LEN, bias, selb[:, p * blocks_per_page + bb:p * blocks_per_page + bb + 1])
            bias_pages.append(bias)
        k_all = jnp.concatenate([group_rows(p, g, PAGE_SIZE) for p in sk], axis=0).astype(BF16)
        v_all = jnp.concatenate([group_rows(p, g, PAGE_SIZE) for p in sv], axis=0).astype(BF16)
        s_all = _dot_nt(q8[g], k_all) + jnp.concatenate(bias_pages, axis=1)
        s_new = jnp.sum(q_f * ksn_ref[:, kv_new].astype(BF16).astype(F32), axis=1, keepdims=True)
        p_all, p_new = _softmax_lanes_with_new(s_all, s_new)
        o_s = p_new * vsn_ref[:, kv_new] + _dot(p_all.astype(BF16), v_all)
        s_w = _dot_nt(q8[g], group_rows(wk_ref, g, w_buf).astype(BF16)) + bias_w
        s_wn = jnp.sum(q_f * kwn_ref[:, kv_new].astype(BF16).astype(F32), axis=1, keepdims=True)
        p_w, p_wn = _softmax_lanes_with_new(s_w, s_wn)
        o_w = p_wn * vwn_ref[:, kv_new] + _dot(p_w.astype(BF16), group_rows(wv_ref, g, w_buf).astype(BF16))
        for h in range(HPG):
            c0 = (g * HPG + h) * 3
            og = (gate[:, c0:c0 + 1] * oc[g][h:h + 1] + gate[:, c0 + 1:c0 + 2] * o_s[h:h + 1]
                  + gate[:, c0 + 2:c0 + 3] * o_w[h:h + 1])
            o_ref[:, (g * HPG + h) * HEAD_DIM:(g * HPG + h + 1) * HEAD_DIM] = og

    if len(outs) > 1:
        wko_ref, wvo_ref = outs[1:]
        shifts = [(layer, kwn_ref, vwn_ref, wk_ref, wv_ref)]
        shifts += [(lo,) + tuple(older[4 * lo:4 * lo + 4]) for lo in range(n_older)]
        keep = N_KV * (w_buf - 1)
        for lo, kn, vn, wk, wv in shifts:
            wko_ref[lo, 0:keep, :] = wk[N_KV:N_KV * w_buf, :]
            wvo_ref[lo, 0:keep, :] = wv[N_KV:N_KV * w_buf, :]
            for g in range(N_KV):
                wko_ref[lo, keep + g:keep + g + 1, :] = kn[:, g * HEAD_DIM:(g + 1) * HEAD_DIM]
                wvo_ref[lo, keep + g:keep + g + 1, :] = vn[:, g * HEAD_DIM:(g + 1) * HEAD_DIM]


def _nsa_sample(zs3, page_table, cmp_pages, slc_pages, win_k, win_v, cw, wsel, layer, dbs, past, w_buf, older_zs3):
    n_pages = past // PAGE_SIZE
    depth = win_k.shape[0]
    n_older = 0 if older_zs3 is None else len(older_zs3)
    assert older_zs3 is None or n_older == layer == depth - 1
    rowblk = lambda width, off: pl.BlockSpec((None, 1, width), lambda b, pt: (b, 0, off // width))
    const3 = lambda shp: pl.BlockSpec((None,) + shp, lambda b, pt: (layer,) + (0,) * len(shp))
    page_shape = (PAGE_SIZE * N_KV, HEAD_DIM)
    win_shape = (w_buf * N_KV, HEAD_DIM)
    win = pl.BlockSpec((None, None) + win_shape, lambda b, pt: (layer, b, 0, 0))
    in_specs = [
        rowblk(HPG * HEAD_DIM, OFF_NQ), rowblk(HPG * HEAD_DIM, OFF_NQ + HPG * HEAD_DIM),
        rowblk(KV_W, OFF_KS), rowblk(KV_W, OFF_VS), rowblk(KV_W, OFF_KW), rowblk(KV_W, OFF_VW),
        rowblk(LANES, OFF_GL),
        pl.BlockSpec((LANES, LANES), lambda b, pt: (0, 0)),
        const3((CHUNK_W, 2 * HEAD_DIM)), const3((HEAD_DIM, HEAD_DIM)), const3((SUBLANES, CHUNK_W)),
        const3((CHUNK_W, 2 * HEAD_DIM)), const3((HEAD_DIM, HEAD_DIM)), const3((SUBLANES, CHUNK_W)),
        win, win,
    ] + [pl.BlockSpec(memory_space=pl.ANY)] * N_CACHE
    operands = [zs3] * 7 + [wsel, cw["w1k"], cw["w2k"], cw["pek"], cw["w1v"], cw["w2v"], cw["pev"], win_k, win_v]
    operands += list(cmp_pages) + list(slc_pages)
    for lo in range(n_older):
        win_lo = pl.BlockSpec((None, None) + win_shape, lambda b, pt, lo=lo: (lo, b, 0, 0))
        operands += [older_zs3[lo], older_zs3[lo], win_k, win_v]
        in_specs += [rowblk(KV_W, OFF_KW), rowblk(KV_W, OFF_VW), win_lo, win_lo]
    out_specs = [pl.BlockSpec((None, 1, NSA_W), lambda b, pt: (b, 0, 0))]
    out_shape = [jax.ShapeDtypeStruct((dbs, 1, NSA_W), F32)]
    if older_zs3 is not None:
        out_specs += [pl.BlockSpec((depth, None) + win_shape, lambda b, pt: (0, b, 0, 0))] * 2
        out_shape += [jax.ShapeDtypeStruct((depth, dbs) + win_shape, F32)] * 2
    return pl.pallas_call(
        functools.partial(_nsa_sample_kernel, n_pages=n_pages, past=past, w_buf=w_buf, layer=layer,
                          n_older=n_older),
        grid_spec=pltpu.PrefetchScalarGridSpec(
            num_scalar_prefetch=1,
            grid=(dbs,),
            in_specs=in_specs,
            out_specs=out_specs,
            scratch_shapes=[pltpu.VMEM((2, N_CACHE, n_pages) + page_shape, F32), pltpu.SemaphoreType.DMA((2,))],
        ),
        out_shape=out_shape,
        compiler_params=_cparams(("arbitrary",)),
        name="nsa_sample",
    )(page_table.reshape(-1), *operands)


def _rope_tables(pos):
    half = HEAD_DIM // 2
    inv = ROPE_THETA ** (-jnp.arange(half, dtype=F32) / half)
    ang = pos.astype(F32)[:, None] * inv[None, :]
    cos, sin = jnp.cos(ang), jnp.sin(ang)
    return jnp.concatenate([cos, cos], axis=-1), jnp.concatenate([-sin, sin], axis=-1)


def _block_sum_matrix():
    w = np.zeros((LANES, LANES), np.float32)
    for j in range(LANES):
        for mm in range(SEL_RATIO):
            for nn in range(CMP_RATIO):
                c = SEL_RATIO * j + mm - nn
                if 0 <= c < LANES:
                    w[c, j] += 1.0
    return w


def _row_tile(m, cap):
    tm = min(m, cap)
    while m % tm:
        tm //= 2
    return tm


def kernel(x_prompt, x_sample, state_ret, state_pool, cache_cmp_k, cache_cmp_v, cache_slc_k, cache_slc_v,
           cache_win_k, cache_win_v, page_table, norm1, w_in, w_pool, pool_scale, cmp_w1_k, cmp_w2_k, cmp_pe_k,
           cmp_w1_v, cmp_w2_v, cmp_pe_v, w_out, norm2, w_up, w_down, norm_f):
    bsz, t, _ = x_prompt.shape
    dbs, dec_seq, _ = x_sample.shape
    depth = w_in.shape[0]
    assert dec_seq == 1 and t % 512 == 0 and dbs % SAMPLE_TB == 0
    n_pages = page_table.shape[1]
    past = n_pages * PAGE_SIZE
    n_pool = cache_cmp_k.shape[1]
    w_buf = cache_win_k.shape[2]
    wlen = min(WINDOW, t)

    w_in_bf = jnp.pad(w_in, ((0, 0), (0, 0), (0, N_PAD - D_IN))).astype(BF16)
    w_out_bf, w_up_bf, w_down_bf = w_out.astype(BF16), w_up.astype(BF16), w_down.astype(BF16)
    w1cat = lambda w1: jnp.concatenate([w1[:, :CMP_STRIDE], w1[:, CMP_STRIDE:]], axis=-1).astype(BF16).reshape(
        depth, CHUNK_W, 2 * HEAD_DIM)
    pe_rows = lambda pe: jnp.pad(pe.reshape(depth, CMP_RATIO, CHUNK_W), ((0, 0), (0, SUBLANES - CMP_RATIO), (0, 0)))
    cw = dict(w1k=w1cat(cmp_w1_k), w2k=cmp_w2_k.astype(BF16), pek=pe_rows(cmp_pe_k),
              w1v=w1cat(cmp_w1_v), w2v=cmp_w2_v.astype(BF16), pev=pe_rows(cmp_pe_v))
    norm1_3, norm2_3 = norm1[:, None, :], norm2[:, None, :]
    pool_scale_3 = pool_scale[:, None, :]
    norm_f2 = norm_f[None, :]
    log_g = jnp.log1p(-jnp.exp2(-5.0 - jnp.arange(R_HEADS, dtype=F32)))
    cos_p, sin_p = _rope_tables(jnp.arange(t))
    cos_s, sin_s = _rope_tables(jnp.full((dbs,), past, jnp.int32))
    wsel = jnp.asarray(_block_sum_matrix(), BF16)
    page_view = lambda c: c.reshape(depth, n_pool, PAGE_SIZE * N_KV, HEAD_DIM)
    cmp_pages = [page_view(c) for c in (cache_cmp_k, cache_cmp_v)]
    slc_pages = [page_view(c) for c in (cache_slc_k, cache_slc_v)]
    win_k = cache_win_k.reshape(depth, dbs, w_buf * N_KV, HEAD_DIM)
    win_v = cache_win_v.reshape(depth, dbs, w_buf * N_KV, HEAD_DIM)

    mp = bsz * t
    tm_p = _row_tile(t, 1024)
    tm_mlp = _row_tile(t, 1024)
    xp = x_prompt.reshape(mp, D_MODEL)
    xs = x_sample.reshape(dbs, D_MODEL)
    outs = {k: [] for k in ("ret_p", "ret_s", "pool_p", "pool_s", "ck_p", "ck_s", "cv_p", "cv_s", "sk_p", "sk_s",
                            "sv_p", "sv_s", "wk_p", "wv_p")}
    zs3_all = []
    for l in range(depth):
        last = l == depth - 1
        z = _inproj(xp, norm1_3, w_in_bf, l, cos_p, sin_p, tm_p)
        o_r, s_new = _ret_prompt(z, log_g, bsz, t)
        o_p = _pool_prompt(z, w_pool, pool_scale_3, l, bsz, t)
        kcs, vcst, vst, vwt = _prep_prompt(z, cw, l, bsz, t)
        o_n = _nsa_prompt(z, kcs, vcst, vst, vwt, bsz, t)
        xp = _outproj(o_r, o_p, o_n, w_out_bf, l, xp, tm_p)
        xp = _mlp(xp, norm2_3, w_up_bf, w_down_bf, l, norm_f2, last, tm_mlp)
        z3 = z.reshape(bsz, t, N_PAD)
        kv = lambda off: z3[:, :, off:off + KV_W].reshape(bsz, t, N_KV, HEAD_DIM)
        outs["ret_p"].append(s_new)
        outs["pool_p"].append(z3[:, t - POOL_BUF:, OFF_PU:OFF_PU + POOL_W])
        outs["ck_p"].append(kv(OFF_KC))
        outs["cv_p"].append(kv(OFF_VC))
        outs["sk_p"].append(kv(OFF_KS))
        outs["sv_p"].append(kv(OFF_VS))
        outs["wk_p"].append(kv(OFF_KW)[:, t - wlen:])
        outs["wv_p"].append(kv(OFF_VW)[:, t - wlen:])
        zs = _inproj(xs, norm1_3, w_in_bf, l, cos_s, sin_s, dbs)
        o_r, s_new, o_p, buf_new = _state_sample(zs, log_g, state_ret, state_pool, w_pool, pool_scale_3, l, dbs, past)
        zs3 = zs.reshape(dbs, 1, N_PAD)
        res = _nsa_sample(zs3, page_table, cmp_pages, slc_pages, win_k, win_v, cw, wsel, l, dbs, past, w_buf,
                          zs3_all if last else None)
        o_n = res[0]
        zs3_all.append(zs3)
        if last:
            wk_s, wv_s = res[1:]
        xs = _outproj(o_r, o_p, o_n.reshape(dbs, NSA_W), w_out_bf, l, xs, dbs)
        xs = _mlp(xs, norm2_3, w_up_bf, w_down_bf, l, norm_f2, last, dbs)
        kvs = lambda off: zs[:, off:off + KV_W].reshape(dbs, 1, N_KV, HEAD_DIM)
        outs["ret_s"].append(s_new)
        outs["pool_s"].append(buf_new)
        outs["ck_s"].append(kvs(OFF_KC))
        outs["cv_s"].append(kvs(OFF_VC))
        outs["sk_s"].append(kvs(OFF_KS))
        outs["sv_s"].append(kvs(OFF_VS))
    st = lambda k: jnp.stack(outs[k])
    win_out = lambda w: w.reshape(depth, dbs, w_buf, N_KV, HEAD_DIM)
    return (xp.reshape(bsz, t, D_MODEL), xs.reshape(dbs, 1, D_MODEL),
            st("ret_p"), st("ret_s"), st("pool_p"), st("pool_s"),
            st("ck_p"), st("ck_s"), st("cv_p"), st("cv_s"), st("sk_p"), st("sk_s"), st("sv_p"), st("sv_s"),
            st("wk_p"), win_out(wk_s), st("wv_p"), win_out(wv_s))
```

```python
import functools
import itertools

import numpy as np
import jax
import jax.numpy as jnp
from jax import lax
from jax.experimental import pallas as pl
from jax.experimental.pallas import tpu as pltpu

F32 = jnp.float32
BF16 = jnp.bfloat16

D_MODEL = 2048
HEAD_DIM = 128
PAGE_SIZE = 128
RET_W = D_MODEL // 4
POOL_W = D_MODEL // 4
NSA_W = D_MODEL - RET_W - POOL_W
R_HEADS = RET_W // HEAD_DIM
N_HEADS = NSA_W // HEAD_DIM
N_KV = max(1, N_HEADS // 4)
HPG = N_HEADS // N_KV
KV_W = N_KV * HEAD_DIM
POOL_WINDOWS = (2, 4, 8, 16)
POOL_GW = POOL_W // len(POOL_WINDOWS)
POOL_BUF = max(POOL_WINDOWS) - 1
RET_CHUNK = 128
CMP_LEN = 32
CMP_STRIDE = 16
CMP_RATIO = CMP_LEN // CMP_STRIDE
SEL_LEN = 64
SEL_RATIO = SEL_LEN // CMP_STRIDE
SEL_TOPK = 16
N_LOCAL_SEL = 2
WINDOW = 512
Q_BLOCK = 128
D_FF = 4 * D_MODEL
ROPE_THETA = 10000.0
EPS = 1e-6
D_IN = 4 * RET_W + POOL_W + NSA_W + 6 * KV_W + 3 * N_HEADS

OFF_RQ = 0
OFF_RK = OFF_RQ + RET_W
OFF_RV = OFF_RK + RET_W
OFF_RG = OFF_RV + RET_W
OFF_PU = OFF_RG + RET_W
OFF_NQ = OFF_PU + POOL_W
OFF_KC = OFF_NQ + NSA_W
OFF_VC = OFF_KC + KV_W
OFF_KS = OFF_VC + KV_W
OFF_VS = OFF_KS + KV_W
OFF_KW = OFF_VS + KV_W
OFF_VW = OFF_KW + KV_W
OFF_GL = OFF_VW + KV_W
PROJ_TN = 512
PROJ_ROW_CHUNK = 256
N_PAD = -(-D_IN // PROJ_TN) * PROJ_TN
ROPE_FULL_TILES = (OFF_RQ // PROJ_TN, OFF_RK // PROJ_TN, OFF_NQ // PROJ_TN, OFF_NQ // PROJ_TN + 1)
ROPE_HALF_TILES = (OFF_KC // PROJ_TN, OFF_KS // PROJ_TN, OFF_KW // PROJ_TN)
RK_TILE = OFF_RK // PROJ_TN

NEG = -1e30
V7X_VMEM_BYTES = 64 * 1024 * 1024
VMEM_LIMIT = V7X_VMEM_BYTES * 3 // 4
VMEM_LIMIT_BIG = V7X_VMEM_BYTES * 29 // 32
LANES = 128
SUBLANES = 8


def _cparams(sem, vmem_limit=VMEM_LIMIT):
    return pltpu.CompilerParams(dimension_semantics=sem, vmem_limit_bytes=vmem_limit)


def _dot(a, b):
    return jnp.dot(a, b, preferred_element_type=F32)


def _dot_nt(a, b):
    return lax.dot_general(a, b, (((1,), (1,)), ((), ())), preferred_element_type=F32)


def _dot_tn(a, b):
    return lax.dot_general(a, b, (((0,), (0,)), ((), ())), preferred_element_type=F32)


def _rms(x, g):
    return x * lax.rsqrt(jnp.mean(x * x, axis=-1, keepdims=True) + EPS) * g


def _silu(x):
    return x * jax.nn.sigmoid(x)


def _inproj_kernel(x_ref, g_ref, w_ref, cos_ref, sin_ref, z_ref, hn_ref):
    j = pl.program_id(1)

    @pl.when(j == 0)
    def _():
        hn_ref[...] = _rms(x_ref[...], g_ref[...]).astype(BF16)

    is_full = functools.reduce(jnp.logical_or, [j == t for t in ROPE_FULL_TILES])
    is_half = functools.reduce(jnp.logical_or, [j == t for t in ROPE_HALF_TILES])
    scale = jnp.where(j == RK_TILE, HEAD_DIM ** -0.5, 1.0).astype(F32)
    is_any = jnp.logical_or(is_full, is_half)
    tm = x_ref.shape[0]
    rc = min(tm, PROJ_ROW_CHUNK)
    for r0 in range(0, tm, rc):
        rows = slice(r0, r0 + rc)
        z = _dot(hn_ref[rows, :], w_ref[...])
        cos_t, sin_t = cos_ref[rows, :] * scale, sin_ref[rows, :] * scale
        tab_lead = (jnp.where(is_any, cos_t, 1.0), jnp.where(is_any, sin_t, 0.0))
        tab_rest = (jnp.where(is_full, cos_t, 1.0), jnp.where(is_full, sin_t, 0.0))
        for s in range(PROJ_TN // HEAD_DIM):
            sl = slice(s * HEAD_DIM, (s + 1) * HEAD_DIM)
            c, sn = tab_lead if s < KV_W // HEAD_DIM else tab_rest
            zs = z[:, sl]
            z_ref[rows, sl] = zs * c + pltpu.roll(zs, HEAD_DIM // 2, 1) * sn


def _inproj(x, gamma, w_in_bf, layer, cos_t, sin_t, tm):
    m = x.shape[0]
    n_row_tiles = cos_t.shape[0] // tm
    return pl.pallas_call(
        _inproj_kernel,
        grid=(m // tm, N_PAD // PROJ_TN),
        in_specs=[
            pl.BlockSpec((tm, D_MODEL), lambda i, j: (i, 0)),
            pl.BlockSpec((None, 1, D_MODEL), lambda i, j: (layer, 0, 0)),
            pl.BlockSpec((None, D_MODEL, PROJ_TN), lambda i, j: (layer, 0, j)),
            pl.BlockSpec((tm, HEAD_DIM), lambda i, j: (i % n_row_tiles, 0)),
            pl.BlockSpec((tm, HEAD_DIM), lambda i, j: (i % n_row_tiles, 0)),
        ],
        out_specs=pl.BlockSpec((tm, PROJ_TN), lambda i, j: (i, j)),
        out_shape=jax.ShapeDtypeStruct((m, N_PAD), F32),
        scratch_shapes=[pltpu.VMEM((tm, D_MODEL), BF16)],
        compiler_params=_cparams(("parallel", "arbitrary")),
        name="inproj",
    )(x, gamma, w_in_bf, cos_t, sin_t)


def _outproj_kernel(r_ref, p_ref, n_ref, w_ref, x_ref, o_ref):
    acc = _dot(r_ref[...].astype(BF16), w_ref[0:RET_W, :])
    acc += _dot(p_ref[...].astype(BF16), w_ref[RET_W:RET_W + POOL_W, :])
    acc += _dot(n_ref[...].astype(BF16), w_ref[RET_W + POOL_W:, :])
    o_ref[...] = x_ref[...] + acc


def _outproj(o_r, o_p, o_n, w_out_bf, layer, x, tm, tn=D_MODEL):
    m = x.shape[0]
    return pl.pallas_call(
        _outproj_kernel,
        grid=(m // tm, D_MODEL // tn),
        in_specs=[
            pl.BlockSpec((tm, RET_W), lambda i, j: (i, 0)),
            pl.BlockSpec((tm, POOL_W), lambda i, j: (i, 0)),
            pl.BlockSpec((tm, NSA_W), lambda i, j: (i, 0)),
            pl.BlockSpec((None, D_MODEL, tn), lambda i, j: (layer, 0, j)),
            pl.BlockSpec((tm, tn), lambda i, j: (i, j)),
        ],
        out_specs=pl.BlockSpec((tm, tn), lambda i, j: (i, j)),
        out_shape=jax.ShapeDtypeStruct((m, D_MODEL), F32),
        compiler_params=_cparams(("parallel", "parallel")),
        name="outproj",
    )(o_r, o_p, o_n, w_out_bf, x)


def _mlp_kernel(x_ref, g_ref, wu_ref, wd_ref, gf_ref, o_ref, hn_ref, *, final_norm):
    f = pl.program_id(1)

    @pl.when(f == 0)
    def _():
        hn_ref[...] = _rms(x_ref[...], g_ref[...]).astype(BF16)
        o_ref[...] = jnp.zeros_like(o_ref)

    u = jnp.maximum(_dot(hn_ref[...], wu_ref[...]), 0.0)
    o_ref[...] += _dot((u * u).astype(BF16), wd_ref[...])

    @pl.when(f == pl.num_programs(1) - 1)
    def _():
        y = x_ref[...] + o_ref[...]
        o_ref[...] = _rms(y, gf_ref[...]) if final_norm else y


def _mlp(x, gamma, w_up_bf, w_down_bf, layer, gamma_f, final_norm, tm, tf=512):
    m = x.shape[0]
    return pl.pallas_call(
        functools.partial(_mlp_kernel, final_norm=final_norm),
        grid=(m // tm, D_FF // tf),
        in_specs=[
            pl.BlockSpec((tm, D_MODEL), lambda i, f: (i, 0)),
            pl.BlockSpec((None, 1, D_MODEL), lambda i, f: (layer, 0, 0)),
            pl.BlockSpec((None, D_MODEL, tf), lambda i, f: (layer, 0, f)),
            pl.BlockSpec((None, tf, D_MODEL), lambda i, f: (layer, f, 0)),
            pl.BlockSpec((1, D_MODEL), lambda i, f: (0, 0)),
        ],
        out_specs=pl.BlockSpec((tm, D_MODEL), lambda i, f: (i, 0)),
        out_shape=jax.ShapeDtypeStruct((m, D_MODEL), F32),
        scratch_shapes=[pltpu.VMEM((tm, D_MODEL), BF16)],
        compiler_params=_cparams(("parallel", "arbitrary"), VMEM_LIMIT_BIG),
        name="mlp",
    )(x, gamma, w_up_bf, w_down_bf, gamma_f)


def _ret_prompt_kernel(lg_ref, q_ref, k_ref, v_ref, g_ref, o_ref, s_ref, *, n_chunk):
    ti = pl.program_id(1)
    L = RET_CHUNK
    ri = lax.broadcasted_iota(jnp.int32, (L, L), 0)
    ci = lax.broadcasted_iota(jnp.int32, (L, L), 1)
    diff = (ri - ci).astype(F32)
    causal = diff >= 0
    idx = lax.broadcasted_iota(jnp.int32, (L, 1), 0).astype(F32)
    consts = []
    for h in range(R_HEADS):
        lg = lg_ref[h]
        decay = jnp.where(causal, jnp.exp(jnp.where(causal, diff, 0.0) * lg), 0.0)
        q_dec = jnp.exp((idx + 1.0) * lg)
        k_dec = jnp.exp((L - 1.0 - idx) * lg)
        s_dec = jnp.exp(jnp.full((1, 1), float(L), F32) * lg)
        consts.append((decay, q_dec, k_dec, s_dec))

    @pl.when(ti == 0)
    def _():
        s_ref[...] = jnp.zeros_like(s_ref)

    def body(c, states):
        rows = pl.ds(pl.multiple_of(c * L, L), L)
        new_states = []
        for h, (decay, q_dec, k_dec, s_dec) in enumerate(consts):
            sl = slice(h * HEAD_DIM, (h + 1) * HEAD_DIM)
            q, k, v = q_ref[rows, sl], k_ref[rows, sl], v_ref[rows, sl]
            vb = v.astype(BF16)
            att = _dot_nt(q.astype(BF16), k.astype(BF16)) * decay
            o = _dot(att.astype(BF16), vb) + _dot((q * q_dec).astype(BF16), states[h].astype(BF16))
            new_states.append(s_dec * states[h] + _dot_tn((k * k_dec).astype(BF16), vb))
            o = o * lax.rsqrt(jnp.mean(o * o, axis=-1, keepdims=True) + EPS)
            o_ref[rows, sl] = o * _silu(g_ref[rows, sl])
        return tuple(new_states)

    states = lax.fori_loop(0, n_chunk, body, tuple(s_ref[h] for h in range(R_HEADS)))
    for h in range(R_HEADS):
        s_ref[h] = states[h]


def _ret_prompt(z, log_g, bsz, t, tt=1024):
    tt = min(tt, t)
    nt = t // tt
    blk = lambda off: pl.BlockSpec((tt, RET_W), lambda b, i, lg: (b * nt + i, off // RET_W))
    return pl.pallas_call(
        functools.partial(_ret_prompt_kernel, n_chunk=tt // RET_CHUNK),
        grid_spec=pltpu.PrefetchScalarGridSpec(
            num_scalar_prefetch=1,
            grid=(bsz, nt),
            in_specs=[blk(OFF_RQ), blk(OFF_RK), blk(OFF_RV), blk(OFF_RG)],
            out_specs=[
                pl.BlockSpec((tt, RET_W), lambda b, i, lg: (b * nt + i, 0)),
                pl.BlockSpec((None, R_HEADS, HEAD_DIM, HEAD_DIM), lambda b, i, lg: (b, 0, 0, 0)),
            ],
        ),
        out_shape=[
            jax.ShapeDtypeStruct((bsz * t, RET_W), F32),
            jax.ShapeDtypeStruct((bsz, R_HEADS, HEAD_DIM, HEAD_DIM), F32),
        ],
        compiler_params=_cparams(("parallel", "arbitrary")),
        name="ret_prompt",
    )(log_g, z, z, z, z)


POOL_HALO = 16


def _pool_prompt_kernel(u_ref, wp_ref, ps_ref, o_ref, ext_ref, *, tt):
    t = pl.program_id(1)

    @pl.when(t == 0)
    def _():
        ext_ref[0:POOL_HALO, :] = jnp.zeros((POOL_HALO, POOL_W), F32)

    u = u_ref[...]
    ext_ref[POOL_HALO:POOL_HALO + tt, :] = u
    pos = t * tt + lax.broadcasted_iota(jnp.int32, (tt, 1), 0)
    for g, w in enumerate(POOL_WINDOWS):
        sl = slice(g * POOL_GW, (g + 1) * POOL_GW)
        s = u[:, sl]
        for i in range(1, w):
            s = s + ext_ref[pl.ds(POOL_HALO - i, tt), sl]
        cnt = jnp.minimum(pos + 1, w).astype(F32)
        p = s / cnt - u[:, sl]
        o_ref[:, sl] = _dot(p.astype(BF16), wp_ref[g].astype(BF16)) * ps_ref[:, sl]
    ext_ref[0:POOL_HALO, :] = ext_ref[tt:tt + POOL_HALO, :]


def _pool_prompt(z, w_pool, pool_scale, layer, bsz, t, tt=512):
    nt = t // tt
    return pl.pallas_call(
        functools.partial(_pool_prompt_kernel, tt=tt),
        grid=(bsz, nt),
        in_specs=[
            pl.BlockSpec((tt, POOL_W), lambda b, i: (b * nt + i, OFF_PU // POOL_W)),
            pl.BlockSpec((None, len(POOL_WINDOWS), POOL_GW, POOL_GW), lambda b, i: (layer, 0, 0, 0)),
            pl.BlockSpec((None, 1, POOL_W), lambda b, i: (layer, 0, 0)),
        ],
        out_specs=pl.BlockSpec((tt, POOL_W), lambda b, i: (b * nt + i, 0)),
        out_shape=jax.ShapeDtypeStruct((bsz * t, POOL_W), F32),
        scratch_shapes=[pltpu.VMEM((POOL_HALO + tt, POOL_W), F32)],
        compiler_params=_cparams(("parallel", "arbitrary")),
        name="pool_prompt",
    )(z, w_pool, pool_scale)


CHUNK_W = CMP_STRIDE * HEAD_DIM
PE_ROWS = 16


def _compress_core(chunks, w1_ref, w2_ref, pe_ref):
    n_rows = chunks.shape[0]
    acc = _dot(jnp.concatenate([chunks, pe_ref[...]], axis=0), w1_ref[...])
    first = acc[:n_rows, :HEAD_DIM]
    second = acc[:n_rows, HEAD_DIM:]
    pe_term = acc[n_rows:n_rows + 1, :HEAD_DIM] + acc[n_rows + 1:n_rows + 2, HEAD_DIM:]
    h = first + pltpu.roll(second, n_rows - 1, 0) + pe_term
    return _dot(_silu(h).astype(BF16), w2_ref[...])


def _prep_prompt_kernel(kc_ref, vc_ref, vs_ref, vw_ref, w1k_ref, w2k_ref, pek_ref, w1v_ref, w2v_ref, pev_ref,
                        kcs_ref, vcst_ref, vst_ref, vwt_ref, *, n_chunk, n_tile):
    chunks_of = lambda r: jnp.concatenate(
        [r[pl.ds(s, n_chunk, stride=CMP_STRIDE), :].astype(BF16) for s in range(CMP_STRIDE)], axis=1)
    kcs_ref[...] = _compress_core(chunks_of(kc_ref), w1k_ref, w2k_ref, pek_ref).astype(BF16)
    vcst_ref[...] = _compress_core(chunks_of(vc_ref), w1v_ref, w2v_ref, pev_ref).T.astype(BF16)

    def tbody(i, carry):
        rows = pl.ds(pl.multiple_of(i * HEAD_DIM, HEAD_DIM), HEAD_DIM)
        vst_ref[i] = vs_ref[rows, :].T.astype(BF16)
        vwt_ref[i] = vw_ref[rows, :].T.astype(BF16)
        return carry

    lax.fori_loop(0, n_tile, tbody, 0)


def _prep_prompt(z, cw, layer, bsz, t):
    n_chunk = t // CMP_STRIDE
    n_tile = t // HEAD_DIM
    col = lambda off: pl.BlockSpec((t, HEAD_DIM), lambda b, g: (b, off // HEAD_DIM + g))
    w1 = pl.BlockSpec((None, CHUNK_W, 2 * HEAD_DIM), lambda b, g: (layer, 0, 0))
    w2 = pl.BlockSpec((None, HEAD_DIM, HEAD_DIM), lambda b, g: (layer, 0, 0))
    pe = pl.BlockSpec((None, PE_ROWS, CHUNK_W), lambda b, g: (layer, 0, 0))
    return pl.pallas_call(
        functools.partial(_prep_prompt_kernel, n_chunk=n_chunk, n_tile=n_tile),
        grid=(bsz, N_KV),
        in_specs=[col(OFF_KC), col(OFF_VC), col(OFF_VS), col(OFF_VW), w1, w2, pe, w1, w2, pe],
        out_specs=[
            pl.BlockSpec((None, None, n_chunk, HEAD_DIM), lambda b, g: (b, g, 0, 0)),
            pl.BlockSpec((None, None, HEAD_DIM, n_chunk), lambda b, g: (b, g, 0, 0)),
            pl.BlockSpec((None, None, n_tile, HEAD_DIM, HEAD_DIM), lambda b, g: (b, g, 0, 0, 0)),
            pl.BlockSpec((None, None, n_tile, HEAD_DIM, HEAD_DIM), lambda b, g: (b, g, 0, 0, 0)),
        ],
        out_shape=[
            jax.ShapeDtypeStruct((bsz, N_KV, n_chunk, HEAD_DIM), BF16),
            jax.ShapeDtypeStruct((bsz, N_KV, HEAD_DIM, n_chunk), BF16),
            jax.ShapeDtypeStruct((bsz, N_KV, n_tile, HEAD_DIM, HEAD_DIM), BF16),
            jax.ShapeDtypeStruct((bsz, N_KV, n_tile, HEAD_DIM, HEAD_DIM), BF16),
        ],
        compiler_params=_cparams(("parallel", "parallel")),
        name="prep_prompt",
    )(z, z, z, z, cw["w1k"], cw["w2k"], cw["pek"], cw["w1v"], cw["w2v"], cw["pev"])


LOG2E = 1.4426950408889634


def _softmax_tiles(tiles, carry):
    m, l, acc = carry
    m_new = m
    for s, _ in tiles:
        m_new = jnp.maximum(m_new, jnp.max(s, axis=0, keepdims=True))
    alpha = jnp.exp2(m - m_new)
    l = alpha * l
    acc = alpha * acc
    for s, vt in tiles:
        p = jnp.exp2(s - m_new)
        l = l + jnp.sum(p, axis=0, keepdims=True)
        acc = acc + _dot(vt, p.astype(BF16))
    return m_new, l, acc


def _select_blocks(score, score_ref):
    n_blocks = score.shape[0]
    score_ref[...] = score
    groups = [score[v * SUBLANES:(v + 1) * SUBLANES] for v in range(n_blocks // SUBLANES)]
    jb = lax.broadcasted_iota(jnp.int32, groups[0].shape, 0)
    ranks = [jnp.zeros(gv.shape, F32) for gv in groups]
    for k in range(n_blocks):
        rk = score_ref[pl.ds(k, 1), :]
        for v, gv in enumerate(groups):
            if v * SUBLANES > k:
                beats = rk >= gv
            elif (v + 1) * SUBLANES - 1 <= k:
                beats = rk > gv
            else:
                beats = jnp.logical_or(rk > gv, jnp.logical_and(jb + v * SUBLANES > k, rk == gv))
            ranks[v] = ranks[v] + jnp.where(beats, 1.0, 0.0)
    rank = jnp.concatenate(ranks, axis=0)
    return jnp.logical_and(rank < SEL_TOPK, score > -jnp.inf)


def _nsa_prompt_kernel(q_ref, gl_ref, kcs_ref, vcst_ref, ks_ref, vst_ref, kw_ref, vwt_ref, o_ref,
                       imp_ref, selb_ref, gt_ref, score_ref, *, n_chunk, n_sel):
    g = pl.program_id(1)
    i = pl.program_id(2)
    qw = HPG * Q_BLOCK
    qscale = HEAD_DIM ** -0.5 * LOG2E
    q = q_ref[...]
    qt = jnp.concatenate([(q[:, h * HEAD_DIM:(h + 1) * HEAD_DIM] * qscale).T for h in range(HPG)],
                         axis=1).astype(BF16)
    lane = lax.broadcasted_iota(jnp.int32, (1, qw), 1)
    tl4 = jnp.bitwise_and(lane, Q_BLOCK - 1)
    tpos4 = i * Q_BLOCK + tl4
    tpos = i * Q_BLOCK + lax.broadcasted_iota(jnp.int32, (1, Q_BLOCK), 1)

    sc = _dot(kcs_ref[...], qt)
    c_end = lax.broadcasted_iota(jnp.int32, (n_chunk, 1), 0) * CMP_STRIDE + (CMP_LEN - 1)
    mask_c = jnp.logical_and(c_end <= tpos4, c_end < n_chunk * CMP_STRIDE)
    m = jnp.max(jnp.where(mask_c, sc, NEG), axis=0, keepdims=True)
    e = jnp.where(mask_c, jnp.exp2(sc - m), 0.0)
    d = jnp.sum(e, axis=0, keepdims=True)
    pc = e * (1.0 / jnp.where(d > 0, d, 1.0))
    oc = _dot(vcst_ref[...], pc.astype(BF16))
    imp = pc[:, 0:Q_BLOCK]
    for h in range(1, HPG):
        imp = imp + pc[:, h * Q_BLOCK:(h + 1) * Q_BLOCK]

    imp_ref[0:SUBLANES, :] = jnp.zeros((SUBLANES, Q_BLOCK), F32)
    imp_ref[SUBLANES:SUBLANES + n_chunk, :] = imp
    imp_ref[SUBLANES + n_chunk:, :] = jnp.zeros((SUBLANES, Q_BLOCK), F32)
    p_slc = None
    for mm in range(SEL_RATIO):
        for nn in range(CMP_RATIO):
            r = imp_ref[pl.ds(SUBLANES + mm - nn, n_sel, stride=SEL_RATIO), :]
            p_slc = r if p_slc is None else p_slc + r
    jb = lax.broadcasted_iota(jnp.int32, (n_sel, 1), 0)
    blk_valid = jb * SEL_LEN <= tpos
    cur = lax.shift_right_logical(tpos, int(np.log2(SEL_LEN)))
    forced = jnp.logical_or(jb == 0, jnp.logical_and(jb <= cur, jb > cur - N_LOCAL_SEL))
    score = jnp.where(blk_valid, jnp.where(forced, jnp.inf, p_slc), -jnp.inf)
    sel = _select_blocks(score, score_ref)
    selb_ref[...] = jnp.concatenate([jnp.where(sel, 0.0, NEG)] * HPG, axis=1)

    row = lax.broadcasted_iota(jnp.int32, (Q_BLOCK, 1), 0)
    causal_bias = jnp.where(row <= tl4, 0.0, NEG)
    init = (jnp.full((1, qw), NEG, F32), jnp.zeros((1, qw), F32), jnp.zeros((HEAD_DIM, qw), F32))
    blocks_per_tile = Q_BLOCK // SEL_LEN

    n_back = WINDOW // Q_BLOCK
    win_tiles = []
    for w in range(n_back + 1):
        kb = i - n_back + w
        kbc = jnp.maximum(kb, 0)
        rows = pl.ds(pl.multiple_of(kbc * Q_BLOCK, Q_BLOCK), Q_BLOCK)
        s = _dot(kw_ref[rows, :].astype(BF16), qt)
        if w == n_back:
            s = s + causal_bias
        else:
            before_start = jnp.where(kb >= 0, 0.0, NEG).astype(F32)
            s = s + (jnp.where(row > tl4, before_start, NEG) if w == 0 else before_start)
        win_tiles.append((s, vwt_ref[kbc]))
    _, l_w, a_w = _softmax_tiles(win_tiles, init)

    n_pair = lax.shift_right_logical(i, 1)
    odd_bias = jnp.where(jnp.bitwise_and(i, 1) == 1, 0.0, NEG).astype(F32)

    def sel_scores(kt, tile_bias):
        rows = pl.ds(pl.multiple_of(kt * Q_BLOCK, Q_BLOCK), Q_BLOCK)
        s = _dot(ks_ref[rows, :].astype(BF16), qt)
        parts = []
        for bb in range(blocks_per_tile):
            bias = selb_ref[pl.ds(kt * blocks_per_tile + bb, 1), :] + tile_bias
            parts.append(s[bb * SEL_LEN:(bb + 1) * SEL_LEN] + bias)
        return jnp.concatenate(parts, axis=0)

    def update_scores(u):
        is_last = u == n_pair
        s_a = sel_scores(jnp.where(is_last, jnp.maximum(i - 1, 0), 2 * u), jnp.where(is_last, odd_bias, 0.0))
        s_b = sel_scores(jnp.where(is_last, i, 2 * u + 1), 0.0) + jnp.where(is_last, causal_bias, 0.0)
        return s_a, s_b

    def sel_body(u, carry):
        m, l, acc, s_a, s_b = carry
        nxt = update_scores(u + 1)
        m, l, acc = _softmax_tiles([(s_a, vst_ref[2 * u]), (s_b, vst_ref[2 * u + 1])], (m, l, acc))
        return (m, l, acc) + nxt

    m_s, l_s, a_s, s_a, s_b = lax.fori_loop(0, n_pair, sel_body, init + update_scores(0))
    _, l_s, a_s = _softmax_tiles([(s_a, vst_ref[jnp.maximum(i - 1, 0)]), (s_b, vst_ref[i])], (m_s, l_s, a_s))

    gt_ref[...] = jax.nn.sigmoid(gl_ref[...]).T
    gate = lambda c: jnp.concatenate(
        [gt_ref[pl.ds((g * HPG + h) * 3 + c, 1), :] for h in range(HPG)], axis=1)
    ot = gate(0) * oc + gate(1) * (a_s * (1.0 / l_s)) + gate(2) * (a_w * (1.0 / l_w))
    for h in range(HPG):
        o_ref[:, h * HEAD_DIM:(h + 1) * HEAD_DIM] = ot[:, h * Q_BLOCK:(h + 1) * Q_BLOCK].T


def _nsa_prompt(z, kcs, vcst, vst, vwt, bsz, t):
    n_chunk = t // CMP_STRIDE
    n_sel = t // SEL_LEN
    n_tile = t // Q_BLOCK
    nqb = t // Q_BLOCK
    qw = HPG * HEAD_DIM
    return pl.pallas_call(
        functools.partial(_nsa_prompt_kernel, n_chunk=n_chunk, n_sel=n_sel),
        grid=(bsz, N_KV, nqb),
        in_specs=[
            pl.BlockSpec((Q_BLOCK, qw), lambda b, g, i: (b * nqb + i, OFF_NQ // qw + g)),
            pl.BlockSpec((Q_BLOCK, LANES), lambda b, g, i: (b * nqb + i, OFF_GL // LANES)),
            pl.BlockSpec((None, None, n_chunk, HEAD_DIM), lambda b, g, i: (b, g, 0, 0)),
            pl.BlockSpec((None, None, HEAD_DIM, n_chunk), lambda b, g, i: (b, g, 0, 0)),
            pl.BlockSpec((t, HEAD_DIM), lambda b, g, i: (b, OFF_KS // HEAD_DIM + g)),
            pl.BlockSpec((None, None, n_tile, HEAD_DIM, HEAD_DIM), lambda b, g, i: (b, g, 0, 0, 0)),
            pl.BlockSpec((t, HEAD_DIM), lambda b, g, i: (b, OFF_KW // HEAD_DIM + g)),
            pl.BlockSpec((None, None, n_tile, HEAD_DIM, HEAD_DIM), lambda b, g, i: (b, g, 0, 0, 0)),
        ],
        out_specs=pl.BlockSpec((Q_BLOCK, qw), lambda b, g, i: (b * nqb + i, g)),
        out_shape=jax.ShapeDtypeStruct((bsz * t, NSA_W), F32),
        scratch_shapes=[
            pltpu.VMEM((n_chunk + 2 * SUBLANES, Q_BLOCK), F32),
            pltpu.VMEM((n_sel, HPG * Q_BLOCK), F32),
            pltpu.VMEM((LANES, Q_BLOCK), F32),
            pltpu.VMEM((n_sel, Q_BLOCK), F32),
        ],
        compiler_params=_cparams(("parallel", "parallel", "arbitrary")),
        name="nsa_prompt",
    )(z, z, kcs, vcst, z, vst, z, vwt)


SAMPLE_TB = 8


def _state_sample_kernel(lg_ref, q_ref, k_ref, v_ref, g_ref, u_ref, sr_ref, sp_ref, wp_ref, ps_ref,
                         or_ref, sn_ref, op_ref, pn_ref, *, pos):
    tb = SAMPLE_TB
    pad = jnp.zeros((SUBLANES - 1, HEAD_DIM), F32)
    for h in range(R_HEADS):
        gamma = jnp.exp(jnp.full((1, HEAD_DIM), lg_ref[h], F32))
        sl = slice(h * HEAD_DIM, (h + 1) * HEAD_DIM)
        rows = []
        for b in range(tb):
            q = q_ref[b:b + 1, sl]
            k = k_ref[b:b + 1, sl]
            v = v_ref[b:b + 1, sl]
            state = sr_ref[b, h]
            k8 = jnp.concatenate([k, pad], axis=0).astype(BF16)
            v8 = jnp.concatenate([v, pad], axis=0).astype(BF16)
            qk = jnp.sum(q * k, axis=1, keepdims=True)
            qs = _dot(jnp.concatenate([q * gamma, pad], axis=0).astype(BF16), state.astype(BF16))[0:1]
            rows.append(qk * v + qs)
            sn_ref[b, h] = gamma[:, 0:1] * state + _dot_tn(k8, v8)
        o = jnp.concatenate(rows, axis=0)
        o = o * lax.rsqrt(jnp.mean(o * o, axis=-1, keepdims=True) + EPS)
        or_ref[:, sl] = o * _silu(g_ref[:, sl])
    u = u_ref[...]
    for gi, w in enumerate(POOL_WINDOWS):
        sl = slice(gi * POOL_GW, (gi + 1) * POOL_GW)
        s = u[:, sl] + jnp.sum(sp_ref[:, POOL_BUF - (w - 1):POOL_BUF, sl], axis=1)
        p = s / float(min(pos + 1, w)) - u[:, sl]
        op_ref[:, sl] = _dot(p.astype(BF16), wp_ref[gi].astype(BF16)) * ps_ref[:, sl]
    pn_ref[:, 0:POOL_BUF - 1, :] = sp_ref[:, 1:POOL_BUF, :]
    for b in range(tb):
        pn_ref[b, POOL_BUF - 1:POOL_BUF, :] = u[b:b + 1, :]


def _state_sample(zs, log_g, state_ret, state_pool, w_pool, pool_scale, layer, dbs, pos):
    tb = SAMPLE_TB
    col = lambda off: pl.BlockSpec((tb, RET_W), lambda i, lg: (i, off // RET_W))
    return pl.pallas_call(
        functools.partial(_state_sample_kernel, pos=pos),
        grid_spec=pltpu.PrefetchScalarGridSpec(
            num_scalar_prefetch=1,
            grid=(dbs // tb,),
            in_specs=[
                col(OFF_RQ), col(OFF_RK), col(OFF_RV), col(OFF_RG), col(OFF_PU),
                pl.BlockSpec((None, tb, R_HEADS, HEAD_DIM, HEAD_DIM), lambda i, lg: (layer, i, 0, 0, 0)),
                pl.BlockSpec((None, tb, POOL_BUF, POOL_W), lambda i, lg: (layer, i, 0, 0)),
                pl.BlockSpec((None, len(POOL_WINDOWS), POOL_GW, POOL_GW), lambda i, lg: (layer, 0, 0, 0)),
                pl.BlockSpec((None, 1, POOL_W), lambda i, lg: (layer, 0, 0)),
            ],
            out_specs=[
                pl.BlockSpec((tb, RET_W), lambda i, lg: (i, 0)),
                pl.BlockSpec((tb, R_HEADS, HEAD_DIM, HEAD_DIM), lambda i, lg: (i, 0, 0, 0)),
                pl.BlockSpec((tb, POOL_W), lambda i, lg: (i, 0)),
                pl.BlockSpec((tb, POOL_BUF, POOL_W), lambda i, lg: (i, 0, 0)),
            ],
        ),
        out_shape=[
            jax.ShapeDtypeStruct((dbs, RET_W), F32),
            jax.ShapeDtypeStruct((dbs, R_HEADS, HEAD_DIM, HEAD_DIM), F32),
            jax.ShapeDtypeStruct((dbs, POOL_W), F32),
            jax.ShapeDtypeStruct((dbs, POOL_BUF, POOL_W), F32),
        ],
        compiler_params=_cparams(("parallel",)),
        name="state_sample",
    )(log_g, zs, zs, zs, zs, zs, state_ret, state_pool, w_pool, pool_scale)


N_CACHE = 4


def _softmax_lanes_with_new(s, s_new):
    m = jnp.maximum(jnp.max(s, axis=1, keepdims=True), s_new)
    p = jnp.exp(s - m)
    p_new = jnp.exp(s_new - m)
    inv = 1.0 / (jnp.sum(p, axis=1, keepdims=True) + p_new)
    return p * inv, p_new * inv


def _rank_select(srow):
    across = jnp.broadcast_to(srow, (LANES, LANES))
    down = across.T
    r_i = lax.broadcasted_iota(jnp.int32, (LANES, LANES), 0)
    c_i = lax.broadcasted_iota(jnp.int32, (LANES, LANES), 1)
    beats = jnp.logical_or(down > across, jnp.logical_and(down == across, r_i < c_i))
    rank = jnp.sum(jnp.where(beats, 1.0, 0.0), axis=0, keepdims=True)
    return jnp.logical_and(rank < SEL_TOPK, srow > -jnp.inf)


def _nsa_sample_sequence(q0_ref, q1_ref, ksn_ref, vsn_ref, kwn_ref, vwn_ref, gl_ref, wsel_ref,
                         w1k_ref, w2k_ref, pek_ref, w1v_ref, w2v_ref, pev_ref, wk_ref, wv_ref,
                         ck, cv, sk, sv, o_ref, *, n_pages, past, w_buf):
    scale = HEAD_DIM ** -0.5
    chunks_per_page = PAGE_SIZE // CMP_STRIDE
    n_chunk = n_pages * chunks_per_page
    n_sel = -(-(past + 1) // SEL_LEN)
    assert n_chunk <= LANES and n_sel <= LANES and past % SEL_LEN == 0

    group_rows = lambda ref, g, n: ref[pl.ds(g, n, stride=N_KV), :]

    def chunks_of(prefs):
        return jnp.concatenate(
            [group_rows(p, g, PAGE_SIZE).astype(BF16).reshape(chunks_per_page, CHUNK_W)
             for g in range(N_KV) for p in prefs],
            axis=0)

    ksum = _compress_core(chunks_of(ck), w1k_ref, w2k_ref, pek_ref)
    yield
    vsum = _compress_core(chunks_of(cv), w1v_ref, w2v_ref, pev_ref)
    yield

    lane = lax.broadcasted_iota(jnp.int32, (1, LANES), 1)
    q_refs = (q0_ref, q1_ref)
    zpad = jnp.zeros((SUBLANES - HPG, HEAD_DIM), F32)
    q8, oc, imps = [], [], []
    for g in range(N_KV):
        qg = jnp.concatenate([q_refs[g][:, h * HEAD_DIM:(h + 1) * HEAD_DIM] for h in range(HPG)] + [zpad],
                             axis=0) * scale
        q8.append(qg.astype(BF16))
        kcs = ksum[g * n_chunk:(g + 1) * n_chunk]
        vcs = vsum[g * n_chunk:(g + 1) * n_chunk]
        s_c = _dot_nt(q8[g], kcs.astype(BF16))
        c_end = lane[:, :n_chunk] * CMP_STRIDE + (CMP_LEN - 1)
        mask_c = jnp.logical_and(c_end <= past, c_end < n_chunk * CMP_STRIDE)
        m = jnp.max(jnp.where(mask_c, s_c, NEG), axis=1, keepdims=True)
        e = jnp.where(mask_c, jnp.exp(s_c - m), 0.0)
        d = jnp.sum(e, axis=1, keepdims=True)
        pc = e * (1.0 / jnp.where(d > 0, d, 1.0))
        oc.append(_dot(pc.astype(BF16), vcs.astype(BF16)))
        imps.append(jnp.sum(pc[0:HPG], axis=0, keepdims=True))
    yield

    imp = jnp.concatenate(imps + [jnp.zeros((SUBLANES - N_KV, n_chunk), F32)], axis=0)
    if n_chunk < LANES:
        imp = jnp.concatenate([imp, jnp.zeros((SUBLANES, LANES - n_chunk), F32)], axis=1)
    hi = imp.astype(BF16)
    r1 = imp - hi.astype(F32)
    mid = r1.astype(BF16)
    lo = (r1 - mid.astype(F32)).astype(BF16)
    wsel = wsel_ref[...]
    p_slc = _dot(hi, wsel) + _dot(mid, wsel) + _dot(lo, wsel)
    cur = past // SEL_LEN
    blk_valid = jnp.logical_and(lane * SEL_LEN <= past, lane < n_sel)
    forced = jnp.logical_or(lane == 0, jnp.logical_and(lane <= cur, lane > cur - N_LOCAL_SEL))
    score = jnp.where(blk_valid, jnp.where(forced, jnp.inf, p_slc), -jnp.inf)
    yield

    gate = jax.nn.sigmoid(gl_ref[...])
    blocks_per_page = PAGE_SIZE // SEL_LEN
    wl = lax.broadcasted_iota(jnp.int32, (1, w_buf), 1)
    dpos = w_buf - wl
    bias_w = jnp.where(jnp.logical_and(jnp.logical_and(dpos >= 0, dpos < WINDOW), past - dpos >= 0), 0.0, NEG)
    for g in range(N_KV):
        kv_new = slice(g * HEAD_DIM, (g + 1) * HEAD_DIM)
        q_f = q8[g].astype(F32)
        selb = jnp.where(_rank_select(score[g:g + 1]), 0.0, NEG)
        bias_pages = []
        for p in range(n_pages):
            bias = selb[:, p * blocks_per_page:p * blocks_per_page + 1]
            for bb in range(1, blocks_per_page):
                bias = jnp.where(lane < bb * SEL_LEN, bias, selb[:, p * blocks_per_page + bb:p * blocks_per_page + bb + 1])
            bias_pages.append(bias)
        k_all = jnp.concatenate([group_rows(p, g, PAGE_SIZE) for p in sk], axis=0).astype(BF16)
        v_all = jnp.concatenate([group_rows(p, g, PAGE_SIZE) for p in sv], axis=0).astype(BF16)
        s_all = _dot_nt(q8[g], k_all) + jnp.concatenate(bias_pages, axis=1)
        s_new = jnp.sum(q_f * ksn_ref[:, kv_new].astype(BF16).astype(F32), axis=1, keepdims=True)
        p_all, p_new = _softmax_lanes_with_new(s_all, s_new)
        o_s = p_new * vsn_ref[:, kv_new] + _dot(p_all.astype(BF16), v_all)
        yield
        s_w = _dot_nt(q8[g], group_rows(wk_ref, g, w_buf).astype(BF16)) + bias_w
        s_wn = jnp.sum(q_f * kwn_ref[:, kv_new].astype(BF16).astype(F32), axis=1, keepdims=True)
        p_w, p_wn = _softmax_lanes_with_new(s_w, s_wn)
        o_w = p_wn * vwn_ref[:, kv_new] + _dot(p_w.astype(BF16), group_rows(wv_ref, g, w_buf).astype(BF16))
        for h in range(HPG):
            c0 = (g * HPG + h) * 3
            og = (gate[:, c0:c0 + 1] * oc[g][h:h + 1] + gate[:, c0 + 1:c0 + 2] * o_s[h:h + 1]
                  + gate[:, c0 + 2:c0 + 3] * o_w[h:h + 1])
            o_ref[:, (g * HPG + h) * HEAD_DIM:(g * HPG + h + 1) * HEAD_DIM] = og


SAMPLE_SEQS = 1


def _nsa_sample_kernel(pt_ref, *refs, n_pages, past, w_buf, layer, n_layers_out):
    blocked, rest = refs[:16], refs[16:]
    caches = rest[:N_CACHE]
    win_src = rest[N_CACHE:N_CACHE + 3 * n_layers_out]
    o_ref = rest[N_CACHE + 3 * n_layers_out]
    win_out = rest[N_CACHE + 3 * n_layers_out + 1:-3]
    pages_buf, sem, wsem = rest[-3:]
    b = pl.program_id(0)

    def page_copies(step, slot):
        return [pltpu.make_async_copy(caches[c].at[layer, pt_ref[(step * SAMPLE_SEQS + sq) * n_pages + p]],
                                      pages_buf.at[slot, sq, c, p], sem.at[slot])
                for sq in range(SAMPLE_SEQS) for c in range(N_CACHE) for p in range(n_pages)]

    @pl.when(b == 0)
    def _():
        for cp in page_copies(0, 0):
            cp.start()

    @pl.when(b + 1 < pl.num_programs(0))
    def _():
        for cp in page_copies(b + 1, jnp.bitwise_and(b + 1, 1)):
            cp.start()

    keep = N_KV * (w_buf - 1)
    win_copies = []
    for lo in range(n_layers_out):
        zs_hbm, wins = win_src[3 * lo], win_src[3 * lo + 1:3 * lo + 3]
        for sq in range(SAMPLE_SEQS):
            seq = b * SAMPLE_SEQS + sq
            for win_hbm, out_hbm, off in zip(wins, win_out, (OFF_KW, OFF_VW)):
                win_copies.append(pltpu.make_async_copy(win_hbm.at[lo, seq, pl.ds(N_KV, keep)],
                                                        out_hbm.at[lo, seq, pl.ds(0, keep)], wsem.at[0]))
                for g in range(N_KV):
                    win_copies.append(pltpu.make_async_copy(
                        zs_hbm.at[seq, pl.ds(0, 1), pl.ds(off + g * HEAD_DIM, HEAD_DIM)],
                        out_hbm.at[lo, seq, pl.ds(keep + g, 1)], wsem.at[0]))
    for cp in win_copies:
        cp.start()

    slot = jnp.bitwise_and(b, 1)
    for cp in page_copies(b, slot):
        cp.wait()
    per_seq = (0, 1, 2, 3, 4, 5, 6, 14, 15)
    stages = []
    for sq in range(SAMPLE_SEQS):
        ins = [r.at[sq] if k in per_seq else r for k, r in enumerate(blocked)]
        pages = [[pages_buf.at[slot, sq, c, p] for p in range(n_pages)] for c in range(N_CACHE)]
        stages.append(_nsa_sample_sequence(*ins, *pages, o_ref.at[sq], n_pages=n_pages, past=past, w_buf=w_buf))
    for _ in itertools.zip_longest(*stages):
        pass
    for cp in win_copies:
        cp.wait()


def _nsa_sample(zs3, page_table, cmp_pages, slc_pages, win_k, win_v, cw, wsel, layer, dbs, past, w_buf, older_zs3):
    n_pages = past // PAGE_SIZE
    depth = win_k.shape[0]
    n_older = 0 if older_zs3 is None else len(older_zs3)
    assert older_zs3 is None or n_older == layer == depth - 1
    assert dbs % SAMPLE_SEQS == 0
    nsq = SAMPLE_SEQS
    rowblk = lambda width, off: pl.BlockSpec((nsq, 1, width), lambda b, pt: (b, 0, off // width))
    const3 = lambda shp: pl.BlockSpec((None,) + shp, lambda b, pt: (layer,) + (0,) * len(shp))
    page_shape = (PAGE_SIZE * N_KV, HEAD_DIM)
    win_shape = (w_buf * N_KV, HEAD_DIM)
    win = pl.BlockSpec((None, nsq) + win_shape, lambda b, pt: (layer, b, 0, 0))
    hbm = pl.BlockSpec(memory_space=pl.ANY)
    in_specs = [
        rowblk(HPG * HEAD_DIM, OFF_NQ), rowblk(HPG * HEAD_DIM, OFF_NQ + HPG * HEAD_DIM),
        rowblk(KV_W, OFF_KS), rowblk(KV_W, OFF_VS), rowblk(KV_W, OFF_KW), rowblk(KV_W, OFF_VW),
        rowblk(LANES, OFF_GL),
        pl.BlockSpec((LANES, LANES), lambda b, pt: (0, 0)),
        const3((CHUNK_W, 2 * HEAD_DIM)), const3((HEAD_DIM, HEAD_DIM)), const3((PE_ROWS, CHUNK_W)),
        const3((CHUNK_W, 2 * HEAD_DIM)), const3((HEAD_DIM, HEAD_DIM)), const3((PE_ROWS, CHUNK_W)),
        win, win,
    ] + [hbm] * N_CACHE
    operands = [zs3] * 7 + [wsel, cw["w1k"], cw["w2k"], cw["pek"], cw["w1v"], cw["w2v"], cw["pev"], win_k, win_v]
    operands += list(cmp_pages) + list(slc_pages)
    out_specs = [pl.BlockSpec((nsq, 1, NSA_W), lambda b, pt: (b, 0, 0))]
    out_shape = [jax.ShapeDtypeStruct((dbs, 1, NSA_W), F32)]
    n_layers_out = 0
    if older_zs3 is not None:
        n_layers_out = depth
        for zs3_lo in list(older_zs3) + [zs3]:
            operands += [zs3_lo, win_k, win_v]
            in_specs += [hbm] * 3
        out_specs += [hbm] * 2
        out_shape += [jax.ShapeDtypeStruct((depth, dbs) + win_shape, F32)] * 2
    return pl.pallas_call(
        functools.partial(_nsa_sample_kernel, n_pages=n_pages, past=past, w_buf=w_buf, layer=layer,
                          n_layers_out=n_layers_out),
        grid_spec=pltpu.PrefetchScalarGridSpec(
            num_scalar_prefetch=1,
            grid=(dbs // nsq,),
            in_specs=in_specs,
            out_specs=out_specs,
            scratch_shapes=[pltpu.VMEM((2, nsq, N_CACHE, n_pages) + page_shape, F32),
                            pltpu.SemaphoreType.DMA((2,)), pltpu.SemaphoreType.DMA((1,))],
        ),
        out_shape=out_shape,
        compiler_params=_cparams(("arbitrary",), VMEM_LIMIT_BIG),
        name="nsa_sample",
    )(page_table.reshape(-1), *operands)


def _rope_tables(pos):
    half = HEAD_DIM // 2
    inv = ROPE_THETA ** (-jnp.arange(half, dtype=F32) / half)
    ang = pos.astype(F32)[:, None] * inv[None, :]
    cos, sin = jnp.cos(ang), jnp.sin(ang)
    return jnp.concatenate([cos, cos], axis=-1), jnp.concatenate([-sin, sin], axis=-1)


def _block_sum_matrix():
    w = np.zeros((LANES, LANES), np.float32)
    for j in range(LANES):
        for mm in range(SEL_RATIO):
            for nn in range(CMP_RATIO):
                c = SEL_RATIO * j + mm - nn
                if 0 <= c < LANES:
                    w[c, j] += 1.0
    return w


def _row_tile(m, cap):
    tm = min(m, cap)
    while m % tm:
        tm //= 2
    return tm


def kernel(x_prompt, x_sample, state_ret, state_pool, cache_cmp_k, cache_cmp_v, cache_slc_k, cache_slc_v,
           cache_win_k, cache_win_v, page_table, norm1, w_in, w_pool, pool_scale, cmp_w1_k, cmp_w2_k, cmp_pe_k,
           cmp_w1_v, cmp_w2_v, cmp_pe_v, w_out, norm2, w_up, w_down, norm_f):
    bsz, t, _ = x_prompt.shape
    dbs, dec_seq, _ = x_sample.shape
    depth = w_in.shape[0]
    assert dec_seq == 1 and t % 512 == 0 and dbs % SAMPLE_TB == 0
    n_pages = page_table.shape[1]
    past = n_pages * PAGE_SIZE
    n_pool = cache_cmp_k.shape[1]
    w_buf = cache_win_k.shape[2]
    wlen = min(WINDOW, t)

    w_in_bf = jnp.pad(w_in, ((0, 0), (0, 0), (0, N_PAD - D_IN))).astype(BF16)
    w_out_bf, w_up_bf, w_down_bf = w_out.astype(BF16), w_up.astype(BF16), w_down.astype(BF16)
    w1cat = lambda w1: jnp.concatenate([w1[:, :CMP_STRIDE], w1[:, CMP_STRIDE:]], axis=-1).astype(BF16).reshape(
        depth, CHUNK_W, 2 * HEAD_DIM)
    pe_rows = lambda pe: jnp.pad(pe.reshape(depth, CMP_RATIO, CHUNK_W),
                                 ((0, 0), (0, PE_ROWS - CMP_RATIO), (0, 0))).astype(BF16)
    cw = dict(w1k=w1cat(cmp_w1_k), w2k=cmp_w2_k.astype(BF16), pek=pe_rows(cmp_pe_k),
              w1v=w1cat(cmp_w1_v), w2v=cmp_w2_v.astype(BF16), pev=pe_rows(cmp_pe_v))
    norm1_3, norm2_3 = norm1[:, None, :], norm2[:, None, :]
    pool_scale_3 = pool_scale[:, None, :]
    norm_f2 = norm_f[None, :]
    log_g = jnp.log1p(-jnp.exp2(-5.0 - jnp.arange(R_HEADS, dtype=F32)))
    cos_p, sin_p = _rope_tables(jnp.arange(t))
    cos_s, sin_s = _rope_tables(jnp.full((dbs,), past, jnp.int32))
    wsel = jnp.asarray(_block_sum_matrix(), BF16)
    page_view = lambda c: c.reshape(depth, n_pool, PAGE_SIZE * N_KV, HEAD_DIM)
    cmp_pages = [page_view(c) for c in (cache_cmp_k, cache_cmp_v)]
    slc_pages = [page_view(c) for c in (cache_slc_k, cache_slc_v)]
    win_k = cache_win_k.reshape(depth, dbs, w_buf * N_KV, HEAD_DIM)
    win_v = cache_win_v.reshape(depth, dbs, w_buf * N_KV, HEAD_DIM)

    mp = bsz * t
    tm_p = _row_tile(t, 1024)
    tm_mlp = _row_tile(t, 1024)
    xp = x_prompt.reshape(mp, D_MODEL)
    xs = x_sample.reshape(dbs, D_MODEL)
    outs = {k: [] for k in ("ret_p", "ret_s", "pool_p", "pool_s", "ck_p", "ck_s", "cv_p", "cv_s", "sk_p", "sk_s",
                            "sv_p", "sv_s", "wk_p", "wv_p")}
    zs3_all = []
    for l in range(depth):
        last = l == depth - 1
        z = _inproj(xp, norm1_3, w_in_bf, l, cos_p, sin_p, tm_p)
        o_r, s_new = _ret_prompt(z, log_g, bsz, t)
        o_p = _pool_prompt(z, w_pool, pool_scale_3, l, bsz, t)
        kcs, vcst, vst, vwt = _prep_prompt(z, cw, l, bsz, t)
        o_n = _nsa_prompt(z, kcs, vcst, vst, vwt, bsz, t)
        xp = _outproj(o_r, o_p, o_n, w_out_bf, l, xp, _row_tile(t, 512))
        xp = _mlp(xp, norm2_3, w_up_bf, w_down_bf, l, norm_f2, last, tm_mlp)
        z3 = z.reshape(bsz, t, N_PAD)
        kv = lambda off: z3[:, :, off:off + KV_W].reshape(bsz, t, N_KV, HEAD_DIM)
        outs["ret_p"].append(s_new)
        outs["pool_p"].append(z3[:, t - POOL_BUF:, OFF_PU:OFF_PU + POOL_W])
        outs["ck_p"].append(kv(OFF_KC))
        outs["cv_p"].append(kv(OFF_VC))
        outs["sk_p"].append(kv(OFF_KS))
        outs["sv_p"].append(kv(OFF_VS))
        outs["wk_p"].append(kv(OFF_KW)[:, t - wlen:])
        outs["wv_p"].append(kv(OFF_VW)[:, t - wlen:])
        zs = _inproj(xs, norm1_3, w_in_bf, l, cos_s, sin_s, dbs)
        o_r, s_new, o_p, buf_new = _state_sample(zs, log_g, state_ret, state_pool, w_pool, pool_scale_3, l, dbs, past)
        zs3 = zs.reshape(dbs, 1, N_PAD)
        res = _nsa_sample(zs3, page_table, cmp_pages, slc_pages, win_k, win_v, cw, wsel, l, dbs, past, w_buf,
                          zs3_all if last else None)
        o_n = res[0]
        zs3_all.append(zs3)
        if last:
            wk_s, wv_s = res[1:]
        xs = _outproj(o_r, o_p, o_n.reshape(dbs, NSA_W), w_out_bf, l, xs, dbs)
        xs = _mlp(xs, norm2_3, w_up_bf, w_down_bf, l, norm_f2, last, dbs)
        kvs = lambda off: zs[:, off:off + KV_W].reshape(dbs, 1, N_KV, HEAD_DIM)
        outs["ret_s"].append(s_new)
        outs["pool_s"].append(buf_new)
        outs["ck_s"].append(kvs(OFF_KC))
        outs["cv_s"].append(kvs(OFF_VC))
        outs["sk_s"].append(kvs(OFF_KS))
        outs["sv_s"].append(kvs(OFF_VS))
    st = lambda k: jnp.stack(outs[k])
    win_out = lambda w: w.reshape(depth, dbs, w_buf, N_KV, HEAD_DIM)
    return (xp.reshape(bsz, t, D_MODEL), xs.reshape(dbs, 1, D_MODEL),
            st("ret_p"), st("ret_s"), st("pool_p"), st("pool_s"),
            st("ck_p"), st("ck_s"), st("cv_p"), st("cv_s"), st("sk_p"), st("sk_s"), st("sv_p"), st("sv_s"),
            st("wk_p"), win_out(wk_s), st("wv_p"), win_out(wv_s))
```

```python
import functools
import itertools

import numpy as np
import jax
import jax.numpy as jnp
from jax import lax
from jax.experimental import pallas as pl
from jax.experimental.pallas import tpu as pltpu

F32 = jnp.float32
BF16 = jnp.bfloat16

D_MODEL = 2048
HEAD_DIM = 128
PAGE_SIZE = 128
RET_W = D_MODEL // 4
POOL_W = D_MODEL // 4
NSA_W = D_MODEL - RET_W - POOL_W
R_HEADS = RET_W // HEAD_DIM
N_HEADS = NSA_W // HEAD_DIM
N_KV = max(1, N_HEADS // 4)
HPG = N_HEADS // N_KV
KV_W = N_KV * HEAD_DIM
POOL_WINDOWS = (2, 4, 8, 16)
POOL_GW = POOL_W // len(POOL_WINDOWS)
POOL_BUF = max(POOL_WINDOWS) - 1
RET_CHUNK = 128
CMP_LEN = 32
CMP_STRIDE = 16
CMP_RATIO = CMP_LEN // CMP_STRIDE
SEL_LEN = 64
SEL_RATIO = SEL_LEN // CMP_STRIDE
SEL_TOPK = 16
N_LOCAL_SEL = 2
WINDOW = 512
Q_BLOCK = 128
D_FF = 4 * D_MODEL
ROPE_THETA = 10000.0
EPS = 1e-6
D_IN = 4 * RET_W + POOL_W + NSA_W + 6 * KV_W + 3 * N_HEADS

OFF_RQ = 0
OFF_RK = OFF_RQ + RET_W
OFF_RV = OFF_RK + RET_W
OFF_RG = OFF_RV + RET_W
OFF_PU = OFF_RG + RET_W
OFF_NQ = OFF_PU + POOL_W
OFF_KC = OFF_NQ + NSA_W
OFF_VC = OFF_KC + KV_W
OFF_KS = OFF_VC + KV_W
OFF_VS = OFF_KS + KV_W
OFF_KW = OFF_VS + KV_W
OFF_VW = OFF_KW + KV_W
OFF_GL = OFF_VW + KV_W
PROJ_TN = 512
PROJ_ROW_CHUNK = 256
N_PAD = -(-D_IN // PROJ_TN) * PROJ_TN
ROPE_FULL_TILES = (OFF_RQ // PROJ_TN, OFF_RK // PROJ_TN, OFF_NQ // PROJ_TN, OFF_NQ // PROJ_TN + 1)
ROPE_HALF_TILES = (OFF_KC // PROJ_TN, OFF_KS // PROJ_TN, OFF_KW // PROJ_TN)
RK_TILE = OFF_RK // PROJ_TN

NEG = -1e30
V7X_VMEM_BYTES = 64 * 1024 * 1024
VMEM_LIMIT = V7X_VMEM_BYTES * 3 // 4
VMEM_LIMIT_BIG = V7X_VMEM_BYTES * 29 // 32
LANES = 128
SUBLANES = 8


def _cparams(sem, vmem_limit=VMEM_LIMIT):
    return pltpu.CompilerParams(dimension_semantics=sem, vmem_limit_bytes=vmem_limit)


def _dot(a, b):
    return jnp.dot(a, b, preferred_element_type=F32)


def _dot_nt(a, b):
    return lax.dot_general(a, b, (((1,), (1,)), ((), ())), preferred_element_type=F32)


def _dot_tn(a, b):
    return lax.dot_general(a, b, (((0,), (0,)), ((), ())), preferred_element_type=F32)


def _rms(x, g):
    return x * lax.rsqrt(jnp.mean(x * x, axis=-1, keepdims=True) + EPS) * g


def _silu(x):
    return x * jax.nn.sigmoid(x)


def _inproj_kernel(x_ref, g_ref, w_ref, cos_ref, sin_ref, z_ref, hn_ref):
    j = pl.program_id(1)

    @pl.when(j == 0)
    def _():
        hn_ref[...] = _rms(x_ref[...], g_ref[...]).astype(BF16)

    is_full = functools.reduce(jnp.logical_or, [j == t for t in ROPE_FULL_TILES])
    is_half = functools.reduce(jnp.logical_or, [j == t for t in ROPE_HALF_TILES])
    scale = jnp.where(j == RK_TILE, HEAD_DIM ** -0.5, 1.0).astype(F32)
    is_any = jnp.logical_or(is_full, is_half)
    tm = x_ref.shape[0]
    rc = min(tm, PROJ_ROW_CHUNK)
    for r0 in range(0, tm, rc):
        rows = slice(r0, r0 + rc)
        z = _dot(hn_ref[rows, :], w_ref[...])
        cos_t, sin_t = cos_ref[rows, :] * scale, sin_ref[rows, :] * scale
        tab_lead = (jnp.where(is_any, cos_t, 1.0), jnp.where(is_any, sin_t, 0.0))
        tab_rest = (jnp.where(is_full, cos_t, 1.0), jnp.where(is_full, sin_t, 0.0))
        for s in range(PROJ_TN // HEAD_DIM):
            sl = slice(s * HEAD_DIM, (s + 1) * HEAD_DIM)
            c, sn = tab_lead if s < KV_W // HEAD_DIM else tab_rest
            zs = z[:, sl]
            z_ref[rows, sl] = zs * c + pltpu.roll(zs, HEAD_DIM // 2, 1) * sn


def _inproj(x, gamma, w_in_bf, layer, cos_t, sin_t, tm):
    m = x.shape[0]
    n_row_tiles = cos_t.shape[0] // tm
    return pl.pallas_call(
        _inproj_kernel,
        grid=(m // tm, N_PAD // PROJ_TN),
        in_specs=[
            pl.BlockSpec((tm, D_MODEL), lambda i, j: (i, 0)),
            pl.BlockSpec((None, 1, D_MODEL), lambda i, j: (layer, 0, 0)),
            pl.BlockSpec((None, D_MODEL, PROJ_TN), lambda i, j: (layer, 0, j)),
            pl.BlockSpec((tm, HEAD_DIM), lambda i, j: (i % n_row_tiles, 0)),
            pl.BlockSpec((tm, HEAD_DIM), lambda i, j: (i % n_row_tiles, 0)),
        ],
        out_specs=pl.BlockSpec((tm, PROJ_TN), lambda i, j: (i, j)),
        out_shape=jax.ShapeDtypeStruct((m, N_PAD), F32),
        scratch_shapes=[pltpu.VMEM((tm, D_MODEL), BF16)],
        compiler_params=_cparams(("parallel", "arbitrary")),
        name="inproj",
    )(x, gamma, w_in_bf, cos_t, sin_t)


def _outproj_kernel(r_ref, p_ref, n_ref, w_ref, x_ref, o_ref):
    acc = _dot(r_ref[...].astype(BF16), w_ref[0:RET_W, :])
    acc += _dot(p_ref[...].astype(BF16), w_ref[RET_W:RET_W + POOL_W, :])
    acc += _dot(n_ref[...].astype(BF16), w_ref[RET_W + POOL_W:, :])
    o_ref[...] = x_ref[...] + acc


def _outproj(o_r, o_p, o_n, w_out_bf, layer, x, tm, tn=D_MODEL):
    m = x.shape[0]
    return pl.pallas_call(
        _outproj_kernel,
        grid=(m // tm, D_MODEL // tn),
        in_specs=[
            pl.BlockSpec((tm, RET_W), lambda i, j: (i, 0)),
            pl.BlockSpec((tm, POOL_W), lambda i, j: (i, 0)),
            pl.BlockSpec((tm, NSA_W), lambda i, j: (i, 0)),
            pl.BlockSpec((None, D_MODEL, tn), lambda i, j: (layer, 0, j)),
            pl.BlockSpec((tm, tn), lambda i, j: (i, j)),
        ],
        out_specs=pl.BlockSpec((tm, tn), lambda i, j: (i, j)),
        out_shape=jax.ShapeDtypeStruct((m, D_MODEL), F32),
        compiler_params=_cparams(("parallel", "parallel")),
        name="outproj",
    )(o_r, o_p, o_n, w_out_bf, x)


def _mlp_kernel(x_ref, g_ref, wu_ref, wd_ref, gf_ref, o_ref, hn_ref, *, final_norm):
    f = pl.program_id(1)

    @pl.when(f == 0)
    def _():
        hn_ref[...] = _rms(x_ref[...], g_ref[...]).astype(BF16)
        o_ref[...] = jnp.zeros_like(o_ref)

    u = jnp.maximum(_dot(hn_ref[...], wu_ref[...].astype(BF16)), 0.0)
    o_ref[...] += _dot((u * u).astype(BF16), wd_ref[...].astype(BF16))

    @pl.when(f == pl.num_programs(1) - 1)
    def _():
        y = x_ref[...] + o_ref[...]
        o_ref[...] = _rms(y, gf_ref[...]) if final_norm else y


def _mlp(x, gamma, w_up, w_down, layer, gamma_f, final_norm, tm, tf=512):
    m = x.shape[0]
    return pl.pallas_call(
        functools.partial(_mlp_kernel, final_norm=final_norm),
        grid=(m // tm, D_FF // tf),
        in_specs=[
            pl.BlockSpec((tm, D_MODEL), lambda i, f: (i, 0), pipeline_mode=pl.Buffered(1)),
            pl.BlockSpec((None, 1, D_MODEL), lambda i, f: (layer, 0, 0)),
            pl.BlockSpec((None, D_MODEL, tf), lambda i, f: (layer, 0, f)),
            pl.BlockSpec((None, tf, D_MODEL), lambda i, f: (layer, f, 0)),
            pl.BlockSpec((1, D_MODEL), lambda i, f: (0, 0)),
        ],
        out_specs=pl.BlockSpec((tm, D_MODEL), lambda i, f: (i, 0)),
        out_shape=jax.ShapeDtypeStruct((m, D_MODEL), F32),
        scratch_shapes=[pltpu.VMEM((tm, D_MODEL), BF16)],
        compiler_params=_cparams(("parallel", "arbitrary"), VMEM_LIMIT_BIG),
        name="mlp",
    )(x, gamma, w_up, w_down, gamma_f)


def _ret_prompt_kernel(lg_ref, q_ref, k_ref, v_ref, g_ref, o_ref, s_ref, *, n_chunk):
    ti = pl.program_id(1)
    L = RET_CHUNK
    ri = lax.broadcasted_iota(jnp.int32, (L, L), 0)
    ci = lax.broadcasted_iota(jnp.int32, (L, L), 1)
    diff = (ri - ci).astype(F32)
    causal = diff >= 0
    idx = lax.broadcasted_iota(jnp.int32, (L, 1), 0).astype(F32)
    consts = []
    for h in range(R_HEADS):
        lg = lg_ref[h]
        decay = jnp.where(causal, jnp.exp(jnp.where(causal, diff, 0.0) * lg), 0.0)
        q_dec = jnp.exp((idx + 1.0) * lg)
        k_dec = jnp.exp((L - 1.0 - idx) * lg)
        s_dec = jnp.exp(jnp.full((1, 1), float(L), F32) * lg)
        consts.append((decay, q_dec, k_dec, s_dec))

    @pl.when(ti == 0)
    def _():
        s_ref[...] = jnp.zeros_like(s_ref)

    def body(c, states):
        rows = pl.ds(pl.multiple_of(c * L, L), L)
        new_states = []
        for h, (decay, q_dec, k_dec, s_dec) in enumerate(consts):
            sl = slice(h * HEAD_DIM, (h + 1) * HEAD_DIM)
            q, k, v = q_ref[rows, sl], k_ref[rows, sl], v_ref[rows, sl]
            vb = v.astype(BF16)
            att = _dot_nt(q.astype(BF16), k.astype(BF16)) * decay
            o = _dot(att.astype(BF16), vb) + _dot((q * q_dec).astype(BF16), states[h].astype(BF16))
            new_states.append(s_dec * states[h] + _dot_tn((k * k_dec).astype(BF16), vb))
            o = o * lax.rsqrt(jnp.mean(o * o, axis=-1, keepdims=True) + EPS)
            o_ref[rows, sl] = o * _silu(g_ref[rows, sl])
        return tuple(new_states)

    states = lax.fori_loop(0, n_chunk, body, tuple(s_ref[h] for h in range(R_HEADS)))
    for h in range(R_HEADS):
        s_ref[h] = states[h]


def _ret_prompt(z, log_g, bsz, t, tt=1024):
    tt = min(tt, t)
    nt = t // tt
    blk = lambda off: pl.BlockSpec((tt, RET_W), lambda b, i, lg: (b * nt + i, off // RET_W))
    return pl.pallas_call(
        functools.partial(_ret_prompt_kernel, n_chunk=tt // RET_CHUNK),
        grid_spec=pltpu.PrefetchScalarGridSpec(
            num_scalar_prefetch=1,
            grid=(bsz, nt),
            in_specs=[blk(OFF_RQ), blk(OFF_RK), blk(OFF_RV), blk(OFF_RG)],
            out_specs=[
                pl.BlockSpec((tt, RET_W), lambda b, i, lg: (b * nt + i, 0)),
                pl.BlockSpec((None, R_HEADS, HEAD_DIM, HEAD_DIM), lambda b, i, lg: (b, 0, 0, 0)),
            ],
        ),
        out_shape=[
            jax.ShapeDtypeStruct((bsz * t, RET_W), F32),
            jax.ShapeDtypeStruct((bsz, R_HEADS, HEAD_DIM, HEAD_DIM), F32),
        ],
        compiler_params=_cparams(("parallel", "arbitrary")),
        name="ret_prompt",
    )(log_g, z, z, z, z)


POOL_HALO = 16


def _pool_prompt_kernel(u_ref, wp_ref, ps_ref, o_ref, ext_ref, *, tt):
    t = pl.program_id(1)

    @pl.when(t == 0)
    def _():
        ext_ref[0:POOL_HALO, :] = jnp.zeros((POOL_HALO, POOL_W), F32)

    u = u_ref[...]
    ext_ref[POOL_HALO:POOL_HALO + tt, :] = u
    pos = t * tt + lax.broadcasted_iota(jnp.int32, (tt, 1), 0)
    for g, w in enumerate(POOL_WINDOWS):
        sl = slice(g * POOL_GW, (g + 1) * POOL_GW)
        s = u[:, sl]
        for i in range(1, w):
            s = s + ext_ref[pl.ds(POOL_HALO - i, tt), sl]
        cnt = jnp.minimum(pos + 1, w).astype(F32)
        p = s / cnt - u[:, sl]
        o_ref[:, sl] = _dot(p.astype(BF16), wp_ref[g].astype(BF16)) * ps_ref[:, sl]
    ext_ref[0:POOL_HALO, :] = ext_ref[tt:tt + POOL_HALO, :]


def _pool_prompt(z, w_pool, pool_scale, layer, bsz, t, tt=512):
    nt = t // tt
    return pl.pallas_call(
        functools.partial(_pool_prompt_kernel, tt=tt),
        grid=(bsz, nt),
        in_specs=[
            pl.BlockSpec((tt, POOL_W), lambda b, i: (b * nt + i, OFF_PU // POOL_W)),
            pl.BlockSpec((None, len(POOL_WINDOWS), POOL_GW, POOL_GW), lambda b, i: (layer, 0, 0, 0)),
            pl.BlockSpec((None, 1, POOL_W), lambda b, i: (layer, 0, 0)),
        ],
        out_specs=pl.BlockSpec((tt, POOL_W), lambda b, i: (b * nt + i, 0)),
        out_shape=jax.ShapeDtypeStruct((bsz * t, POOL_W), F32),
        scratch_shapes=[pltpu.VMEM((POOL_HALO + tt, POOL_W), F32)],
        compiler_params=_cparams(("parallel", "arbitrary")),
        name="pool_prompt",
    )(z, w_pool, pool_scale)


CHUNK_W = CMP_STRIDE * HEAD_DIM
PE_ROWS = 16


def _compress_core(chunks, w1_ref, w2_ref, pe_ref):
    n_rows = chunks.shape[0]
    acc = _dot(jnp.concatenate([chunks, pe_ref[...]], axis=0), w1_ref[...])
    first = acc[:n_rows, :HEAD_DIM]
    second = acc[:n_rows, HEAD_DIM:]
    pe_term = acc[n_rows:n_rows + 1, :HEAD_DIM] + acc[n_rows + 1:n_rows + 2, HEAD_DIM:]
    h = first + pltpu.roll(second, n_rows - 1, 0) + pe_term
    return _dot(_silu(h).astype(BF16), w2_ref[...])


def _prep_prompt_kernel(kc_ref, vc_ref, vs_ref, vw_ref, w1k_ref, w2k_ref, pek_ref, w1v_ref, w2v_ref, pev_ref,
                        kcs_ref, vcst_ref, vst_ref, vwt_ref, *, n_chunk, n_tile):
    chunks_of = lambda r: jnp.concatenate(
        [r[pl.ds(s, n_chunk, stride=CMP_STRIDE), :].astype(BF16) for s in range(CMP_STRIDE)], axis=1)
    kcs_ref[...] = _compress_core(chunks_of(kc_ref), w1k_ref, w2k_ref, pek_ref).astype(BF16)
    vcst_ref[...] = _compress_core(chunks_of(vc_ref), w1v_ref, w2v_ref, pev_ref).T.astype(BF16)

    def tbody(i, carry):
        rows = pl.ds(pl.multiple_of(i * HEAD_DIM, HEAD_DIM), HEAD_DIM)
        vst_ref[i] = vs_ref[rows, :].T.astype(BF16)
        vwt_ref[i] = vw_ref[rows, :].T.astype(BF16)
        return carry

    lax.fori_loop(0, n_tile, tbody, 0)


def _prep_prompt(z, cw, layer, bsz, t):
    n_chunk = t // CMP_STRIDE
    n_tile = t // HEAD_DIM
    col = lambda off: pl.BlockSpec((t, HEAD_DIM), lambda b, g: (b, off // HEAD_DIM + g))
    w1 = pl.BlockSpec((None, CHUNK_W, 2 * HEAD_DIM), lambda b, g: (layer, 0, 0))
    w2 = pl.BlockSpec((None, HEAD_DIM, HEAD_DIM), lambda b, g: (layer, 0, 0))
    pe = pl.BlockSpec((None, PE_ROWS, CHUNK_W), lambda b, g: (layer, 0, 0))
    return pl.pallas_call(
        functools.partial(_prep_prompt_kernel, n_chunk=n_chunk, n_tile=n_tile),
        grid=(bsz, N_KV),
        in_specs=[col(OFF_KC), col(OFF_VC), col(OFF_VS), col(OFF_VW), w1, w2, pe, w1, w2, pe],
        out_specs=[
            pl.BlockSpec((None, None, n_chunk, HEAD_DIM), lambda b, g: (b, g, 0, 0)),
            pl.BlockSpec((None, None, HEAD_DIM, n_chunk), lambda b, g: (b, g, 0, 0)),
            pl.BlockSpec((None, None, n_tile, HEAD_DIM, HEAD_DIM), lambda b, g: (b, g, 0, 0, 0)),
            pl.BlockSpec((None, None, n_tile, HEAD_DIM, HEAD_DIM), lambda b, g: (b, g, 0, 0, 0)),
        ],
        out_shape=[
            jax.ShapeDtypeStruct((bsz, N_KV, n_chunk, HEAD_DIM), BF16),
            jax.ShapeDtypeStruct((bsz, N_KV, HEAD_DIM, n_chunk), BF16),
            jax.ShapeDtypeStruct((bsz, N_KV, n_tile, HEAD_DIM, HEAD_DIM), BF16),
            jax.ShapeDtypeStruct((bsz, N_KV, n_tile, HEAD_DIM, HEAD_DIM), BF16),
        ],
        compiler_params=_cparams(("parallel", "parallel")),
        name="prep_prompt",
    )(z, z, z, z, cw["w1k"], cw["w2k"], cw["pek"], cw["w1v"], cw["w2v"], cw["pev"])


LOG2E = 1.4426950408889634


def _softmax_tiles(tiles, carry):
    m, l, acc = carry
    m_new = m
    for s, _ in tiles:
        m_new = jnp.maximum(m_new, jnp.max(s, axis=0, keepdims=True))
    alpha = jnp.exp2(m - m_new)
    l = alpha * l
    acc = alpha * acc
    for s, vt in tiles:
        p = jnp.exp2(s - m_new)
        l = l + jnp.sum(p, axis=0, keepdims=True)
        acc = acc + _dot(vt, p.astype(BF16))
    return m_new, l, acc


def _select_blocks(score, score_ref):
    n_blocks = score.shape[0]
    score_ref[...] = score
    groups = [score[v * SUBLANES:(v + 1) * SUBLANES] for v in range(n_blocks // SUBLANES)]
    jb = lax.broadcasted_iota(jnp.int32, groups[0].shape, 0)
    ranks = [jnp.zeros(gv.shape, F32) for gv in groups]
    for k in range(n_blocks):
        rk = score_ref[pl.ds(k, 1), :]
        for v, gv in enumerate(groups):
            if v * SUBLANES > k:
                beats = rk >= gv
            elif (v + 1) * SUBLANES - 1 <= k:
                beats = rk > gv
            else:
                beats = jnp.logical_or(rk > gv, jnp.logical_and(jb + v * SUBLANES > k, rk == gv))
            ranks[v] = ranks[v] + jnp.where(beats, 1.0, 0.0)
    rank = jnp.concatenate(ranks, axis=0)
    return jnp.logical_and(rank < SEL_TOPK, score > -jnp.inf)


def _nsa_prompt_kernel(q_ref, gl_ref, kcs_ref, vcst_ref, ks_ref, vst_ref, kw_ref, vwt_ref, o_ref,
                       imp_ref, selb_ref, gt_ref, score_ref, *, n_chunk, n_sel):
    g = pl.program_id(1)
    i = pl.program_id(2)
    qw = HPG * Q_BLOCK
    qscale = HEAD_DIM ** -0.5 * LOG2E
    q = q_ref[...]
    qt = jnp.concatenate([(q[:, h * HEAD_DIM:(h + 1) * HEAD_DIM] * qscale).T for h in range(HPG)],
                         axis=1).astype(BF16)
    lane = lax.broadcasted_iota(jnp.int32, (1, qw), 1)
    tl4 = jnp.bitwise_and(lane, Q_BLOCK - 1)
    tpos4 = i * Q_BLOCK + tl4
    tpos = i * Q_BLOCK + lax.broadcasted_iota(jnp.int32, (1, Q_BLOCK), 1)

    sc = _dot(kcs_ref[...], qt)
    c_end = lax.broadcasted_iota(jnp.int32, (n_chunk, 1), 0) * CMP_STRIDE + (CMP_LEN - 1)
    mask_c = jnp.logical_and(c_end <= tpos4, c_end < n_chunk * CMP_STRIDE)
    m = jnp.max(jnp.where(mask_c, sc, NEG), axis=0, keepdims=True)
    e = jnp.where(mask_c, jnp.exp2(sc - m), 0.0)
    d = jnp.sum(e, axis=0, keepdims=True)
    pc = e * (1.0 / jnp.where(d > 0, d, 1.0))
    oc = _dot(vcst_ref[...], pc.astype(BF16))
    imp = pc[:, 0:Q_BLOCK]
    for h in range(1, HPG):
        imp = imp + pc[:, h * Q_BLOCK:(h + 1) * Q_BLOCK]

    imp_ref[0:SUBLANES, :] = jnp.zeros((SUBLANES, Q_BLOCK), F32)
    imp_ref[SUBLANES:SUBLANES + n_chunk, :] = imp
    imp_ref[SUBLANES + n_chunk:, :] = jnp.zeros((SUBLANES, Q_BLOCK), F32)
    p_slc = None
    for mm in range(SEL_RATIO):
        for nn in range(CMP_RATIO):
            r = imp_ref[pl.ds(SUBLANES + mm - nn, n_sel, stride=SEL_RATIO), :]
            p_slc = r if p_slc is None else p_slc + r
    jb = lax.broadcasted_iota(jnp.int32, (n_sel, 1), 0)
    blk_valid = jb * SEL_LEN <= tpos
    cur = lax.shift_right_logical(tpos, int(np.log2(SEL_LEN)))
    forced = jnp.logical_or(jb == 0, jnp.logical_and(jb <= cur, jb > cur - N_LOCAL_SEL))
    score = jnp.where(blk_valid, jnp.where(forced, jnp.inf, p_slc), -jnp.inf)
    sel = _select_blocks(score, score_ref)
    selb_ref[...] = jnp.concatenate([jnp.where(sel, 0.0, NEG)] * HPG, axis=1)

    row = lax.broadcasted_iota(jnp.int32, (Q_BLOCK, 1), 0)
    causal_bias = jnp.where(row <= tl4, 0.0, NEG)
    init = (jnp.full((1, qw), NEG, F32), jnp.zeros((1, qw), F32), jnp.zeros((HEAD_DIM, qw), F32))
    blocks_per_tile = Q_BLOCK // SEL_LEN

    n_back = WINDOW // Q_BLOCK
    win_tiles = []
    for w in range(n_back + 1):
        kb = i - n_back + w
        kbc = jnp.maximum(kb, 0)
        rows = pl.ds(pl.multiple_of(kbc * Q_BLOCK, Q_BLOCK), Q_BLOCK)
        s = _dot(kw_ref[rows, :].astype(BF16), qt)
        if w == n_back:
            s = s + causal_bias
        else:
            before_start = jnp.where(kb >= 0, 0.0, NEG).astype(F32)
            s = s + (jnp.where(row > tl4, before_start, NEG) if w == 0 else before_start)
        win_tiles.append((s, vwt_ref[kbc]))
    _, l_w, a_w = _softmax_tiles(win_tiles, init)

    n_pair = lax.shift_right_logical(i, 1)
    odd_bias = jnp.where(jnp.bitwise_and(i, 1) == 1, 0.0, NEG).astype(F32)

    def sel_scores(kt, tile_bias):
        rows = pl.ds(pl.multiple_of(kt * Q_BLOCK, Q_BLOCK), Q_BLOCK)
        s = _dot(ks_ref[rows, :].astype(BF16), qt)
        parts = []
        for bb in range(blocks_per_tile):
            bias = selb_ref[pl.ds(kt * blocks_per_tile + bb, 1), :] + tile_bias
            parts.append(s[bb * SEL_LEN:(bb + 1) * SEL_LEN] + bias)
        return jnp.concatenate(parts, axis=0)

    def update_scores(u):
        is_last = u == n_pair
        s_a = sel_scores(jnp.where(is_last, jnp.maximum(i - 1, 0), 2 * u), jnp.where(is_last, odd_bias, 0.0))
        s_b = sel_scores(jnp.where(is_last, i, 2 * u + 1), 0.0) + jnp.where(is_last, causal_bias, 0.0)
        return s_a, s_b

    def sel_body(u, carry):
        m, l, acc, s_a, s_b = carry
        nxt = update_scores(u + 1)
        m, l, acc = _softmax_tiles([(s_a, vst_ref[2 * u]), (s_b, vst_ref[2 * u + 1])], (m, l, acc))
        return (m, l, acc) + nxt

    m_s, l_s, a_s, s_a, s_b = lax.fori_loop(0, n_pair, sel_body, init + update_scores(0))
    _, l_s, a_s = _softmax_tiles([(s_a, vst_ref[jnp.maximum(i - 1, 0)]), (s_b, vst_ref[i])], (m_s, l_s, a_s))

    gt_ref[...] = jax.nn.sigmoid(gl_ref[...]).T
    gate = lambda c: jnp.concatenate(
        [gt_ref[pl.ds((g * HPG + h) * 3 + c, 1), :] for h in range(HPG)], axis=1)
    ot = gate(0) * oc + gate(1) * (a_s * (1.0 / l_s)) + gate(2) * (a_w * (1.0 / l_w))
    for h in range(HPG):
        o_ref[:, h * HEAD_DIM:(h + 1) * HEAD_DIM] = ot[:, h * Q_BLOCK:(h + 1) * Q_BLOCK].T


def _nsa_prompt(z, kcs, vcst, vst, vwt, bsz, t):
    n_chunk = t // CMP_STRIDE
    n_sel = t // SEL_LEN
    n_tile = t // Q_BLOCK
    nqb = t // Q_BLOCK
    qw = HPG * HEAD_DIM
    return pl.pallas_call(
        functools.partial(_nsa_prompt_kernel, n_chunk=n_chunk, n_sel=n_sel),
        grid=(bsz, N_KV, nqb),
        in_specs=[
            pl.BlockSpec((Q_BLOCK, qw), lambda b, g, i: (b * nqb + i, OFF_NQ // qw + g)),
            pl.BlockSpec((Q_BLOCK, LANES), lambda b, g, i: (b * nqb + i, OFF_GL // LANES)),
            pl.BlockSpec((None, None, n_chunk, HEAD_DIM), lambda b, g, i: (b, g, 0, 0)),
            pl.BlockSpec((None, None, HEAD_DIM, n_chunk), lambda b, g, i: (b, g, 0, 0)),
            pl.BlockSpec((t, HEAD_DIM), lambda b, g, i: (b, OFF_KS // HEAD_DIM + g)),
            pl.BlockSpec((None, None, n_tile, HEAD_DIM, HEAD_DIM), lambda b, g, i: (b, g, 0, 0, 0)),
            pl.BlockSpec((t, HEAD_DIM), lambda b, g, i: (b, OFF_KW // HEAD_DIM + g)),
            pl.BlockSpec((None, None, n_tile, HEAD_DIM, HEAD_DIM), lambda b, g, i: (b, g, 0, 0, 0)),
        ],
        out_specs=pl.BlockSpec((Q_BLOCK, qw), lambda b, g, i: (b * nqb + i, g)),
        out_shape=jax.ShapeDtypeStruct((bsz * t, NSA_W), F32),
        scratch_shapes=[
            pltpu.VMEM((n_chunk + 2 * SUBLANES, Q_BLOCK), F32),
            pltpu.VMEM((n_sel, HPG * Q_BLOCK), F32),
            pltpu.VMEM((LANES, Q_BLOCK), F32),
            pltpu.VMEM((n_sel, Q_BLOCK), F32),
        ],
        compiler_params=_cparams(("parallel", "parallel", "arbitrary")),
        name="nsa_prompt",
    )(z, z, kcs, vcst, z, vst, z, vwt)


SAMPLE_TB = 8


def _state_sample_kernel(lg_ref, q_ref, k_ref, v_ref, g_ref, u_ref, sr_ref, sp_ref, wp_ref, ps_ref,
                         or_ref, sn_ref, op_ref, pn_ref, *, pos):
    tb = SAMPLE_TB
    pad = jnp.zeros((SUBLANES - 1, HEAD_DIM), F32)
    for h in range(R_HEADS):
        gamma = jnp.exp(jnp.full((1, HEAD_DIM), lg_ref[h], F32))
        sl = slice(h * HEAD_DIM, (h + 1) * HEAD_DIM)
        rows = []
        for b in range(tb):
            q = q_ref[b:b + 1, sl]
            k = k_ref[b:b + 1, sl]
            v = v_ref[b:b + 1, sl]
            state = sr_ref[b, h]
            k8 = jnp.concatenate([k, pad], axis=0).astype(BF16)
            v8 = jnp.concatenate([v, pad], axis=0).astype(BF16)
            qk = jnp.sum(q * k, axis=1, keepdims=True)
            qs = _dot(jnp.concatenate([q * gamma, pad], axis=0).astype(BF16), state.astype(BF16))[0:1]
            rows.append(qk * v + qs)
            sn_ref[b, h] = gamma[:, 0:1] * state + _dot_tn(k8, v8)
        o = jnp.concatenate(rows, axis=0)
        o = o * lax.rsqrt(jnp.mean(o * o, axis=-1, keepdims=True) + EPS)
        or_ref[:, sl] = o * _silu(g_ref[:, sl])
    u = u_ref[...]
    for gi, w in enumerate(POOL_WINDOWS):
        sl = slice(gi * POOL_GW, (gi + 1) * POOL_GW)
        s = u[:, sl] + jnp.sum(sp_ref[:, POOL_BUF - (w - 1):POOL_BUF, sl], axis=1)
        p = s / float(min(pos + 1, w)) - u[:, sl]
        op_ref[:, sl] = _dot(p.astype(BF16), wp_ref[gi].astype(BF16)) * ps_ref[:, sl]
    pn_ref[:, 0:POOL_BUF - 1, :] = sp_ref[:, 1:POOL_BUF, :]
    for b in range(tb):
        pn_ref[b, POOL_BUF - 1:POOL_BUF, :] = u[b:b + 1, :]


def _state_sample(zs, log_g, state_ret, state_pool, w_pool, pool_scale, layer, dbs, pos):
    tb = SAMPLE_TB
    col = lambda off: pl.BlockSpec((tb, RET_W), lambda i, lg: (i, off // RET_W))
    return pl.pallas_call(
        functools.partial(_state_sample_kernel, pos=pos),
        grid_spec=pltpu.PrefetchScalarGridSpec(
            num_scalar_prefetch=1,
            grid=(dbs // tb,),
            in_specs=[
                col(OFF_RQ), col(OFF_RK), col(OFF_RV), col(OFF_RG), col(OFF_PU),
                pl.BlockSpec((None, tb, R_HEADS, HEAD_DIM, HEAD_DIM), lambda i, lg: (layer, i, 0, 0, 0)),
                pl.BlockSpec((None, tb, POOL_BUF, POOL_W), lambda i, lg: (layer, i, 0, 0)),
                pl.BlockSpec((None, len(POOL_WINDOWS), POOL_GW, POOL_GW), lambda i, lg: (layer, 0, 0, 0)),
                pl.BlockSpec((None, 1, POOL_W), lambda i, lg: (layer, 0, 0)),
            ],
            out_specs=[
                pl.BlockSpec((tb, RET_W), lambda i, lg: (i, 0)),
                pl.BlockSpec((tb, R_HEADS, HEAD_DIM, HEAD_DIM), lambda i, lg: (i, 0, 0, 0)),
                pl.BlockSpec((tb, POOL_W), lambda i, lg: (i, 0)),
                pl.BlockSpec((tb, POOL_BUF, POOL_W), lambda i, lg: (i, 0, 0)),
            ],
        ),
        out_shape=[
            jax.ShapeDtypeStruct((dbs, RET_W), F32),
            jax.ShapeDtypeStruct((dbs, R_HEADS, HEAD_DIM, HEAD_DIM), F32),
            jax.ShapeDtypeStruct((dbs, POOL_W), F32),
            jax.ShapeDtypeStruct((dbs, POOL_BUF, POOL_W), F32),
        ],
        compiler_params=_cparams(("parallel",)),
        name="state_sample",
    )(log_g, zs, zs, zs, zs, zs, state_ret, state_pool, w_pool, pool_scale)


N_CACHE = 4


def _softmax_lanes_with_new(s, s_new):
    m = jnp.maximum(jnp.max(s, axis=1, keepdims=True), s_new)
    p = jnp.exp(s - m)
    p_new = jnp.exp(s_new - m)
    inv = 1.0 / (jnp.sum(p, axis=1, keepdims=True) + p_new)
    return p * inv, p_new * inv


def _rank_select(srow):
    across = jnp.broadcast_to(srow, (LANES, LANES))
    down = across.T
    r_i = lax.broadcasted_iota(jnp.int32, (LANES, LANES), 0)
    c_i = lax.broadcasted_iota(jnp.int32, (LANES, LANES), 1)
    beats = jnp.logical_or(down > across, jnp.logical_and(down == across, r_i < c_i))
    rank = jnp.sum(jnp.where(beats, 1.0, 0.0), axis=0, keepdims=True)
    return jnp.logical_and(rank < SEL_TOPK, srow > -jnp.inf)


def _nsa_sample_sequence(q0_ref, q1_ref, ksn_ref, vsn_ref, kwn_ref, vwn_ref, gl_ref, wsel_ref,
                         w1k_ref, w2k_ref, pek_ref, w1v_ref, w2v_ref, pev_ref, wk_ref, wv_ref,
                         ck, cv, sk, sv, o_ref, *, n_pages, past, w_buf):
    scale = HEAD_DIM ** -0.5
    chunks_per_page = PAGE_SIZE // CMP_STRIDE
    n_chunk = n_pages * chunks_per_page
    n_sel = -(-(past + 1) // SEL_LEN)
    assert n_chunk <= LANES and n_sel <= LANES and past % SEL_LEN == 0

    group_rows = lambda ref, g, n: ref[pl.ds(g, n, stride=N_KV), :]

    def chunks_of(prefs):
        return jnp.concatenate(
            [group_rows(p, g, PAGE_SIZE).astype(BF16).reshape(chunks_per_page, CHUNK_W)
             for g in range(N_KV) for p in prefs],
            axis=0)

    ksum = _compress_core(chunks_of(ck), w1k_ref, w2k_ref, pek_ref)
    yield
    vsum = _compress_core(chunks_of(cv), w1v_ref, w2v_ref, pev_ref)
    yield

    lane = lax.broadcasted_iota(jnp.int32, (1, LANES), 1)
    q_refs = (q0_ref, q1_ref)
    zpad = jnp.zeros((SUBLANES - HPG, HEAD_DIM), F32)
    q8, oc, imps = [], [], []
    for g in range(N_KV):
        qg = jnp.concatenate([q_refs[g][:, h * HEAD_DIM:(h + 1) * HEAD_DIM] for h in range(HPG)] + [zpad],
                             axis=0) * scale
        q8.append(qg.astype(BF16))
        kcs = ksum[g * n_chunk:(g + 1) * n_chunk]
        vcs = vsum[g * n_chunk:(g + 1) * n_chunk]
        s_c = _dot_nt(q8[g], kcs.astype(BF16))
        c_end = lane[:, :n_chunk] * CMP_STRIDE + (CMP_LEN - 1)
        mask_c = jnp.logical_and(c_end <= past, c_end < n_chunk * CMP_STRIDE)
        m = jnp.max(jnp.where(mask_c, s_c, NEG), axis=1, keepdims=True)
        e = jnp.where(mask_c, jnp.exp(s_c - m), 0.0)
        d = jnp.sum(e, axis=1, keepdims=True)
        pc = e * (1.0 / jnp.where(d > 0, d, 1.0))
        oc.append(_dot(pc.astype(BF16), vcs.astype(BF16)))
        imps.append(jnp.sum(pc[0:HPG], axis=0, keepdims=True))
    yield

    imp = jnp.concatenate(imps + [jnp.zeros((SUBLANES - N_KV, n_chunk), F32)], axis=0)
    if n_chunk < LANES:
        imp = jnp.concatenate([imp, jnp.zeros((SUBLANES, LANES - n_chunk), F32)], axis=1)
    hi = imp.astype(BF16)
    r1 = imp - hi.astype(F32)
    mid = r1.astype(BF16)
    lo = (r1 - mid.astype(F32)).astype(BF16)
    wsel = wsel_ref[...]
    p_slc = _dot(hi, wsel) + _dot(mid, wsel) + _dot(lo, wsel)
    cur = past // SEL_LEN
    blk_valid = jnp.logical_and(lane * SEL_LEN <= past, lane < n_sel)
    forced = jnp.logical_or(lane == 0, jnp.logical_and(lane <= cur, lane > cur - N_LOCAL_SEL))
    score = jnp.where(blk_valid, jnp.where(forced, jnp.inf, p_slc), -jnp.inf)
    yield

    gate = jax.nn.sigmoid(gl_ref[...])
    blocks_per_page = PAGE_SIZE // SEL_LEN
    wl = lax.broadcasted_iota(jnp.int32, (1, w_buf), 1)
    dpos = w_buf - wl
    bias_w = jnp.where(jnp.logical_and(jnp.logical_and(dpos >= 0, dpos < WINDOW), past - dpos >= 0), 0.0, NEG)
    for g in range(N_KV):
        kv_new = slice(g * HEAD_DIM, (g + 1) * HEAD_DIM)
        q_f = q8[g].astype(F32)
        selb = jnp.where(_rank_select(score[g:g + 1]), 0.0, NEG)
        bias_pages = []
        for p in range(n_pages):
            bias = selb[:, p * blocks_per_page:p * blocks_per_page + 1]
            for bb in range(1, blocks_per_page):
                bias = jnp.where(lane < bb * SEL_LEN, bias, selb[:, p * blocks_per_page + bb:p * blocks_per_page + bb + 1])
            bias_pages.append(bias)
        k_all = jnp.concatenate([group_rows(p, g, PAGE_SIZE) for p in sk], axis=0).astype(BF16)
        v_all = jnp.concatenate([group_rows(p, g, PAGE_SIZE) for p in sv], axis=0).astype(BF16)
        s_all = _dot_nt(q8[g], k_all) + jnp.concatenate(bias_pages, axis=1)
        s_new = jnp.sum(q_f * ksn_ref[:, kv_new].astype(BF16).astype(F32), axis=1, keepdims=True)
        p_all, p_new = _softmax_lanes_with_new(s_all, s_new)
        o_s = p_new * vsn_ref[:, kv_new] + _dot(p_all.astype(BF16), v_all)
        yield
        s_w = _dot_nt(q8[g], group_rows(wk_ref, g, w_buf).astype(BF16)) + bias_w
        s_wn = jnp.sum(q_f * kwn_ref[:, kv_new].astype(BF16).astype(F32), axis=1, keepdims=True)
        p_w, p_wn = _softmax_lanes_with_new(s_w, s_wn)
        o_w = p_wn * vwn_ref[:, kv_new] + _dot(p_w.astype(BF16), group_rows(wv_ref, g, w_buf).astype(BF16))
        for h in range(HPG):
            c0 = (g * HPG + h) * 3
            og = (gate[:, c0:c0 + 1] * oc[g][h:h + 1] + gate[:, c0 + 1:c0 + 2] * o_s[h:h + 1]
                  + gate[:, c0 + 2:c0 + 3] * o_w[h:h + 1])
            o_ref[:, (g * HPG + h) * HEAD_DIM:(g * HPG + h + 1) * HEAD_DIM] = og


SAMPLE_SEQS = 1


def _nsa_sample_kernel(pt_ref, *refs, n_pages, past, w_buf, layer, n_older):
    blocked, rest = refs[:16], refs[16:]
    caches = rest[:N_CACHE]
    older = rest[N_CACHE:N_CACHE + 4 * n_older]
    outs = rest[N_CACHE + 4 * n_older:-2]
    o_ref = outs[0]
    pages_buf, sem = rest[-2:]
    b = pl.program_id(0)

    def page_copies(step, slot):
        return [pltpu.make_async_copy(caches[c].at[layer, pt_ref[(step * SAMPLE_SEQS + sq) * n_pages + p]],
                                      pages_buf.at[slot, sq, c, p], sem.at[slot])
                for sq in range(SAMPLE_SEQS) for c in range(N_CACHE) for p in range(n_pages)]

    @pl.when(b == 0)
    def _():
        for cp in page_copies(0, 0):
            cp.start()

    @pl.when(b + 1 < pl.num_programs(0))
    def _():
        for cp in page_copies(b + 1, jnp.bitwise_and(b + 1, 1)):
            cp.start()

    slot = jnp.bitwise_and(b, 1)
    for cp in page_copies(b, slot):
        cp.wait()
    per_seq = (0, 1, 2, 3, 4, 5, 6, 14, 15)
    stages = []
    for sq in range(SAMPLE_SEQS):
        ins = [r.at[sq] if k in per_seq else r for k, r in enumerate(blocked)]
        pages = [[pages_buf.at[slot, sq, c, p] for p in range(n_pages)] for c in range(N_CACHE)]
        stages.append(_nsa_sample_sequence(*ins, *pages, o_ref.at[sq], n_pages=n_pages, past=past, w_buf=w_buf))
    for _ in itertools.zip_longest(*stages):
        pass

    if len(outs) > 1:
        wko_ref, wvo_ref = outs[1:]
        shifts = [(layer, blocked[4], blocked[5], blocked[14], blocked[15])]
        shifts += [(lo,) + tuple(older[4 * lo:4 * lo + 4]) for lo in range(n_older)]
        keep = N_KV * (w_buf - 1)
        for lo, kn, vn, wk, wv in shifts:
            for sq in range(SAMPLE_SEQS):
                wko_ref[lo, sq, 0:keep, :] = wk[sq, N_KV:N_KV * w_buf, :]
                wvo_ref[lo, sq, 0:keep, :] = wv[sq, N_KV:N_KV * w_buf, :]
                for g in range(N_KV):
                    wko_ref[lo, sq, keep + g:keep + g + 1, :] = kn[sq, :, g * HEAD_DIM:(g + 1) * HEAD_DIM]
                    wvo_ref[lo, sq, keep + g:keep + g + 1, :] = vn[sq, :, g * HEAD_DIM:(g + 1) * HEAD_DIM]


def _nsa_sample(zs3, page_table, cmp_pages, slc_pages, win_k, win_v, cw, wsel, layer, dbs, past, w_buf, older_zs3):
    n_pages = past // PAGE_SIZE
    depth = win_k.shape[0]
    n_older = 0 if older_zs3 is None else len(older_zs3)
    assert older_zs3 is None or n_older == layer == depth - 1
    assert dbs % SAMPLE_SEQS == 0
    nsq = SAMPLE_SEQS
    rowblk = lambda width, off: pl.BlockSpec((nsq, 1, width), lambda b, pt: (b, 0, off // width))
    const3 = lambda shp: pl.BlockSpec((None,) + shp, lambda b, pt: (layer,) + (0,) * len(shp))
    page_shape = (PAGE_SIZE * N_KV, HEAD_DIM)
    win_shape = (w_buf * N_KV, HEAD_DIM)
    win = pl.BlockSpec((None, nsq) + win_shape, lambda b, pt: (layer, b, 0, 0))
    hbm = pl.BlockSpec(memory_space=pl.ANY)
    in_specs = [
        rowblk(HPG * HEAD_DIM, OFF_NQ), rowblk(HPG * HEAD_DIM, OFF_NQ + HPG * HEAD_DIM),
        rowblk(KV_W, OFF_KS), rowblk(KV_W, OFF_VS), rowblk(KV_W, OFF_KW), rowblk(KV_W, OFF_VW),
        rowblk(LANES, OFF_GL),
        pl.BlockSpec((LANES, LANES), lambda b, pt: (0, 0)),
        const3((CHUNK_W, 2 * HEAD_DIM)), const3((HEAD_DIM, HEAD_DIM)), const3((PE_ROWS, CHUNK_W)),
        const3((CHUNK_W, 2 * HEAD_DIM)), const3((HEAD_DIM, HEAD_DIM)), const3((PE_ROWS, CHUNK_W)),
        win, win,
    ] + [hbm] * N_CACHE
    operands = [zs3] * 7 + [wsel, cw["w1k"], cw["w2k"], cw["pek"], cw["w1v"], cw["w2v"], cw["pev"], win_k, win_v]
    operands += list(cmp_pages) + list(slc_pages)
    out_specs = [pl.BlockSpec((nsq, 1, NSA_W), lambda b, pt: (b, 0, 0))]
    out_shape = [jax.ShapeDtypeStruct((dbs, 1, NSA_W), F32)]
    if older_zs3 is not None:
        for lo in range(n_older):
            win_lo = pl.BlockSpec((None, nsq) + win_shape, lambda b, pt, lo=lo: (lo, b, 0, 0))
            operands += [older_zs3[lo], older_zs3[lo], win_k, win_v]
            in_specs += [rowblk(KV_W, OFF_KW), rowblk(KV_W, OFF_VW), win_lo, win_lo]
        out_specs += [pl.BlockSpec((depth, nsq) + win_shape, lambda b, pt: (0, b, 0, 0))] * 2
        out_shape += [jax.ShapeDtypeStruct((depth, dbs) + win_shape, F32)] * 2
    return pl.pallas_call(
        functools.partial(_nsa_sample_kernel, n_pages=n_pages, past=past, w_buf=w_buf, layer=layer,
                          n_older=n_older),
        grid_spec=pltpu.PrefetchScalarGridSpec(
            num_scalar_prefetch=1,
            grid=(dbs // nsq,),
            in_specs=in_specs,
            out_specs=out_specs,
            scratch_shapes=[pltpu.VMEM((2, nsq, N_CACHE, n_pages) + page_shape, F32),
                            pltpu.SemaphoreType.DMA((2,))],
        ),
        out_shape=out_shape,
        compiler_params=_cparams(("arbitrary",), VMEM_LIMIT_BIG),
        name="nsa_sample",
    )(page_table.reshape(-1), *operands)


def _rope_tables(pos):
    half = HEAD_DIM // 2
    inv = ROPE_THETA ** (-jnp.arange(half, dtype=F32) / half)
    ang = pos.astype(F32)[:, None] * inv[None, :]
    cos, sin = jnp.cos(ang), jnp.sin(ang)
    return jnp.concatenate([cos, cos], axis=-1), jnp.concatenate([-sin, sin], axis=-1)


def _block_sum_matrix():
    w = np.zeros((LANES, LANES), np.float32)
    for j in range(LANES):
        for mm in range(SEL_RATIO):
            for nn in range(CMP_RATIO):
                c = SEL_RATIO * j + mm - nn
                if 0 <= c < LANES:
                    w[c, j] += 1.0
    return w


def _row_tile(m, cap):
    tm = min(m, cap)
    while m % tm:
        tm //= 2
    return tm


def kernel(x_prompt, x_sample, state_ret, state_pool, cache_cmp_k, cache_cmp_v, cache_slc_k, cache_slc_v,
           cache_win_k, cache_win_v, page_table, norm1, w_in, w_pool, pool_scale, cmp_w1_k, cmp_w2_k, cmp_pe_k,
           cmp_w1_v, cmp_w2_v, cmp_pe_v, w_out, norm2, w_up, w_down, norm_f):
    bsz, t, _ = x_prompt.shape
    dbs, dec_seq, _ = x_sample.shape
    depth = w_in.shape[0]
    assert dec_seq == 1 and t % 512 == 0 and dbs % SAMPLE_TB == 0
    n_pages = page_table.shape[1]
    past = n_pages * PAGE_SIZE
    n_pool = cache_cmp_k.shape[1]
    w_buf = cache_win_k.shape[2]
    wlen = min(WINDOW, t)

    w_in_bf = jnp.pad(w_in, ((0, 0), (0, 0), (0, N_PAD - D_IN))).astype(BF16)
    w_out_bf = w_out.astype(BF16)
    w1cat = lambda w1: jnp.concatenate([w1[:, :CMP_STRIDE], w1[:, CMP_STRIDE:]], axis=-1).astype(BF16).reshape(
        depth, CHUNK_W, 2 * HEAD_DIM)
    pe_rows = lambda pe: jnp.pad(pe.reshape(depth, CMP_RATIO, CHUNK_W),
                                 ((0, 0), (0, PE_ROWS - CMP_RATIO), (0, 0))).astype(BF16)
    cw = dict(w1k=w1cat(cmp_w1_k), w2k=cmp_w2_k.astype(BF16), pek=pe_rows(cmp_pe_k),
              w1v=w1cat(cmp_w1_v), w2v=cmp_w2_v.astype(BF16), pev=pe_rows(cmp_pe_v))
    norm1_3, norm2_3 = norm1[:, None, :], norm2[:, None, :]
    pool_scale_3 = pool_scale[:, None, :]
    norm_f2 = norm_f[None, :]
    log_g = jnp.log1p(-jnp.exp2(-5.0 - jnp.arange(R_HEADS, dtype=F32)))
    cos_p, sin_p = _rope_tables(jnp.arange(t))
    cos_s, sin_s = _rope_tables(jnp.full((dbs,), past, jnp.int32))
    wsel = jnp.asarray(_block_sum_matrix(), BF16)
    page_view = lambda c: c.reshape(depth, n_pool, PAGE_SIZE * N_KV, HEAD_DIM)
    cmp_pages = [page_view(c) for c in (cache_cmp_k, cache_cmp_v)]
    slc_pages = [page_view(c) for c in (cache_slc_k, cache_slc_v)]
    win_k = cache_win_k.reshape(depth, dbs, w_buf * N_KV, HEAD_DIM)
    win_v = cache_win_v.reshape(depth, dbs, w_buf * N_KV, HEAD_DIM)

    mp = bsz * t
    tm_p = _row_tile(t, 1024)
    tm_mlp = _row_tile(t, 1024)
    xp = x_prompt.reshape(mp, D_MODEL)
    xs = x_sample.reshape(dbs, D_MODEL)
    outs = {k: [] for k in ("ret_p", "ret_s", "pool_p", "pool_s", "ck_p", "ck_s", "cv_p", "cv_s", "sk_p", "sk_s",
                            "sv_p", "sv_s", "wk_p", "wv_p")}
    zs3_all = []
    for l in range(depth):
        last = l == depth - 1
        z = _inproj(xp, norm1_3, w_in_bf, l, cos_p, sin_p, tm_p)
        o_r, s_new = _ret_prompt(z, log_g, bsz, t)
        o_p = _pool_prompt(z, w_pool, pool_scale_3, l, bsz, t)
        kcs, vcst, vst, vwt = _prep_prompt(z, cw, l, bsz, t)
        o_n = _nsa_prompt(z, kcs, vcst, vst, vwt, bsz, t)
        xp = _outproj(o_r, o_p, o_n, w_out_bf, l, xp, _row_tile(t, 512))
        xp = _mlp(xp, norm2_3, w_up, w_down, l, norm_f2, last, tm_mlp)
        z3 = z.reshape(bsz, t, N_PAD)
        kv = lambda off: z3[:, :, off:off + KV_W].reshape(bsz, t, N_KV, HEAD_DIM)
        outs["ret_p"].append(s_new)
        outs["pool_p"].append(z3[:, t - POOL_BUF:, OFF_PU:OFF_PU + POOL_W])
        outs["ck_p"].append(kv(OFF_KC))
        outs["cv_p"].append(kv(OFF_VC))
        outs["sk_p"].append(kv(OFF_KS))
        outs["sv_p"].append(kv(OFF_VS))
        outs["wk_p"].append(kv(OFF_KW)[:, t - wlen:])
        outs["wv_p"].append(kv(OFF_VW)[:, t - wlen:])
        zs = _inproj(xs, norm1_3, w_in_bf, l, cos_s, sin_s, dbs)
        o_r, s_new, o_p, buf_new = _state_sample(zs, log_g, state_ret, state_pool, w_pool, pool_scale_3, l, dbs, past)
        zs3 = zs.reshape(dbs, 1, N_PAD)
        res = _nsa_sample(zs3, page_table, cmp_pages, slc_pages, win_k, win_v, cw, wsel, l, dbs, past, w_buf,
                          zs3_all if last else None)
        o_n = res[0]
        zs3_all.append(zs3)
        if last:
            wk_s, wv_s = res[1:]
        xs = _outproj(o_r, o_p, o_n.reshape(dbs, NSA_W), w_out_bf, l, xs, dbs)
        xs = _mlp(xs, norm2_3, w_up, w_down, l, norm_f2, last, dbs)
        kvs = lambda off: zs[:, off:off + KV_W].reshape(dbs, 1, N_KV, HEAD_DIM)
        outs["ret_s"].append(s_new)
        outs["pool_s"].append(buf_new)
        outs["ck_s"].append(kvs(OFF_KC))
        outs["cv_s"].append(kvs(OFF_VC))
        outs["sk_s"].append(kvs(OFF_KS))
        outs["sv_s"].append(kvs(OFF_VS))
    st = lambda k: jnp.stack(outs[k])
    win_out = lambda w: w.reshape(depth, dbs, w_buf, N_KV, HEAD_DIM)
    return (xp.reshape(bsz, t, D_MODEL), xs.reshape(dbs, 1, D_MODEL),
            st("ret_p"), st("ret_s"), st("pool_p"), st("pool_s"),
            st("ck_p"), st("ck_s"), st("cv_p"), st("cv_s"), st("sk_p"), st("sk_s"), st("sv_p"), st("sv_s"),
            st("wk_p"), win_out(wk_s), st("wv_p"), win_out(wv_s))
```

```python
import functools
import itertools

import numpy as np
import jax
import jax.numpy as jnp
from jax import lax
from jax.experimental import pallas as pl
from jax.experimental.pallas import tpu as pltpu

F32 = jnp.float32
BF16 = jnp.bfloat16

D_MODEL = 2048
HEAD_DIM = 128
PAGE_SIZE = 128
RET_W = D_MODEL // 4
POOL_W = D_MODEL // 4
NSA_W = D_MODEL - RET_W - POOL_W
R_HEADS = RET_W // HEAD_DIM
N_HEADS = NSA_W // HEAD_DIM
N_KV = max(1, N_HEADS // 4)
HPG = N_HEADS // N_KV
KV_W = N_KV * HEAD_DIM
POOL_WINDOWS = (2, 4, 8, 16)
POOL_GW = POOL_W // len(POOL_WINDOWS)
POOL_BUF = max(POOL_WINDOWS) - 1
RET_CHUNK = 128
CMP_LEN = 32
CMP_STRIDE = 16
CMP_RATIO = CMP_LEN // CMP_STRIDE
SEL_LEN = 64
SEL_RATIO = SEL_LEN // CMP_STRIDE
SEL_TOPK = 16
N_LOCAL_SEL = 2
WINDOW = 512
Q_BLOCK = 128
D_FF = 4 * D_MODEL
ROPE_THETA = 10000.0
EPS = 1e-6
D_IN = 4 * RET_W + POOL_W + NSA_W + 6 * KV_W + 3 * N_HEADS

OFF_RQ = 0
OFF_RK = OFF_RQ + RET_W
OFF_RV = OFF_RK + RET_W
OFF_RG = OFF_RV + RET_W
OFF_PU = OFF_RG + RET_W
OFF_NQ = OFF_PU + POOL_W
OFF_KC = OFF_NQ + NSA_W
OFF_VC = OFF_KC + KV_W
OFF_KS = OFF_VC + KV_W
OFF_VS = OFF_KS + KV_W
OFF_KW = OFF_VS + KV_W
OFF_VW = OFF_KW + KV_W
OFF_GL = OFF_VW + KV_W
PROJ_TN = 512
PROJ_ROW_CHUNK = 256
N_PAD = -(-D_IN // PROJ_TN) * PROJ_TN
ROPE_FULL_TILES = (OFF_RQ // PROJ_TN, OFF_RK // PROJ_TN, OFF_NQ // PROJ_TN, OFF_NQ // PROJ_TN + 1)
ROPE_HALF_TILES = (OFF_KC // PROJ_TN, OFF_KS // PROJ_TN, OFF_KW // PROJ_TN)
RK_TILE = OFF_RK // PROJ_TN

NEG = -1e30
V7X_VMEM_BYTES = 64 * 1024 * 1024
VMEM_LIMIT = V7X_VMEM_BYTES * 3 // 4
VMEM_LIMIT_BIG = V7X_VMEM_BYTES * 29 // 32
LANES = 128
SUBLANES = 8


def _cparams(sem, vmem_limit=VMEM_LIMIT):
    return pltpu.CompilerParams(dimension_semantics=sem, vmem_limit_bytes=vmem_limit)


def _dot(a, b):
    return jnp.dot(a, b, preferred_element_type=F32)


def _dot_nt(a, b):
    return lax.dot_general(a, b, (((1,), (1,)), ((), ())), preferred_element_type=F32)


def _dot_tn(a, b):
    return lax.dot_general(a, b, (((0,), (0,)), ((), ())), preferred_element_type=F32)


def _rms(x, g):
    return x * lax.rsqrt(jnp.mean(x * x, axis=-1, keepdims=True) + EPS) * g


def _silu(x):
    return x * jax.nn.sigmoid(x)


def _inproj_kernel(x_ref, g_ref, w_ref, cos_ref, sin_ref, z_ref, hn_ref):
    j = pl.program_id(1)

    @pl.when(j == 0)
    def _():
        hn_ref[...] = _rms(x_ref[...], g_ref[...]).astype(BF16)

    is_full = functools.reduce(jnp.logical_or, [j == t for t in ROPE_FULL_TILES])
    is_half = functools.reduce(jnp.logical_or, [j == t for t in ROPE_HALF_TILES])
    scale = jnp.where(j == RK_TILE, HEAD_DIM ** -0.5, 1.0).astype(F32)
    is_any = jnp.logical_or(is_full, is_half)
    tm = x_ref.shape[0]
    rc = min(tm, PROJ_ROW_CHUNK)
    for r0 in range(0, tm, rc):
        rows = slice(r0, r0 + rc)
        z = _dot(hn_ref[rows, :], w_ref[...])
        cos_t, sin_t = cos_ref[rows, :] * scale, sin_ref[rows, :] * scale
        tab_lead = (jnp.where(is_any, cos_t, 1.0), jnp.where(is_any, sin_t, 0.0))
        tab_rest = (jnp.where(is_full, cos_t, 1.0), jnp.where(is_full, sin_t, 0.0))
        for s in range(PROJ_TN // HEAD_DIM):
            sl = slice(s * HEAD_DIM, (s + 1) * HEAD_DIM)
            c, sn = tab_lead if s < KV_W // HEAD_DIM else tab_rest
            zs = z[:, sl]
            z_ref[rows, sl] = zs * c + pltpu.roll(zs, HEAD_DIM // 2, 1) * sn


def _inproj(x, gamma, w_in_bf, layer, cos_t, sin_t, tm):
    m = x.shape[0]
    n_row_tiles = cos_t.shape[0] // tm
    return pl.pallas_call(
        _inproj_kernel,
        grid=(m // tm, N_PAD // PROJ_TN),
        in_specs=[
            pl.BlockSpec((tm, D_MODEL), lambda i, j: (i, 0)),
            pl.BlockSpec((None, 1, D_MODEL), lambda i, j: (layer, 0, 0)),
            pl.BlockSpec((None, D_MODEL, PROJ_TN), lambda i, j: (layer, 0, j)),
            pl.BlockSpec((tm, HEAD_DIM), lambda i, j: (i % n_row_tiles, 0)),
            pl.BlockSpec((tm, HEAD_DIM), lambda i, j: (i % n_row_tiles, 0)),
        ],
        out_specs=pl.BlockSpec((tm, PROJ_TN), lambda i, j: (i, j)),
        out_shape=jax.ShapeDtypeStruct((m, N_PAD), F32),
        scratch_shapes=[pltpu.VMEM((tm, D_MODEL), BF16)],
        compiler_params=_cparams(("parallel", "arbitrary")),
        name="inproj",
    )(x, gamma, w_in_bf, cos_t, sin_t)


def _outproj_kernel(r_ref, p_ref, n_ref, w_ref, x_ref, o_ref):
    acc = _dot(r_ref[...].astype(BF16), w_ref[0:RET_W, :])
    acc += _dot(p_ref[...].astype(BF16), w_ref[RET_W:RET_W + POOL_W, :])
    acc += _dot(n_ref[...].astype(BF16), w_ref[RET_W + POOL_W:, :])
    o_ref[...] = x_ref[...] + acc


def _outproj(o_r, o_p, o_n, w_out_bf, layer, x, tm, tn=D_MODEL):
    m = x.shape[0]
    return pl.pallas_call(
        _outproj_kernel,
        grid=(m // tm, D_MODEL // tn),
        in_specs=[
            pl.BlockSpec((tm, RET_W), lambda i, j: (i, 0)),
            pl.BlockSpec((tm, POOL_W), lambda i, j: (i, 0)),
            pl.BlockSpec((tm, NSA_W), lambda i, j: (i, 0)),
            pl.BlockSpec((None, D_MODEL, tn), lambda i, j: (layer, 0, j)),
            pl.BlockSpec((tm, tn), lambda i, j: (i, j)),
        ],
        out_specs=pl.BlockSpec((tm, tn), lambda i, j: (i, j)),
        out_shape=jax.ShapeDtypeStruct((m, D_MODEL), F32),
        compiler_params=_cparams(("parallel", "parallel")),
        name="outproj",
    )(o_r, o_p, o_n, w_out_bf, x)


def _mlp_kernel(x_ref, g_ref, wu_ref, wd_ref, gf_ref, o_ref, hn_ref, *, final_norm):
    f = pl.program_id(1)

    @pl.when(f == 0)
    def _():
        hn_ref[...] = _rms(x_ref[...], g_ref[...]).astype(BF16)
        o_ref[...] = jnp.zeros_like(o_ref)

    u = jnp.maximum(_dot(hn_ref[...], wu_ref[...].astype(BF16)), 0.0)
    o_ref[...] += _dot((u * u).astype(BF16), wd_ref[...].astype(BF16))

    @pl.when(f == pl.num_programs(1) - 1)
    def _():
        y = x_ref[...] + o_ref[...]
        o_ref[...] = _rms(y, gf_ref[...]) if final_norm else y


def _mlp(x, gamma, w_up, w_down, layer, gamma_f, final_norm, tm, tf=512):
    m = x.shape[0]
    return pl.pallas_call(
        functools.partial(_mlp_kernel, final_norm=final_norm),
        grid=(m // tm, D_FF // tf),
        in_specs=[
            pl.BlockSpec((tm, D_MODEL), lambda i, f: (i, 0), pipeline_mode=pl.Buffered(1)),
            pl.BlockSpec((None, 1, D_MODEL), lambda i, f: (layer, 0, 0)),
            pl.BlockSpec((None, D_MODEL, tf), lambda i, f: (layer, 0, f)),
            pl.BlockSpec((None, tf, D_MODEL), lambda i, f: (layer, f, 0)),
            pl.BlockSpec((1, D_MODEL), lambda i, f: (0, 0)),
        ],
        out_specs=pl.BlockSpec((tm, D_MODEL), lambda i, f: (i, 0)),
        out_shape=jax.ShapeDtypeStruct((m, D_MODEL), F32),
        scratch_shapes=[pltpu.VMEM((tm, D_MODEL), BF16)],
        compiler_params=_cparams(("parallel", "arbitrary"), VMEM_LIMIT_BIG),
        name="mlp",
    )(x, gamma, w_up, w_down, gamma_f)


def _ret_prompt_kernel(lg_ref, q_ref, k_ref, v_ref, g_ref, o_ref, s_ref, *, n_chunk):
    ti = pl.program_id(1)
    L = RET_CHUNK
    ri = lax.broadcasted_iota(jnp.int32, (L, L), 0)
    ci = lax.broadcasted_iota(jnp.int32, (L, L), 1)
    diff = (ri - ci).astype(F32)
    causal = diff >= 0
    idx = lax.broadcasted_iota(jnp.int32, (L, 1), 0).astype(F32)
    consts = []
    for h in range(R_HEADS):
        lg = lg_ref[h]
        decay = jnp.where(causal, jnp.exp(jnp.where(causal, diff, 0.0) * lg), 0.0)
        q_dec = jnp.exp((idx + 1.0) * lg)
        k_dec = jnp.exp((L - 1.0 - idx) * lg)
        s_dec = jnp.exp(jnp.full((1, 1), float(L), F32) * lg)
        consts.append((decay, q_dec, k_dec, s_dec))

    @pl.when(ti == 0)
    def _():
        s_ref[...] = jnp.zeros_like(s_ref)

    def body(c, states):
        rows = pl.ds(pl.multiple_of(c * L, L), L)
        new_states = []
        for h, (decay, q_dec, k_dec, s_dec) in enumerate(consts):
            sl = slice(h * HEAD_DIM, (h + 1) * HEAD_DIM)
            q, k, v = q_ref[rows, sl], k_ref[rows, sl], v_ref[rows, sl]
            vb = v.astype(BF16)
            att = _dot_nt(q.astype(BF16), k.astype(BF16)) * decay
            o = _dot(att.astype(BF16), vb) + _dot((q * q_dec).astype(BF16), states[h].astype(BF16))
            new_states.append(s_dec * states[h] + _dot_tn((k * k_dec).astype(BF16), vb))
            o = o * lax.rsqrt(jnp.mean(o * o, axis=-1, keepdims=True) + EPS)
            o_ref[rows, sl] = o * _silu(g_ref[rows, sl])
        return tuple(new_states)

    states = lax.fori_loop(0, n_chunk, body, tuple(s_ref[h] for h in range(R_HEADS)))
    for h in range(R_HEADS):
        s_ref[h] = states[h]


def _ret_prompt(z, log_g, bsz, t, tt=1024):
    tt = min(tt, t)
    nt = t // tt
    blk = lambda off: pl.BlockSpec((tt, RET_W), lambda b, i, lg: (b * nt + i, off // RET_W))
    return pl.pallas_call(
        functools.partial(_ret_prompt_kernel, n_chunk=tt // RET_CHUNK),
        grid_spec=pltpu.PrefetchScalarGridSpec(
            num_scalar_prefetch=1,
            grid=(bsz, nt),
            in_specs=[blk(OFF_RQ), blk(OFF_RK), blk(OFF_RV), blk(OFF_RG)],
            out_specs=[
                pl.BlockSpec((tt, RET_W), lambda b, i, lg: (b * nt + i, 0)),
                pl.BlockSpec((None, R_HEADS, HEAD_DIM, HEAD_DIM), lambda b, i, lg: (b, 0, 0, 0)),
            ],
        ),
        out_shape=[
            jax.ShapeDtypeStruct((bsz * t, RET_W), F32),
            jax.ShapeDtypeStruct((bsz, R_HEADS, HEAD_DIM, HEAD_DIM), F32),
        ],
        compiler_params=_cparams(("parallel", "arbitrary")),
        name="ret_prompt",
    )(log_g, z, z, z, z)


POOL_HALO = 16


def _pool_prompt_kernel(u_ref, wp_ref, ps_ref, o_ref, ext_ref, *, tt):
    t = pl.program_id(1)

    @pl.when(t == 0)
    def _():
        ext_ref[0:POOL_HALO, :] = jnp.zeros((POOL_HALO, POOL_W), F32)

    u = u_ref[...]
    ext_ref[POOL_HALO:POOL_HALO + tt, :] = u
    pos = t * tt + lax.broadcasted_iota(jnp.int32, (tt, 1), 0)
    for g, w in enumerate(POOL_WINDOWS):
        sl = slice(g * POOL_GW, (g + 1) * POOL_GW)
        s = u[:, sl]
        for i in range(1, w):
            s = s + ext_ref[pl.ds(POOL_HALO - i, tt), sl]
        cnt = jnp.minimum(pos + 1, w).astype(F32)
        p = s / cnt - u[:, sl]
        o_ref[:, sl] = _dot(p.astype(BF16), wp_ref[g].astype(BF16)) * ps_ref[:, sl]
    ext_ref[0:POOL_HALO, :] = ext_ref[tt:tt + POOL_HALO, :]


def _pool_prompt(z, w_pool, pool_scale, layer, bsz, t, tt=512):
    nt = t // tt
    return pl.pallas_call(
        functools.partial(_pool_prompt_kernel, tt=tt),
        grid=(bsz, nt),
        in_specs=[
            pl.BlockSpec((tt, POOL_W), lambda b, i: (b * nt + i, OFF_PU // POOL_W)),
            pl.BlockSpec((None, len(POOL_WINDOWS), POOL_GW, POOL_GW), lambda b, i: (layer, 0, 0, 0)),
            pl.BlockSpec((None, 1, POOL_W), lambda b, i: (layer, 0, 0)),
        ],
        out_specs=pl.BlockSpec((tt, POOL_W), lambda b, i: (b * nt + i, 0)),
        out_shape=jax.ShapeDtypeStruct((bsz * t, POOL_W), F32),
        scratch_shapes=[pltpu.VMEM((POOL_HALO + tt, POOL_W), F32)],
        compiler_params=_cparams(("parallel", "arbitrary")),
        name="pool_prompt",
    )(z, w_pool, pool_scale)


CHUNK_W = CMP_STRIDE * HEAD_DIM
PE_ROWS = 16


def _compress_core(chunks, w1_ref, w2_ref, pe_ref):
    n_rows = chunks.shape[0]
    acc = _dot(jnp.concatenate([chunks, pe_ref[...]], axis=0), w1_ref[...])
    first = acc[:n_rows, :HEAD_DIM]
    second = acc[:n_rows, HEAD_DIM:]
    pe_term = acc[n_rows:n_rows + 1, :HEAD_DIM] + acc[n_rows + 1:n_rows + 2, HEAD_DIM:]
    h = first + pltpu.roll(second, n_rows - 1, 0) + pe_term
    return _dot(_silu(h).astype(BF16), w2_ref[...])


def _prep_prompt_kernel(kc_ref, vc_ref, vs_ref, vw_ref, w1k_ref, w2k_ref, pek_ref, w1v_ref, w2v_ref, pev_ref,
                        kcs_ref, vcst_ref, vst_ref, vwt_ref, *, n_chunk, n_tile):
    chunks_of = lambda r: jnp.concatenate(
        [r[pl.ds(s, n_chunk, stride=CMP_STRIDE), :].astype(BF16) for s in range(CMP_STRIDE)], axis=1)
    kcs_ref[...] = _compress_core(chunks_of(kc_ref), w1k_ref, w2k_ref, pek_ref).astype(BF16)
    vcst_ref[...] = _compress_core(chunks_of(vc_ref), w1v_ref, w2v_ref, pev_ref).T.astype(BF16)

    def tbody(i, carry):
        rows = pl.ds(pl.multiple_of(i * HEAD_DIM, HEAD_DIM), HEAD_DIM)
        vst_ref[i] = vs_ref[rows, :].T.astype(BF16)
        vwt_ref[i] = vw_ref[rows, :].T.astype(BF16)
        return carry

    lax.fori_loop(0, n_tile, tbody, 0)


def _prep_prompt(z, cw, layer, bsz, t):
    n_chunk = t // CMP_STRIDE
    n_tile = t // HEAD_DIM
    col = lambda off: pl.BlockSpec((t, HEAD_DIM), lambda b, g: (b, off // HEAD_DIM + g))
    w1 = pl.BlockSpec((None, CHUNK_W, 2 * HEAD_DIM), lambda b, g: (layer, 0, 0))
    w2 = pl.BlockSpec((None, HEAD_DIM, HEAD_DIM), lambda b, g: (layer, 0, 0))
    pe = pl.BlockSpec((None, PE_ROWS, CHUNK_W), lambda b, g: (layer, 0, 0))
    return pl.pallas_call(
        functools.partial(_prep_prompt_kernel, n_chunk=n_chunk, n_tile=n_tile),
        grid=(bsz, N_KV),
        in_specs=[col(OFF_KC), col(OFF_VC), col(OFF_VS), col(OFF_VW), w1, w2, pe, w1, w2, pe],
        out_specs=[
            pl.BlockSpec((None, None, n_chunk, HEAD_DIM), lambda b, g: (b, g, 0, 0)),
            pl.BlockSpec((None, None, HEAD_DIM, n_chunk), lambda b, g: (b, g, 0, 0)),
            pl.BlockSpec((None, None, n_tile, HEAD_DIM, HEAD_DIM), lambda b, g: (b, g, 0, 0, 0)),
            pl.BlockSpec((None, None, n_tile, HEAD_DIM, HEAD_DIM), lambda b, g: (b, g, 0, 0, 0)),
        ],
        out_shape=[
            jax.ShapeDtypeStruct((bsz, N_KV, n_chunk, HEAD_DIM), BF16),
            jax.ShapeDtypeStruct((bsz, N_KV, HEAD_DIM, n_chunk), BF16),
            jax.ShapeDtypeStruct((bsz, N_KV, n_tile, HEAD_DIM, HEAD_DIM), BF16),
            jax.ShapeDtypeStruct((bsz, N_KV, n_tile, HEAD_DIM, HEAD_DIM), BF16),
        ],
        compiler_params=_cparams(("parallel", "parallel")),
        name="prep_prompt",
    )(z, z, z, z, cw["w1k"], cw["w2k"], cw["pek"], cw["w1v"], cw["w2v"], cw["pev"])


LOG2E = 1.4426950408889634


def _softmax_tiles(tiles, carry):
    m, l, acc = carry
    m_new = m
    for s, _ in tiles:
        m_new = jnp.maximum(m_new, jnp.max(s, axis=0, keepdims=True))
    alpha = jnp.exp2(m - m_new)
    l = alpha * l
    acc = alpha * acc
    for s, vt in tiles:
        p = jnp.exp2(s - m_new)
        l = l + jnp.sum(p, axis=0, keepdims=True)
        acc = acc + _dot(vt, p.astype(BF16))
    return m_new, l, acc


def _select_blocks(score, score_ref):
    n_blocks = score.shape[0]
    score_ref[...] = score
    groups = [score[v * SUBLANES:(v + 1) * SUBLANES] for v in range(n_blocks // SUBLANES)]
    jb = lax.broadcasted_iota(jnp.int32, groups[0].shape, 0)
    ranks = [jnp.zeros(gv.shape, F32) for gv in groups]
    for k in range(n_blocks):
        rk = score_ref[pl.ds(k, 1), :]
        for v, gv in enumerate(groups):
            if v * SUBLANES > k:
                beats = rk >= gv
            elif (v + 1) * SUBLANES - 1 <= k:
                beats = rk > gv
            else:
                beats = jnp.logical_or(rk > gv, jnp.logical_and(jb + v * SUBLANES > k, rk == gv))
            ranks[v] = ranks[v] + jnp.where(beats, 1.0, 0.0)
    rank = jnp.concatenate(ranks, axis=0)
    return jnp.logical_and(rank < SEL_TOPK, score > -jnp.inf)


def _nsa_prompt_kernel(q0_ref, q1_ref, gl_ref, kcs_ref, vcst_ref, ks_ref, vst_ref, kw_ref, vwt_ref, o_ref,
                       imp_ref, selb_ref, gt_ref, score_ref, *, n_chunk, n_sel):
    i = pl.program_id(1)
    groups = range(N_KV)
    qw = HPG * Q_BLOCK
    qscale = HEAD_DIM ** -0.5 * LOG2E
    lane = lax.broadcasted_iota(jnp.int32, (1, qw), 1)
    tl4 = jnp.bitwise_and(lane, Q_BLOCK - 1)
    tpos4 = i * Q_BLOCK + tl4
    tpos = i * Q_BLOCK + lax.broadcasted_iota(jnp.int32, (1, Q_BLOCK), 1)
    kv_lanes = lambda g: slice(g * HEAD_DIM, (g + 1) * HEAD_DIM)

    def q_transposed(g):
        q = (q0_ref, q1_ref)[g][...]
        return jnp.concatenate([(q[:, h * HEAD_DIM:(h + 1) * HEAD_DIM] * qscale).T for h in range(HPG)],
                               axis=1).astype(BF16)

    qt = [q_transposed(g) for g in groups]

    c_end = lax.broadcasted_iota(jnp.int32, (n_chunk, 1), 0) * CMP_STRIDE + (CMP_LEN - 1)
    mask_c = jnp.logical_and(c_end <= tpos4, c_end < n_chunk * CMP_STRIDE)

    def compressed(g):
        sc = _dot(kcs_ref[g], qt[g])
        m = jnp.max(jnp.where(mask_c, sc, NEG), axis=0, keepdims=True)
        e = jnp.where(mask_c, jnp.exp2(sc - m), 0.0)
        d = jnp.sum(e, axis=0, keepdims=True)
        pc = e * (1.0 / jnp.where(d > 0, d, 1.0))
        imp = pc[:, 0:Q_BLOCK]
        for h in range(1, HPG):
            imp = imp + pc[:, h * Q_BLOCK:(h + 1) * Q_BLOCK]
        return _dot(vcst_ref[g], pc.astype(BF16)), imp

    oc, imps = zip(*[compressed(g) for g in groups])

    jb = lax.broadcasted_iota(jnp.int32, (n_sel, 1), 0)
    blk_valid = jb * SEL_LEN <= tpos
    cur = lax.shift_right_logical(tpos, int(np.log2(SEL_LEN)))
    forced = jnp.logical_or(jb == 0, jnp.logical_and(jb <= cur, jb > cur - N_LOCAL_SEL))

    def select(g):
        imp_ref[g, 0:SUBLANES, :] = jnp.zeros((SUBLANES, Q_BLOCK), F32)
        imp_ref[g, SUBLANES:SUBLANES + n_chunk, :] = imps[g]
        imp_ref[g, SUBLANES + n_chunk:, :] = jnp.zeros((SUBLANES, Q_BLOCK), F32)
        p_slc = None
        for mm in range(SEL_RATIO):
            for nn in range(CMP_RATIO):
                r = imp_ref[g, pl.ds(SUBLANES + mm - nn, n_sel, stride=SEL_RATIO), :]
                p_slc = r if p_slc is None else p_slc + r
        score = jnp.where(blk_valid, jnp.where(forced, jnp.inf, p_slc), -jnp.inf)
        sel = _select_blocks(score, score_ref.at[g])
        selb_ref[g] = jnp.concatenate([jnp.where(sel, 0.0, NEG)] * HPG, axis=1)

    for g in groups:
        select(g)

    row = lax.broadcasted_iota(jnp.int32, (Q_BLOCK, 1), 0)
    causal_bias = jnp.where(row <= tl4, 0.0, NEG)
    init = (jnp.full((1, qw), NEG, F32), jnp.zeros((1, qw), F32), jnp.zeros((HEAD_DIM, qw), F32))
    blocks_per_tile = Q_BLOCK // SEL_LEN

    n_back = WINDOW // Q_BLOCK

    def window(g):
        win_tiles = []
        for w in range(n_back + 1):
            kb = i - n_back + w
            kbc = jnp.maximum(kb, 0)
            rows = pl.ds(pl.multiple_of(kbc * Q_BLOCK, Q_BLOCK), Q_BLOCK)
            s = _dot(kw_ref[rows, kv_lanes(g)].astype(BF16), qt[g])
            if w == n_back:
                s = s + causal_bias
            else:
                before_start = jnp.where(kb >= 0, 0.0, NEG).astype(F32)
                s = s + (jnp.where(row > tl4, before_start, NEG) if w == 0 else before_start)
            win_tiles.append((s, vwt_ref[g, kbc]))
        return _softmax_tiles(win_tiles, init)[1:]

    l_w, a_w = zip(*[window(g) for g in groups])

    n_pair = lax.shift_right_logical(i, 1)
    odd_bias = jnp.where(jnp.bitwise_and(i, 1) == 1, 0.0, NEG).astype(F32)

    def sel_scores(g, kt, tile_bias):
        rows = pl.ds(pl.multiple_of(kt * Q_BLOCK, Q_BLOCK), Q_BLOCK)
        s = _dot(ks_ref[rows, kv_lanes(g)].astype(BF16), qt[g])
        parts = []
        for bb in range(blocks_per_tile):
            bias = selb_ref[g, pl.ds(kt * blocks_per_tile + bb, 1), :] + tile_bias
            parts.append(s[bb * SEL_LEN:(bb + 1) * SEL_LEN] + bias)
        return jnp.concatenate(parts, axis=0)

    def sel_body(u, carry):
        scores = [(sel_scores(g, 2 * u, 0.0), sel_scores(g, 2 * u + 1, 0.0)) for g in groups]
        return tuple(_softmax_tiles([(scores[g][0], vst_ref[g, 2 * u]), (scores[g][1], vst_ref[g, 2 * u + 1])],
                                    carry[g]) for g in groups)

    carry = lax.fori_loop(0, n_pair, sel_body, tuple(init for g in groups))
    prev = jnp.maximum(i - 1, 0)
    last_scores = [(sel_scores(g, prev, odd_bias), sel_scores(g, i, 0.0) + causal_bias) for g in groups]

    gt_ref[...] = jax.nn.sigmoid(gl_ref[...]).T
    for g in groups:
        _, l_s, a_s = _softmax_tiles([(last_scores[g][0], vst_ref[g, prev]), (last_scores[g][1], vst_ref[g, i])],
                                     carry[g])
        gate = lambda c: jnp.concatenate(
            [gt_ref[pl.ds((g * HPG + h) * 3 + c, 1), :] for h in range(HPG)], axis=1)
        ot = gate(0) * oc[g] + gate(1) * (a_s * (1.0 / l_s)) + gate(2) * (a_w[g] * (1.0 / l_w[g]))
        for h in range(HPG):
            col = (g * HPG + h) * HEAD_DIM
            o_ref[:, col:col + HEAD_DIM] = ot[:, h * Q_BLOCK:(h + 1) * Q_BLOCK].T


def _nsa_prompt(z, kcs, vcst, vst, vwt, bsz, t):
    n_chunk = t // CMP_STRIDE
    n_sel = t // SEL_LEN
    n_tile = t // Q_BLOCK
    nqb = t // Q_BLOCK
    qw = HPG * HEAD_DIM
    return pl.pallas_call(
        functools.partial(_nsa_prompt_kernel, n_chunk=n_chunk, n_sel=n_sel),
        grid=(bsz, nqb),
        in_specs=[
            pl.BlockSpec((Q_BLOCK, qw), lambda b, i: (b * nqb + i, OFF_NQ // qw)),
            pl.BlockSpec((Q_BLOCK, qw), lambda b, i: (b * nqb + i, OFF_NQ // qw + 1)),
            pl.BlockSpec((Q_BLOCK, LANES), lambda b, i: (b * nqb + i, OFF_GL // LANES)),
            pl.BlockSpec((None, N_KV, n_chunk, HEAD_DIM), lambda b, i: (b, 0, 0, 0)),
            pl.BlockSpec((None, N_KV, HEAD_DIM, n_chunk), lambda b, i: (b, 0, 0, 0)),
            pl.BlockSpec((t, KV_W), lambda b, i: (b, OFF_KS // KV_W)),
            pl.BlockSpec((None, N_KV, n_tile, HEAD_DIM, HEAD_DIM), lambda b, i: (b, 0, 0, 0, 0)),
            pl.BlockSpec((t, KV_W), lambda b, i: (b, OFF_KW // KV_W)),
            pl.BlockSpec((None, N_KV, n_tile, HEAD_DIM, HEAD_DIM), lambda b, i: (b, 0, 0, 0, 0)),
        ],
        out_specs=pl.BlockSpec((Q_BLOCK, NSA_W), lambda b, i: (b * nqb + i, 0)),
        out_shape=jax.ShapeDtypeStruct((bsz * t, NSA_W), F32),
        scratch_shapes=[
            pltpu.VMEM((N_KV, n_chunk + 2 * SUBLANES, Q_BLOCK), F32),
            pltpu.VMEM((N_KV, n_sel, HPG * Q_BLOCK), F32),
            pltpu.VMEM((LANES, Q_BLOCK), F32),
            pltpu.VMEM((N_KV, n_sel, Q_BLOCK), F32),
        ],
        compiler_params=_cparams(("parallel", "arbitrary")),
        name="nsa_prompt",
    )(z, z, z, kcs, vcst, z, vst, z, vwt)


SAMPLE_TB = 8


def _state_sample_kernel(lg_ref, q_ref, k_ref, v_ref, g_ref, u_ref, sr_ref, sp_ref, wp_ref, ps_ref,
                         or_ref, sn_ref, op_ref, pn_ref, *, pos):
    tb = SAMPLE_TB
    pad = jnp.zeros((SUBLANES - 1, HEAD_DIM), F32)
    for h in range(R_HEADS):
        gamma = jnp.exp(jnp.full((1, HEAD_DIM), lg_ref[h], F32))
        sl = slice(h * HEAD_DIM, (h + 1) * HEAD_DIM)
        rows = []
        for b in range(tb):
            q = q_ref[b:b + 1, sl]
            k = k_ref[b:b + 1, sl]
            v = v_ref[b:b + 1, sl]
            state = sr_ref[b, h]
            k8 = jnp.concatenate([k, pad], axis=0).astype(BF16)
            v8 = jnp.concatenate([v, pad], axis=0).astype(BF16)
            qk = jnp.sum(q * k, axis=1, keepdims=True)
            qs = _dot(jnp.concatenate([q * gamma, pad], axis=0).astype(BF16), state.astype(BF16))[0:1]
            rows.append(qk * v + qs)
            sn_ref[b, h] = gamma[:, 0:1] * state + _dot_tn(k8, v8)
        o = jnp.concatenate(rows, axis=0)
        o = o * lax.rsqrt(jnp.mean(o * o, axis=-1, keepdims=True) + EPS)
        or_ref[:, sl] = o * _silu(g_ref[:, sl])
    u = u_ref[...]
    for gi, w in enumerate(POOL_WINDOWS):
        sl = slice(gi * POOL_GW, (gi + 1) * POOL_GW)
        s = u[:, sl] + jnp.sum(sp_ref[:, POOL_BUF - (w - 1):POOL_BUF, sl], axis=1)
        p = s / float(min(pos + 1, w)) - u[:, sl]
        op_ref[:, sl] = _dot(p.astype(BF16), wp_ref[gi].astype(BF16)) * ps_ref[:, sl]
    pn_ref[:, 0:POOL_BUF - 1, :] = sp_ref[:, 1:POOL_BUF, :]
    for b in range(tb):
        pn_ref[b, POOL_BUF - 1:POOL_BUF, :] = u[b:b + 1, :]


def _state_sample(zs, log_g, state_ret, state_pool, w_pool, pool_scale, layer, dbs, pos):
    tb = SAMPLE_TB
    col = lambda off: pl.BlockSpec((tb, RET_W), lambda i, lg: (i, off // RET_W))
    return pl.pallas_call(
        functools.partial(_state_sample_kernel, pos=pos),
        grid_spec=pltpu.PrefetchScalarGridSpec(
            num_scalar_prefetch=1,
            grid=(dbs // tb,),
            in_specs=[
                col(OFF_RQ), col(OFF_RK), col(OFF_RV), col(OFF_RG), col(OFF_PU),
                pl.BlockSpec((None, tb, R_HEADS, HEAD_DIM, HEAD_DIM), lambda i, lg: (layer, i, 0, 0, 0)),
                pl.BlockSpec((None, tb, POOL_BUF, POOL_W), lambda i, lg: (layer, i, 0, 0)),
                pl.BlockSpec((None, len(POOL_WINDOWS), POOL_GW, POOL_GW), lambda i, lg: (layer, 0, 0, 0)),
                pl.BlockSpec((None, 1, POOL_W), lambda i, lg: (layer, 0, 0)),
            ],
            out_specs=[
                pl.BlockSpec((tb, RET_W), lambda i, lg: (i, 0)),
                pl.BlockSpec((tb, R_HEADS, HEAD_DIM, HEAD_DIM), lambda i, lg: (i, 0, 0, 0)),
                pl.BlockSpec((tb, POOL_W), lambda i, lg: (i, 0)),
                pl.BlockSpec((tb, POOL_BUF, POOL_W), lambda i, lg: (i, 0, 0)),
            ],
        ),
        out_shape=[
            jax.ShapeDtypeStruct((dbs, RET_W), F32),
            jax.ShapeDtypeStruct((dbs, R_HEADS, HEAD_DIM, HEAD_DIM), F32),
            jax.ShapeDtypeStruct((dbs, POOL_W), F32),
            jax.ShapeDtypeStruct((dbs, POOL_BUF, POOL_W), F32),
        ],
        compiler_params=_cparams(("parallel",)),
        name="state_sample",
    )(log_g, zs, zs, zs, zs, zs, state_ret, state_pool, w_pool, pool_scale)


N_CACHE = 4


def _softmax_lanes_with_new(s, s_new):
    m = jnp.maximum(jnp.max(s, axis=1, keepdims=True), s_new)
    p = jnp.exp(s - m)
    p_new = jnp.exp(s_new - m)
    inv = 1.0 / (jnp.sum(p, axis=1, keepdims=True) + p_new)
    return p * inv, p_new * inv


def _rank_select(srow):
    across = jnp.broadcast_to(srow, (LANES, LANES))
    down = across.T
    r_i = lax.broadcasted_iota(jnp.int32, (LANES, LANES), 0)
    c_i = lax.broadcasted_iota(jnp.int32, (LANES, LANES), 1)
    beats = jnp.logical_or(down > across, jnp.logical_and(down == across, r_i < c_i))
    rank = jnp.sum(jnp.where(beats, 1.0, 0.0), axis=0, keepdims=True)
    return jnp.logical_and(rank < SEL_TOPK, srow > -jnp.inf)


def _nsa_sample_sequence(q0_ref, q1_ref, ksn_ref, vsn_ref, kwn_ref, vwn_ref, gl_ref, wsel_ref,
                         w1k_ref, w2k_ref, pek_ref, w1v_ref, w2v_ref, pev_ref, wk_ref, wv_ref,
                         ck, cv, sk, sv, o_ref, *, n_pages, past, w_buf):
    scale = HEAD_DIM ** -0.5
    chunks_per_page = PAGE_SIZE // CMP_STRIDE
    n_chunk = n_pages * chunks_per_page
    n_sel = -(-(past + 1) // SEL_LEN)
    assert n_chunk <= LANES and n_sel <= LANES and past % SEL_LEN == 0

    group_rows = lambda ref, g, n: ref[pl.ds(g, n, stride=N_KV), :]

    def chunks_of(prefs):
        return jnp.concatenate(
            [group_rows(p, g, PAGE_SIZE).astype(BF16).reshape(chunks_per_page, CHUNK_W)
             for g in range(N_KV) for p in prefs],
            axis=0)

    ksum = _compress_core(chunks_of(ck), w1k_ref, w2k_ref, pek_ref)
    yield
    vsum = _compress_core(chunks_of(cv), w1v_ref, w2v_ref, pev_ref)
    yield

    lane = lax.broadcasted_iota(jnp.int32, (1, LANES), 1)
    q_refs = (q0_ref, q1_ref)
    zpad = jnp.zeros((SUBLANES - HPG, HEAD_DIM), F32)
    q8, oc, imps = [], [], []
    for g in range(N_KV):
        qg = jnp.concatenate([q_refs[g][:, h * HEAD_DIM:(h + 1) * HEAD_DIM] for h in range(HPG)] + [zpad],
                             axis=0) * scale
        q8.append(qg.astype(BF16))
        kcs = ksum[g * n_chunk:(g + 1) * n_chunk]
        vcs = vsum[g * n_chunk:(g + 1) * n_chunk]
        s_c = _dot_nt(q8[g], kcs.astype(BF16))
        c_end = lane[:, :n_chunk] * CMP_STRIDE + (CMP_LEN - 1)
        mask_c = jnp.logical_and(c_end <= past, c_end < n_chunk * CMP_STRIDE)
        m = jnp.max(jnp.where(mask_c, s_c, NEG), axis=1, keepdims=True)
        e = jnp.where(mask_c, jnp.exp(s_c - m), 0.0)
        d = jnp.sum(e, axis=1, keepdims=True)
        pc = e * (1.0 / jnp.where(d > 0, d, 1.0))
        oc.append(_dot(pc.astype(BF16), vcs.astype(BF16)))
        imps.append(jnp.sum(pc[0:HPG], axis=0, keepdims=True))
    yield

    imp = jnp.concatenate(imps + [jnp.zeros((SUBLANES - N_KV, n_chunk), F32)], axis=0)
    if n_chunk < LANES:
        imp = jnp.concatenate([imp, jnp.zeros((SUBLANES, LANES - n_chunk), F32)], axis=1)
    hi = imp.astype(BF16)
    r1 = imp - hi.astype(F32)
    mid = r1.astype(BF16)
    lo = (r1 - mid.astype(F32)).astype(BF16)
    wsel = wsel_ref[...]
    p_slc = _dot(hi, wsel) + _dot(mid, wsel) + _dot(lo, wsel)
    cur = past // SEL_LEN
    blk_valid = jnp.logical_and(lane * SEL_LEN <= past, lane < n_sel)
    forced = jnp.logical_or(lane == 0, jnp.logical_and(lane <= cur, lane > cur - N_LOCAL_SEL))
    score = jnp.where(blk_valid, jnp.where(forced, jnp.inf, p_slc), -jnp.inf)
    yield

    gate = jax.nn.sigmoid(gl_ref[...])
    blocks_per_page = PAGE_SIZE // SEL_LEN
    wl = lax.broadcasted_iota(jnp.int32, (1, w_buf), 1)
    dpos = w_buf - wl
    bias_w = jnp.where(jnp.logical_and(jnp.logical_and(dpos >= 0, dpos < WINDOW), past - dpos >= 0), 0.0, NEG)
    for g in range(N_KV):
        kv_new = slice(g * HEAD_DIM, (g + 1) * HEAD_DIM)
        q_f = q8[g].astype(F32)
        selb = jnp.where(_rank_select(score[g:g + 1]), 0.0, NEG)
        bias_pages = []
        for p in range(n_pages):
            bias = selb[:, p * blocks_per_page:p * blocks_per_page + 1]
            for bb in range(1, blocks_per_page):
                bias = jnp.where(lane < bb * SEL_LEN, bias, selb[:, p * blocks_per_page + bb:p * blocks_per_page + bb + 1])
            bias_pages.append(bias)
        k_all = jnp.concatenate([group_rows(p, g, PAGE_SIZE) for p in sk], axis=0).astype(BF16)
        v_all = jnp.concatenate([group_rows(p, g, PAGE_SIZE) for p in sv], axis=0).astype(BF16)
        s_all = _dot_nt(q8[g], k_all) + jnp.concatenate(bias_pages, axis=1)
        s_new = jnp.sum(q_f * ksn_ref[:, kv_new].astype(BF16).astype(F32), axis=1, keepdims=True)
        p_all, p_new = _softmax_lanes_with_new(s_all, s_new)
        o_s = p_new * vsn_ref[:, kv_new] + _dot(p_all.astype(BF16), v_all)
        yield
        s_w = _dot_nt(q8[g], group_rows(wk_ref, g, w_buf).astype(BF16)) + bias_w
        s_wn = jnp.sum(q_f * kwn_ref[:, kv_new].astype(BF16).astype(F32), axis=1, keepdims=True)
        p_w, p_wn = _softmax_lanes_with_new(s_w, s_wn)
        o_w = p_wn * vwn_ref[:, kv_new] + _dot(p_w.astype(BF16), group_rows(wv_ref, g, w_buf).astype(BF16))
        for h in range(HPG):
            c0 = (g * HPG + h) * 3
            og = (gate[:, c0:c0 + 1] * oc[g][h:h + 1] + gate[:, c0 + 1:c0 + 2] * o_s[h:h + 1]
                  + gate[:, c0 + 2:c0 + 3] * o_w[h:h + 1])
            o_ref[:, (g * HPG + h) * HEAD_DIM:(g * HPG + h + 1) * HEAD_DIM] = og


SAMPLE_SEQS = 1


def _nsa_sample_kernel(pt_ref, *refs, n_pages, past, w_buf, layer, n_older):
    blocked, rest = refs[:16], refs[16:]
    caches = rest[:N_CACHE]
    older = rest[N_CACHE:N_CACHE + 4 * n_older]
    outs = rest[N_CACHE + 4 * n_older:-2]
    o_ref = outs[0]
    pages_buf, sem = rest[-2:]
    b = pl.program_id(0)

    def page_copies(step, slot):
        return [pltpu.make_async_copy(caches[c].at[layer, pt_ref[(step * SAMPLE_SEQS + sq) * n_pages + p]],
                                      pages_buf.at[slot, sq, c, p], sem.at[slot])
                for sq in range(SAMPLE_SEQS) for c in range(N_CACHE) for p in range(n_pages)]

    @pl.when(b == 0)
    def _():
        for cp in page_copies(0, 0):
            cp.start()

    @pl.when(b + 1 < pl.num_programs(0))
    def _():
        for cp in page_copies(b + 1, jnp.bitwise_and(b + 1, 1)):
            cp.start()

    slot = jnp.bitwise_and(b, 1)
    for cp in page_copies(b, slot):
        cp.wait()
    per_seq = (0, 1, 2, 3, 4, 5, 6, 14, 15)
    stages = []
    for sq in range(SAMPLE_SEQS):
        ins = [r.at[sq] if k in per_seq else r for k, r in enumerate(blocked)]
        pages = [[pages_buf.at[slot, sq, c, p] for p in range(n_pages)] for c in range(N_CACHE)]
        stages.append(_nsa_sample_sequence(*ins, *pages, o_ref.at[sq], n_pages=n_pages, past=past, w_buf=w_buf))
    for _ in itertools.zip_longest(*stages):
        pass

    if len(outs) > 1:
        wko_ref, wvo_ref = outs[1:]
        shifts = [(layer, blocked[4], blocked[5], blocked[14], blocked[15])]
        shifts += [(lo,) + tuple(older[4 * lo:4 * lo + 4]) for lo in range(n_older)]
        keep = N_KV * (w_buf - 1)
        for lo, kn, vn, wk, wv in shifts:
            for sq in range(SAMPLE_SEQS):
                wko_ref[lo, sq, 0:keep, :] = wk[sq, N_KV:N_KV * w_buf, :]
                wvo_ref[lo, sq, 0:keep, :] = wv[sq, N_KV:N_KV * w_buf, :]
                for g in range(N_KV):
                    wko_ref[lo, sq, keep + g:keep + g + 1, :] = kn[sq, :, g * HEAD_DIM:(g + 1) * HEAD_DIM]
                    wvo_ref[lo, sq, keep + g:keep + g + 1, :] = vn[sq, :, g * HEAD_DIM:(g + 1) * HEAD_DIM]


def _nsa_sample(zs3, page_table, cmp_pages, slc_pages, win_k, win_v, cw, wsel, layer, dbs, past, w_buf, older_zs3):
    n_pages = past // PAGE_SIZE
    depth = win_k.shape[0]
    n_older = 0 if older_zs3 is None else len(older_zs3)
    assert older_zs3 is None or n_older == layer == depth - 1
    assert dbs % SAMPLE_SEQS == 0
    nsq = SAMPLE_SEQS
    rowblk = lambda width, off: pl.BlockSpec((nsq, 1, width), lambda b, pt: (b, 0, off // width))
    const3 = lambda shp: pl.BlockSpec((None,) + shp, lambda b, pt: (layer,) + (0,) * len(shp))
    page_shape = (PAGE_SIZE * N_KV, HEAD_DIM)
    win_shape = (w_buf * N_KV, HEAD_DIM)
    win = pl.BlockSpec((None, nsq) + win_shape, lambda b, pt: (layer, b, 0, 0))
    hbm = pl.BlockSpec(memory_space=pl.ANY)
    in_specs = [
        rowblk(HPG * HEAD_DIM, OFF_NQ), rowblk(HPG * HEAD_DIM, OFF_NQ + HPG * HEAD_DIM),
        rowblk(KV_W, OFF_KS), rowblk(KV_W, OFF_VS), rowblk(KV_W, OFF_KW), rowblk(KV_W, OFF_VW),
        rowblk(LANES, OFF_GL),
        pl.BlockSpec((LANES, LANES), lambda b, pt: (0, 0)),
        const3((CHUNK_W, 2 * HEAD_DIM)), const3((HEAD_DIM, HEAD_DIM)), const3((PE_ROWS, CHUNK_W)),
        const3((CHUNK_W, 2 * HEAD_DIM)), const3((HEAD_DIM, HEAD_DIM)), const3((PE_ROWS, CHUNK_W)),
        win, win,
    ] + [hbm] * N_CACHE
    operands = [zs3] * 7 + [wsel, cw["w1k"], cw["w2k"], cw["pek"], cw["w1v"], cw["w2v"], cw["pev"], win_k, win_v]
    operands += list(cmp_pages) + list(slc_pages)
    out_specs = [pl.BlockSpec((nsq, 1, NSA_W), lambda b, pt: (b, 0, 0))]
    out_shape = [jax.ShapeDtypeStruct((dbs, 1, NSA_W), F32)]
    if older_zs3 is not None:
        for lo in range(n_older):
            win_lo = pl.BlockSpec((None, nsq) + win_shape, lambda b, pt, lo=lo: (lo, b, 0, 0))
            operands += [older_zs3[lo], older_zs3[lo], win_k, win_v]
            in_specs += [rowblk(KV_W, OFF_KW), rowblk(KV_W, OFF_VW), win_lo, win_lo]
        out_specs += [pl.BlockSpec((depth, nsq) + win_shape, lambda b, pt: (0, b, 0, 0))] * 2
        out_shape += [jax.ShapeDtypeStruct((depth, dbs) + win_shape, F32)] * 2
    return pl.pallas_call(
        functools.partial(_nsa_sample_kernel, n_pages=n_pages, past=past, w_buf=w_buf, layer=layer,
                          n_older=n_older),
        grid_spec=pltpu.PrefetchScalarGridSpec(
            num_scalar_prefetch=1,
            grid=(dbs // nsq,),
            in_specs=in_specs,
            out_specs=out_specs,
            scratch_shapes=[pltpu.VMEM((2, nsq, N_CACHE, n_pages) + page_shape, F32),
                            pltpu.SemaphoreType.DMA((2,))],
        ),
        out_shape=out_shape,
        compiler_params=_cparams(("arbitrary",), VMEM_LIMIT_BIG),
        name="nsa_sample",
    )(page_table.reshape(-1), *operands)


def _rope_tables(pos):
    half = HEAD_DIM // 2
    inv = ROPE_THETA ** (-jnp.arange(half, dtype=F32) / half)
    ang = pos.astype(F32)[:, None] * inv[None, :]
    cos, sin = jnp.cos(ang), jnp.sin(ang)
    return jnp.concatenate([cos, cos], axis=-1), jnp.concatenate([-sin, sin], axis=-1)


def _block_sum_matrix():
    w = np.zeros((LANES, LANES), np.float32)
    for j in range(LANES):
        for mm in range(SEL_RATIO):
            for nn in range(CMP_RATIO):
                c = SEL_RATIO * j + mm - nn
                if 0 <= c < LANES:
                    w[c, j] += 1.0
    return w


def _row_tile(m, cap):
    tm = min(m, cap)
    while m % tm:
        tm //= 2
    return tm


def kernel(x_prompt, x_sample, state_ret, state_pool, cache_cmp_k, cache_cmp_v, cache_slc_k, cache_slc_v,
           cache_win_k, cache_win_v, page_table, norm1, w_in, w_pool, pool_scale, cmp_w1_k, cmp_w2_k, cmp_pe_k,
           cmp_w1_v, cmp_w2_v, cmp_pe_v, w_out, norm2, w_up, w_down, norm_f):
    bsz, t, _ = x_prompt.shape
    dbs, dec_seq, _ = x_sample.shape
    depth = w_in.shape[0]
    assert dec_seq == 1 and t % 512 == 0 and dbs % SAMPLE_TB == 0
    n_pages = page_table.shape[1]
    past = n_pages * PAGE_SIZE
    n_pool = cache_cmp_k.shape[1]
    w_buf = cache_win_k.shape[2]
    wlen = min(WINDOW, t)

    w_in_bf = jnp.pad(w_in, ((0, 0), (0, 0), (0, N_PAD - D_IN))).astype(BF16)
    w_out_bf = w_out.astype(BF16)
    w1cat = lambda w1: jnp.concatenate([w1[:, :CMP_STRIDE], w1[:, CMP_STRIDE:]], axis=-1).astype(BF16).reshape(
        depth, CHUNK_W, 2 * HEAD_DIM)
    pe_rows = lambda pe: jnp.pad(pe.reshape(depth, CMP_RATIO, CHUNK_W),
                                 ((0, 0), (0, PE_ROWS - CMP_RATIO), (0, 0))).astype(BF16)
    cw = dict(w1k=w1cat(cmp_w1_k), w2k=cmp_w2_k.astype(BF16), pek=pe_rows(cmp_pe_k),
              w1v=w1cat(cmp_w1_v), w2v=cmp_w2_v.astype(BF16), pev=pe_rows(cmp_pe_v))
    norm1_3, norm2_3 = norm1[:, None, :], norm2[:, None, :]
    pool_scale_3 = pool_scale[:, None, :]
    norm_f2 = norm_f[None, :]
    log_g = jnp.log1p(-jnp.exp2(-5.0 - jnp.arange(R_HEADS, dtype=F32)))
    cos_p, sin_p = _rope_tables(jnp.arange(t))
    cos_s, sin_s = _rope_tables(jnp.full((dbs,), past, jnp.int32))
    wsel = jnp.asarray(_block_sum_matrix(), BF16)
    page_view = lambda c: c.reshape(depth, n_pool, PAGE_SIZE * N_KV, HEAD_DIM)
    cmp_pages = [page_view(c) for c in (cache_cmp_k, cache_cmp_v)]
    slc_pages = [page_view(c) for c in (cache_slc_k, cache_slc_v)]
    win_k = cache_win_k.reshape(depth, dbs, w_buf * N_KV, HEAD_DIM)
    win_v = cache_win_v.reshape(depth, dbs, w_buf * N_KV, HEAD_DIM)

    mp = bsz * t
    tm_p = _row_tile(t, 1024)
    tm_mlp = _row_tile(t, 1024)
    xp = x_prompt.reshape(mp, D_MODEL)
    xs = x_sample.reshape(dbs, D_MODEL)
    outs = {k: [] for k in ("ret_p", "ret_s", "pool_p", "pool_s", "ck_p", "ck_s", "cv_p", "cv_s", "sk_p", "sk_s",
                            "sv_p", "sv_s", "wk_p", "wv_p")}
    zs3_all = []
    for l in range(depth):
        last = l == depth - 1
        z = _inproj(xp, norm1_3, w_in_bf, l, cos_p, sin_p, tm_p)
        o_r, s_new = _ret_prompt(z, log_g, bsz, t)
        o_p = _pool_prompt(z, w_pool, pool_scale_3, l, bsz, t)
        kcs, vcst, vst, vwt = _prep_prompt(z, cw, l, bsz, t)
        o_n = _nsa_prompt(z, kcs, vcst, vst, vwt, bsz, t)
        xp = _outproj(o_r, o_p, o_n, w_out_bf, l, xp, _row_tile(t, 512))
        xp = _mlp(xp, norm2_3, w_up, w_down, l, norm_f2, last, tm_mlp)
        z3 = z.reshape(bsz, t, N_PAD)
        kv = lambda off: z3[:, :, off:off + KV_W].reshape(bsz, t, N_KV, HEAD_DIM)
        outs["ret_p"].append(s_new)
        outs["pool_p"].append(z3[:, t - POOL_BUF:, OFF_PU:OFF_PU + POOL_W])
        outs["ck_p"].append(kv(OFF_KC))
        outs["cv_p"].append(kv(OFF_VC))
        outs["sk_p"].append(kv(OFF_KS))
        outs["sv_p"].append(kv(OFF_VS))
        outs["wk_p"].append(kv(OFF_KW)[:, t - wlen:])
        outs["wv_p"].append(kv(OFF_VW)[:, t - wlen:])
        zs = _inproj(xs, norm1_3, w_in_bf, l, cos_s, sin_s, dbs)
        o_r, s_new, o_p, buf_new = _state_sample(zs, log_g, state_ret, state_pool, w_pool, pool_scale_3, l, dbs, past)
        zs3 = zs.reshape(dbs, 1, N_PAD)
        res = _nsa_sample(zs3, page_table, cmp_pages, slc_pages, win_k, win_v, cw, wsel, l, dbs, past, w_buf,
                          zs3_all if last else None)
        o_n = res[0]
        zs3_all.append(zs3)
        if last:
            wk_s, wv_s = res[1:]
        xs = _outproj(o_r, o_p, o_n.reshape(dbs, NSA_W), w_out_bf, l, xs, dbs)
        xs = _mlp(xs, norm2_3, w_up, w_down, l, norm_f2, last, dbs)
        kvs = lambda off: zs[:, off:off + KV_W].reshape(dbs, 1, N_KV, HEAD_DIM)
        outs["ret_s"].append(s_new)
        outs["pool_s"].append(buf_new)
        outs["ck_s"].append(kvs(OFF_KC))
        outs["cv_s"].append(kvs(OFF_VC))
        outs["sk_s"].append(kvs(OFF_KS))
        outs["sv_s"].append(kvs(OFF_VS))
    st = lambda k: jnp.stack(outs[k])
    win_out = lambda w: w.reshape(depth, dbs, w_buf, N_KV, HEAD_DIM)
    return (xp.reshape(bsz, t, D_MODEL), xs.reshape(dbs, 1, D_MODEL),
            st("ret_p"), st("ret_s"), st("pool_p"), st("pool_s"),
            st("ck_p"), st("ck_s"), st("cv_p"), st("cv_s"), st("sk_p"), st("sk_s"), st("sv_p"), st("sv_s"),
            st("wk_p"), win_out(wk_s), st("wv_p"), win_out(wv_s))
```

```python
import functools
import itertools

import numpy as np
import jax
import jax.numpy as jnp
from jax import lax
from jax.experimental import pallas as pl
from jax.experimental.pallas import tpu as pltpu

F32 = jnp.float32
BF16 = jnp.bfloat16

D_MODEL = 2048
HEAD_DIM = 128
PAGE_SIZE = 128
RET_W = D_MODEL // 4
POOL_W = D_MODEL // 4
NSA_W = D_MODEL - RET_W - POOL_W
R_HEADS = RET_W // HEAD_DIM
N_HEADS = NSA_W // HEAD_DIM
N_KV = max(1, N_HEADS // 4)
HPG = N_HEADS // N_KV
KV_W = N_KV * HEAD_DIM
POOL_WINDOWS = (2, 4, 8, 16)
POOL_GW = POOL_W // len(POOL_WINDOWS)
POOL_BUF = max(POOL_WINDOWS) - 1
RET_CHUNK = 128
CMP_LEN = 32
CMP_STRIDE = 16
CMP_RATIO = CMP_LEN // CMP_STRIDE
SEL_LEN = 64
SEL_RATIO = SEL_LEN // CMP_STRIDE
SEL_TOPK = 16
N_LOCAL_SEL = 2
WINDOW = 512
Q_BLOCK = 128
D_FF = 4 * D_MODEL
ROPE_THETA = 10000.0
EPS = 1e-6
D_IN = 4 * RET_W + POOL_W + NSA_W + 6 * KV_W + 3 * N_HEADS

OFF_RQ = 0
OFF_RK = OFF_RQ + RET_W
OFF_RV = OFF_RK + RET_W
OFF_RG = OFF_RV + RET_W
OFF_PU = OFF_RG + RET_W
OFF_NQ = OFF_PU + POOL_W
OFF_KC = OFF_NQ + NSA_W
OFF_VC = OFF_KC + KV_W
OFF_KS = OFF_VC + KV_W
OFF_VS = OFF_KS + KV_W
OFF_KW = OFF_VS + KV_W
OFF_VW = OFF_KW + KV_W
OFF_GL = OFF_VW + KV_W
PROJ_TN = 512
PROJ_ROW_CHUNK = 256
N_PAD = -(-D_IN // PROJ_TN) * PROJ_TN
ROPE_FULL_TILES = (OFF_RQ // PROJ_TN, OFF_RK // PROJ_TN, OFF_NQ // PROJ_TN, OFF_NQ // PROJ_TN + 1)
ROPE_HALF_TILES = (OFF_KC // PROJ_TN, OFF_KS // PROJ_TN, OFF_KW // PROJ_TN)
RK_TILE = OFF_RK // PROJ_TN

NEG = -1e30
V7X_VMEM_BYTES = 64 * 1024 * 1024
VMEM_LIMIT = V7X_VMEM_BYTES * 3 // 4
VMEM_LIMIT_BIG = V7X_VMEM_BYTES * 31 // 32
LANES = 128
SUBLANES = 8


def _cparams(sem, vmem_limit=VMEM_LIMIT):
    return pltpu.CompilerParams(dimension_semantics=sem, vmem_limit_bytes=vmem_limit)


def _dot(a, b):
    return jnp.dot(a, b, preferred_element_type=F32)


def _dot_nt(a, b):
    return lax.dot_general(a, b, (((1,), (1,)), ((), ())), preferred_element_type=F32)


def _dot_tn(a, b):
    return lax.dot_general(a, b, (((0,), (0,)), ((), ())), preferred_element_type=F32)


def _rms(x, g):
    return x * lax.rsqrt(jnp.mean(x * x, axis=-1, keepdims=True) + EPS) * g


def _silu(x):
    return x * jax.nn.sigmoid(x)


def _inproj_kernel(x_ref, g_ref, w_ref, cos_ref, sin_ref, z_ref, hn_ref):
    j = pl.program_id(1)

    @pl.when(j == 0)
    def _():
        hn_ref[...] = _rms(x_ref[...], g_ref[...]).astype(BF16)

    is_full = functools.reduce(jnp.logical_or, [j == t for t in ROPE_FULL_TILES])
    is_half = functools.reduce(jnp.logical_or, [j == t for t in ROPE_HALF_TILES])
    scale = jnp.where(j == RK_TILE, HEAD_DIM ** -0.5, 1.0).astype(F32)
    is_any = jnp.logical_or(is_full, is_half)
    tm = x_ref.shape[0]
    rc = min(tm, PROJ_ROW_CHUNK)
    for r0 in range(0, tm, rc):
        rows = slice(r0, r0 + rc)
        z = _dot(hn_ref[rows, :], w_ref[...])
        cos_t, sin_t = cos_ref[rows, :] * scale, sin_ref[rows, :] * scale
        tab_lead = (jnp.where(is_any, cos_t, 1.0), jnp.where(is_any, sin_t, 0.0))
        tab_rest = (jnp.where(is_full, cos_t, 1.0), jnp.where(is_full, sin_t, 0.0))
        for s in range(PROJ_TN // HEAD_DIM):
            sl = slice(s * HEAD_DIM, (s + 1) * HEAD_DIM)
            c, sn = tab_lead if s < KV_W // HEAD_DIM else tab_rest
            zs = z[:, sl]
            z_ref[rows, sl] = zs * c + pltpu.roll(zs, HEAD_DIM // 2, 1) * sn


def _inproj(x, gamma, w_in_bf, layer, cos_t, sin_t, tm):
    m = x.shape[0]
    n_row_tiles = cos_t.shape[0] // tm
    return pl.pallas_call(
        _inproj_kernel,
        grid=(m // tm, N_PAD // PROJ_TN),
        in_specs=[
            pl.BlockSpec((tm, D_MODEL), lambda i, j: (i, 0)),
            pl.BlockSpec((None, 1, D_MODEL), lambda i, j: (layer, 0, 0)),
            pl.BlockSpec((None, D_MODEL, PROJ_TN), lambda i, j: (layer, 0, j)),
            pl.BlockSpec((tm, HEAD_DIM), lambda i, j: (i % n_row_tiles, 0)),
            pl.BlockSpec((tm, HEAD_DIM), lambda i, j: (i % n_row_tiles, 0)),
        ],
        out_specs=pl.BlockSpec((tm, PROJ_TN), lambda i, j: (i, j)),
        out_shape=jax.ShapeDtypeStruct((m, N_PAD), F32),
        scratch_shapes=[pltpu.VMEM((tm, D_MODEL), BF16)],
        compiler_params=_cparams(("parallel", "arbitrary")),
        name="inproj",
    )(x, gamma, w_in_bf, cos_t, sin_t)


def _outproj_kernel(r_ref, p_ref, n_ref, w_ref, x_ref, o_ref):
    acc = _dot(r_ref[...].astype(BF16), w_ref[0:RET_W, :])
    acc += _dot(p_ref[...].astype(BF16), w_ref[RET_W:RET_W + POOL_W, :])
    acc += _dot(n_ref[...].astype(BF16), w_ref[RET_W + POOL_W:, :])
    o_ref[...] = x_ref[...] + acc


def _outproj(o_r, o_p, o_n, w_out_bf, layer, x, tm, tn=D_MODEL):
    m = x.shape[0]
    return pl.pallas_call(
        _outproj_kernel,
        grid=(m // tm, D_MODEL // tn),
        in_specs=[
            pl.BlockSpec((tm, RET_W), lambda i, j: (i, 0)),
            pl.BlockSpec((tm, POOL_W), lambda i, j: (i, 0)),
            pl.BlockSpec((tm, NSA_W), lambda i, j: (i, 0)),
            pl.BlockSpec((None, D_MODEL, tn), lambda i, j: (layer, 0, j)),
            pl.BlockSpec((tm, tn), lambda i, j: (i, j)),
        ],
        out_specs=pl.BlockSpec((tm, tn), lambda i, j: (i, j)),
        out_shape=jax.ShapeDtypeStruct((m, D_MODEL), F32),
        compiler_params=_cparams(("parallel", "parallel")),
        name="outproj",
    )(o_r, o_p, o_n, w_out_bf, x)


def _mlp_kernel(x_ref, g_ref, wu_ref, wd_ref, gf_ref, o_ref, hn_ref, *, final_norm):
    f = pl.program_id(1)

    @pl.when(f == 0)
    def _():
        hn_ref[...] = _rms(x_ref[...], g_ref[...]).astype(BF16)
        o_ref[...] = jnp.zeros_like(o_ref)

    u = jnp.maximum(_dot(hn_ref[...], wu_ref[...].astype(BF16)), 0.0)
    o_ref[...] += _dot((u * u).astype(BF16), wd_ref[...].astype(BF16))

    @pl.when(f == pl.num_programs(1) - 1)
    def _():
        y = x_ref[...] + o_ref[...]
        o_ref[...] = _rms(y, gf_ref[...]) if final_norm else y


def _mlp(x, gamma, w_up, w_down, layer, gamma_f, final_norm, tm, tf=512):
    m = x.shape[0]
    return pl.pallas_call(
        functools.partial(_mlp_kernel, final_norm=final_norm),
        grid=(m // tm, D_FF // tf),
        in_specs=[
            pl.BlockSpec((tm, D_MODEL), lambda i, f: (i, 0)),
            pl.BlockSpec((None, 1, D_MODEL), lambda i, f: (layer, 0, 0)),
            pl.BlockSpec((None, D_MODEL, tf), lambda i, f: (layer, 0, f)),
            pl.BlockSpec((None, tf, D_MODEL), lambda i, f: (layer, f, 0)),
            pl.BlockSpec((1, D_MODEL), lambda i, f: (0, 0)),
        ],
        out_specs=pl.BlockSpec((tm, D_MODEL), lambda i, f: (i, 0)),
        out_shape=jax.ShapeDtypeStruct((m, D_MODEL), F32),
        scratch_shapes=[pltpu.VMEM((tm, D_MODEL), BF16)],
        compiler_params=_cparams(("parallel", "arbitrary"), VMEM_LIMIT_BIG),
        name="mlp",
    )(x, gamma, w_up, w_down, gamma_f)


def _ret_prompt_kernel(lg_ref, q_ref, k_ref, v_ref, g_ref, o_ref, s_ref, *, n_chunk):
    ti = pl.program_id(1)
    L = RET_CHUNK
    ri = lax.broadcasted_iota(jnp.int32, (L, L), 0)
    ci = lax.broadcasted_iota(jnp.int32, (L, L), 1)
    diff = (ri - ci).astype(F32)
    causal = diff >= 0
    idx = lax.broadcasted_iota(jnp.int32, (L, 1), 0).astype(F32)
    consts = []
    for h in range(R_HEADS):
        lg = lg_ref[h]
        decay = jnp.where(causal, jnp.exp(jnp.where(causal, diff, 0.0) * lg), 0.0)
        q_dec = jnp.exp((idx + 1.0) * lg)
        k_dec = jnp.exp((L - 1.0 - idx) * lg)
        s_dec = jnp.exp(jnp.full((1, 1), float(L), F32) * lg)
        consts.append((decay, q_dec, k_dec, s_dec))

    @pl.when(ti == 0)
    def _():
        s_ref[...] = jnp.zeros_like(s_ref)

    def body(c, states):
        rows = pl.ds(pl.multiple_of(c * L, L), L)
        new_states = []
        for h, (decay, q_dec, k_dec, s_dec) in enumerate(consts):
            sl = slice(h * HEAD_DIM, (h + 1) * HEAD_DIM)
            q, k, v = q_ref[rows, sl], k_ref[rows, sl], v_ref[rows, sl]
            vb = v.astype(BF16)
            att = _dot_nt(q.astype(BF16), k.astype(BF16)) * decay
            o = _dot(att.astype(BF16), vb) + _dot((q * q_dec).astype(BF16), states[h].astype(BF16))
            new_states.append(s_dec * states[h] + _dot_tn((k * k_dec).astype(BF16), vb))
            o = o * lax.rsqrt(jnp.mean(o * o, axis=-1, keepdims=True) + EPS)
            o_ref[rows, sl] = o * _silu(g_ref[rows, sl])
        return tuple(new_states)

    states = lax.fori_loop(0, n_chunk, body, tuple(s_ref[h] for h in range(R_HEADS)))
    for h in range(R_HEADS):
        s_ref[h] = states[h]


def _ret_prompt(z, log_g, bsz, t, tt=1024):
    tt = min(tt, t)
    nt = t // tt
    blk = lambda off: pl.BlockSpec((tt, RET_W), lambda b, i, lg: (b * nt + i, off // RET_W))
    return pl.pallas_call(
        functools.partial(_ret_prompt_kernel, n_chunk=tt // RET_CHUNK),
        grid_spec=pltpu.PrefetchScalarGridSpec(
            num_scalar_prefetch=1,
            grid=(bsz, nt),
            in_specs=[blk(OFF_RQ), blk(OFF_RK), blk(OFF_RV), blk(OFF_RG)],
            out_specs=[
                pl.BlockSpec((tt, RET_W), lambda b, i, lg: (b * nt + i, 0)),
                pl.BlockSpec((None, R_HEADS, HEAD_DIM, HEAD_DIM), lambda b, i, lg: (b, 0, 0, 0)),
            ],
        ),
        out_shape=[
            jax.ShapeDtypeStruct((bsz * t, RET_W), F32),
            jax.ShapeDtypeStruct((bsz, R_HEADS, HEAD_DIM, HEAD_DIM), F32),
        ],
        compiler_params=_cparams(("parallel", "arbitrary")),
        name="ret_prompt",
    )(log_g, z, z, z, z)


POOL_HALO = 16


def _pool_prompt_kernel(u_ref, wp_ref, ps_ref, o_ref, ext_ref, *, tt):
    t = pl.program_id(1)

    @pl.when(t == 0)
    def _():
        ext_ref[0:POOL_HALO, :] = jnp.zeros((POOL_HALO, POOL_W), F32)

    u = u_ref[...]
    ext_ref[POOL_HALO:POOL_HALO + tt, :] = u
    pos = t * tt + lax.broadcasted_iota(jnp.int32, (tt, 1), 0)
    for g, w in enumerate(POOL_WINDOWS):
        sl = slice(g * POOL_GW, (g + 1) * POOL_GW)
        s = u[:, sl]
        for i in range(1, w):
            s = s + ext_ref[pl.ds(POOL_HALO - i, tt), sl]
        cnt = jnp.minimum(pos + 1, w).astype(F32)
        p = s / cnt - u[:, sl]
        o_ref[:, sl] = _dot(p.astype(BF16), wp_ref[g].astype(BF16)) * ps_ref[:, sl]
    ext_ref[0:POOL_HALO, :] = ext_ref[tt:tt + POOL_HALO, :]


def _pool_prompt(z, w_pool, pool_scale, layer, bsz, t, tt=512):
    nt = t // tt
    return pl.pallas_call(
        functools.partial(_pool_prompt_kernel, tt=tt),
        grid=(bsz, nt),
        in_specs=[
            pl.BlockSpec((tt, POOL_W), lambda b, i: (b * nt + i, OFF_PU // POOL_W)),
            pl.BlockSpec((None, len(POOL_WINDOWS), POOL_GW, POOL_GW), lambda b, i: (layer, 0, 0, 0)),
            pl.BlockSpec((None, 1, POOL_W), lambda b, i: (layer, 0, 0)),
        ],
        out_specs=pl.BlockSpec((tt, POOL_W), lambda b, i: (b * nt + i, 0)),
        out_shape=jax.ShapeDtypeStruct((bsz * t, POOL_W), F32),
        scratch_shapes=[pltpu.VMEM((POOL_HALO + tt, POOL_W), F32)],
        compiler_params=_cparams(("parallel", "arbitrary")),
        name="pool_prompt",
    )(z, w_pool, pool_scale)


CHUNK_W = CMP_STRIDE * HEAD_DIM
PE_ROWS = 16


def _compress_core(chunks, w1_ref, w2_ref, pe_ref):
    n_rows = chunks.shape[0]
    acc = _dot(jnp.concatenate([chunks, pe_ref[...]], axis=0), w1_ref[...])
    first = acc[:n_rows, :HEAD_DIM]
    second = acc[:n_rows, HEAD_DIM:]
    pe_term = acc[n_rows:n_rows + 1, :HEAD_DIM] + acc[n_rows + 1:n_rows + 2, HEAD_DIM:]
    h = first + pltpu.roll(second, n_rows - 1, 0) + pe_term
    return _dot(_silu(h).astype(BF16), w2_ref[...])


def _prep_prompt_kernel(kc_ref, vc_ref, vs_ref, vw_ref, w1k_ref, w2k_ref, pek_ref, w1v_ref, w2v_ref, pev_ref,
                        kcs_ref, vcst_ref, vst_ref, vwt_ref, *, n_chunk, n_tile):
    chunks_of = lambda r: jnp.concatenate(
        [r[pl.ds(s, n_chunk, stride=CMP_STRIDE), :].astype(BF16) for s in range(CMP_STRIDE)], axis=1)
    kcs_ref[...] = _compress_core(chunks_of(kc_ref), w1k_ref, w2k_ref, pek_ref).astype(BF16)
    vcst_ref[...] = _compress_core(chunks_of(vc_ref), w1v_ref, w2v_ref, pev_ref).T.astype(BF16)

    def tbody(i, carry):
        rows = pl.ds(pl.multiple_of(i * HEAD_DIM, HEAD_DIM), HEAD_DIM)
        vst_ref[i] = vs_ref[rows, :].T.astype(BF16)
        vwt_ref[i] = vw_ref[rows, :].T.astype(BF16)
        return carry

    lax.fori_loop(0, n_tile, tbody, 0)


def _prep_prompt(z, cw, layer, bsz, t):
    n_chunk = t // CMP_STRIDE
    n_tile = t // HEAD_DIM
    col = lambda off: pl.BlockSpec((t, HEAD_DIM), lambda b, g: (b, off // HEAD_DIM + g))
    w1 = pl.BlockSpec((None, CHUNK_W, 2 * HEAD_DIM), lambda b, g: (layer, 0, 0))
    w2 = pl.BlockSpec((None, HEAD_DIM, HEAD_DIM), lambda b, g: (layer, 0, 0))
    pe = pl.BlockSpec((None, PE_ROWS, CHUNK_W), lambda b, g: (layer, 0, 0))
    return pl.pallas_call(
        functools.partial(_prep_prompt_kernel, n_chunk=n_chunk, n_tile=n_tile),
        grid=(bsz, N_KV),
        in_specs=[col(OFF_KC), col(OFF_VC), col(OFF_VS), col(OFF_VW), w1, w2, pe, w1, w2, pe],
        out_specs=[
            pl.BlockSpec((None, None, n_chunk, HEAD_DIM), lambda b, g: (b, g, 0, 0)),
            pl.BlockSpec((None, None, HEAD_DIM, n_chunk), lambda b, g: (b, g, 0, 0)),
            pl.BlockSpec((None, None, n_tile, HEAD_DIM, HEAD_DIM), lambda b, g: (b, g, 0, 0, 0)),
            pl.BlockSpec((None, None, n_tile, HEAD_DIM, HEAD_DIM), lambda b, g: (b, g, 0, 0, 0)),
        ],
        out_shape=[
            jax.ShapeDtypeStruct((bsz, N_KV, n_chunk, HEAD_DIM), BF16),
            jax.ShapeDtypeStruct((bsz, N_KV, HEAD_DIM, n_chunk), BF16),
            jax.ShapeDtypeStruct((bsz, N_KV, n_tile, HEAD_DIM, HEAD_DIM), BF16),
            jax.ShapeDtypeStruct((bsz, N_KV, n_tile, HEAD_DIM, HEAD_DIM), BF16),
        ],
        compiler_params=_cparams(("parallel", "parallel")),
        name="prep_prompt",
    )(z, z, z, z, cw["w1k"], cw["w2k"], cw["pek"], cw["w1v"], cw["w2v"], cw["pev"])


LOG2E = 1.4426950408889634


def _softmax_tiles(tiles, carry):
    m, l, acc = carry
    m_new = m
    for s, _ in tiles:
        m_new = jnp.maximum(m_new, jnp.max(s, axis=0, keepdims=True))
    alpha = jnp.exp2(m - m_new)
    l = alpha * l
    acc = alpha * acc
    for s, vt in tiles:
        p = jnp.exp2(s - m_new)
        l = l + jnp.sum(p, axis=0, keepdims=True)
        acc = acc + _dot(vt, p.astype(BF16))
    return m_new, l, acc


def _select_blocks(score, score_ref):
    n_blocks = score.shape[0]
    score_ref[...] = score
    groups = [score[v * SUBLANES:(v + 1) * SUBLANES] for v in range(n_blocks // SUBLANES)]
    jb = lax.broadcasted_iota(jnp.int32, groups[0].shape, 0)
    ranks = [jnp.zeros(gv.shape, F32) for gv in groups]
    for k in range(n_blocks):
        rk = score_ref[pl.ds(k, 1), :]
        for v, gv in enumerate(groups):
            if v * SUBLANES > k:
                beats = rk >= gv
            elif (v + 1) * SUBLANES - 1 <= k:
                beats = rk > gv
            else:
                beats = jnp.logical_or(rk > gv, jnp.logical_and(jb + v * SUBLANES > k, rk == gv))
            ranks[v] = ranks[v] + jnp.where(beats, 1.0, 0.0)
    rank = jnp.concatenate(ranks, axis=0)
    return jnp.logical_and(rank < SEL_TOPK, score > -jnp.inf)


def _nsa_prompt_kernel(q0_ref, q1_ref, gl_ref, kcs_ref, vcst_ref, ks_ref, vst_ref, kw_ref, vwt_ref, o_ref,
                       imp_ref, selb_ref, gt_ref, score_ref, *, n_chunk, n_sel):
    i = pl.program_id(1)
    groups = range(N_KV)
    qw = HPG * Q_BLOCK
    qscale = HEAD_DIM ** -0.5 * LOG2E
    lane = lax.broadcasted_iota(jnp.int32, (1, qw), 1)
    tl4 = jnp.bitwise_and(lane, Q_BLOCK - 1)
    tpos4 = i * Q_BLOCK + tl4
    tpos = i * Q_BLOCK + lax.broadcasted_iota(jnp.int32, (1, Q_BLOCK), 1)
    kv_lanes = lambda g: slice(g * HEAD_DIM, (g + 1) * HEAD_DIM)

    def q_transposed(g):
        q = (q0_ref, q1_ref)[g][...]
        return jnp.concatenate([(q[:, h * HEAD_DIM:(h + 1) * HEAD_DIM] * qscale).T for h in range(HPG)],
                               axis=1).astype(BF16)

    qt = [q_transposed(g) for g in groups]

    c_end = lax.broadcasted_iota(jnp.int32, (n_chunk, 1), 0) * CMP_STRIDE + (CMP_LEN - 1)
    mask_c = jnp.logical_and(c_end <= tpos4, c_end < n_chunk * CMP_STRIDE)

    def compressed(g):
        sc = _dot(kcs_ref[g], qt[g])
        m = jnp.max(jnp.where(mask_c, sc, NEG), axis=0, keepdims=True)
        e = jnp.where(mask_c, jnp.exp2(sc - m), 0.0)
        d = jnp.sum(e, axis=0, keepdims=True)
        pc = e * (1.0 / jnp.where(d > 0, d, 1.0))
        imp = pc[:, 0:Q_BLOCK]
        for h in range(1, HPG):
            imp = imp + pc[:, h * Q_BLOCK:(h + 1) * Q_BLOCK]
        return _dot(vcst_ref[g], pc.astype(BF16)), imp

    oc, imps = zip(*[compressed(g) for g in groups])

    jb = lax.broadcasted_iota(jnp.int32, (n_sel, 1), 0)
    blk_valid = jb * SEL_LEN <= tpos
    cur = lax.shift_right_logical(tpos, int(np.log2(SEL_LEN)))
    forced = jnp.logical_or(jb == 0, jnp.logical_and(jb <= cur, jb > cur - N_LOCAL_SEL))

    def select(g):
        imp_ref[g, 0:SUBLANES, :] = jnp.zeros((SUBLANES, Q_BLOCK), F32)
        imp_ref[g, SUBLANES:SUBLANES + n_chunk, :] = imps[g]
        imp_ref[g, SUBLANES + n_chunk:, :] = jnp.zeros((SUBLANES, Q_BLOCK), F32)
        p_slc = None
        for mm in range(SEL_RATIO):
            for nn in range(CMP_RATIO):
                r = imp_ref[g, pl.ds(SUBLANES + mm - nn, n_sel, stride=SEL_RATIO), :]
                p_slc = r if p_slc is None else p_slc + r
        score = jnp.where(blk_valid, jnp.where(forced, jnp.inf, p_slc), -jnp.inf)
        sel = _select_blocks(score, score_ref.at[g])
        selb_ref[g] = jnp.concatenate([jnp.where(sel, 0.0, NEG)] * HPG, axis=1)

    for g in groups:
        select(g)

    row = lax.broadcasted_iota(jnp.int32, (Q_BLOCK, 1), 0)
    causal_bias = jnp.where(row <= tl4, 0.0, NEG)
    init = (jnp.full((1, qw), NEG, F32), jnp.zeros((1, qw), F32), jnp.zeros((HEAD_DIM, qw), F32))
    blocks_per_tile = Q_BLOCK // SEL_LEN

    n_back = WINDOW // Q_BLOCK

    def window(g):
        win_tiles = []
        for w in range(n_back + 1):
            kb = i - n_back + w
            kbc = jnp.maximum(kb, 0)
            rows = pl.ds(pl.multiple_of(kbc * Q_BLOCK, Q_BLOCK), Q_BLOCK)
            s = _dot(kw_ref[rows, kv_lanes(g)].astype(BF16), qt[g])
            if w == n_back:
                s = s + causal_bias
            else:
                before_start = jnp.where(kb >= 0, 0.0, NEG).astype(F32)
                s = s + (jnp.where(row > tl4, before_start, NEG) if w == 0 else before_start)
            win_tiles.append((s, vwt_ref[g, kbc]))
        return _softmax_tiles(win_tiles, init)[1:]

    l_w, a_w = zip(*[window(g) for g in groups])

    n_pair = lax.shift_right_logical(i, 1)
    odd_bias = jnp.where(jnp.bitwise_and(i, 1) == 1, 0.0, NEG).astype(F32)

    def sel_scores(g, kt, tile_bias):
        rows = pl.ds(pl.multiple_of(kt * Q_BLOCK, Q_BLOCK), Q_BLOCK)
        s = _dot(ks_ref[rows, kv_lanes(g)].astype(BF16), qt[g])
        parts = []
        for bb in range(blocks_per_tile):
            bias = selb_ref[g, pl.ds(kt * blocks_per_tile + bb, 1), :] + tile_bias
            parts.append(s[bb * SEL_LEN:(bb + 1) * SEL_LEN] + bias)
        return jnp.concatenate(parts, axis=0)

    def sel_body(u, carry):
        scores = [(sel_scores(g, 2 * u, 0.0), sel_scores(g, 2 * u + 1, 0.0)) for g in groups]
        return tuple(_softmax_tiles([(scores[g][0], vst_ref[g, 2 * u]), (scores[g][1], vst_ref[g, 2 * u + 1])],
                                    carry[g]) for g in groups)

    carry = lax.fori_loop(0, n_pair, sel_body, tuple(init for g in groups))
    prev = jnp.maximum(i - 1, 0)
    last_scores = [(sel_scores(g, prev, odd_bias), sel_scores(g, i, 0.0) + causal_bias) for g in groups]

    gt_ref[...] = jax.nn.sigmoid(gl_ref[...]).T
    for g in groups:
        _, l_s, a_s = _softmax_tiles([(last_scores[g][0], vst_ref[g, prev]), (last_scores[g][1], vst_ref[g, i])],
                                     carry[g])
        gate = lambda c: jnp.concatenate(
            [gt_ref[pl.ds((g * HPG + h) * 3 + c, 1), :] for h in range(HPG)], axis=1)
        ot = gate(0) * oc[g] + gate(1) * (a_s * (1.0 / l_s)) + gate(2) * (a_w[g] * (1.0 / l_w[g]))
        for h in range(HPG):
            col = (g * HPG + h) * HEAD_DIM
            o_ref[:, col:col + HEAD_DIM] = ot[:, h * Q_BLOCK:(h + 1) * Q_BLOCK].T


def _nsa_prompt(z, kcs, vcst, vst, vwt, bsz, t):
    n_chunk = t // CMP_STRIDE
    n_sel = t // SEL_LEN
    n_tile = t // Q_BLOCK
    nqb = t // Q_BLOCK
    qw = HPG * HEAD_DIM
    return pl.pallas_call(
        functools.partial(_nsa_prompt_kernel, n_chunk=n_chunk, n_sel=n_sel),
        grid=(bsz, nqb),
        in_specs=[
            pl.BlockSpec((Q_BLOCK, qw), lambda b, i: (b * nqb + i, OFF_NQ // qw)),
            pl.BlockSpec((Q_BLOCK, qw), lambda b, i: (b * nqb + i, OFF_NQ // qw + 1)),
            pl.BlockSpec((Q_BLOCK, LANES), lambda b, i: (b * nqb + i, OFF_GL // LANES)),
            pl.BlockSpec((None, N_KV, n_chunk, HEAD_DIM), lambda b, i: (b, 0, 0, 0)),
            pl.BlockSpec((None, N_KV, HEAD_DIM, n_chunk), lambda b, i: (b, 0, 0, 0)),
            pl.BlockSpec((t, KV_W), lambda b, i: (b, OFF_KS // KV_W)),
            pl.BlockSpec((None, N_KV, n_tile, HEAD_DIM, HEAD_DIM), lambda b, i: (b, 0, 0, 0, 0)),
            pl.BlockSpec((t, KV_W), lambda b, i: (b, OFF_KW // KV_W)),
            pl.BlockSpec((None, N_KV, n_tile, HEAD_DIM, HEAD_DIM), lambda b, i: (b, 0, 0, 0, 0)),
        ],
        out_specs=pl.BlockSpec((Q_BLOCK, NSA_W), lambda b, i: (b * nqb + i, 0)),
        out_shape=jax.ShapeDtypeStruct((bsz * t, NSA_W), F32),
        scratch_shapes=[
            pltpu.VMEM((N_KV, n_chunk + 2 * SUBLANES, Q_BLOCK), F32),
            pltpu.VMEM((N_KV, n_sel, HPG * Q_BLOCK), F32),
            pltpu.VMEM((LANES, Q_BLOCK), F32),
            pltpu.VMEM((N_KV, n_sel, Q_BLOCK), F32),
        ],
        compiler_params=_cparams(("parallel", "arbitrary")),
        name="nsa_prompt",
    )(z, z, z, kcs, vcst, z, vst, z, vwt)


SAMPLE_TB = 8


def _state_sample_kernel(lg_ref, q_ref, k_ref, v_ref, g_ref, u_ref, sr_ref, sp_ref, wp_ref, ps_ref,
                         or_ref, sn_ref, op_ref, pn_ref, *, pos):
    tb = SAMPLE_TB
    pad = jnp.zeros((SUBLANES - 1, HEAD_DIM), F32)
    for h in range(R_HEADS):
        gamma = jnp.exp(jnp.full((1, HEAD_DIM), lg_ref[h], F32))
        sl = slice(h * HEAD_DIM, (h + 1) * HEAD_DIM)
        rows = []
        for b in range(tb):
            q = q_ref[b:b + 1, sl]
            k = k_ref[b:b + 1, sl]
            v = v_ref[b:b + 1, sl]
            state = sr_ref[b, h]
            k8 = jnp.concatenate([k, pad], axis=0).astype(BF16)
            v8 = jnp.concatenate([v, pad], axis=0).astype(BF16)
            qk = jnp.sum(q * k, axis=1, keepdims=True)
            qs = _dot(jnp.concatenate([q * gamma, pad], axis=0).astype(BF16), state.astype(BF16))[0:1]
            rows.append(qk * v + qs)
            sn_ref[b, h] = gamma[:, 0:1] * state + _dot_tn(k8, v8)
        o = jnp.concatenate(rows, axis=0)
        o = o * lax.rsqrt(jnp.mean(o * o, axis=-1, keepdims=True) + EPS)
        or_ref[:, sl] = o * _silu(g_ref[:, sl])
    u = u_ref[...]
    for gi, w in enumerate(POOL_WINDOWS):
        sl = slice(gi * POOL_GW, (gi + 1) * POOL_GW)
        s = u[:, sl] + jnp.sum(sp_ref[:, POOL_BUF - (w - 1):POOL_BUF, sl], axis=1)
        p = s / float(min(pos + 1, w)) - u[:, sl]
        op_ref[:, sl] = _dot(p.astype(BF16), wp_ref[gi].astype(BF16)) * ps_ref[:, sl]
    pn_ref[:, 0:POOL_BUF - 1, :] = sp_ref[:, 1:POOL_BUF, :]
    for b in range(tb):
        pn_ref[b, POOL_BUF - 1:POOL_BUF, :] = u[b:b + 1, :]


def _state_sample(zs, log_g, state_ret, state_pool, w_pool, pool_scale, layer, dbs, pos):
    tb = SAMPLE_TB
    col = lambda off: pl.BlockSpec((tb, RET_W), lambda i, lg: (i, off // RET_W))
    return pl.pallas_call(
        functools.partial(_state_sample_kernel, pos=pos),
        grid_spec=pltpu.PrefetchScalarGridSpec(
            num_scalar_prefetch=1,
            grid=(dbs // tb,),
            in_specs=[
                col(OFF_RQ), col(OFF_RK), col(OFF_RV), col(OFF_RG), col(OFF_PU),
                pl.BlockSpec((None, tb, R_HEADS, HEAD_DIM, HEAD_DIM), lambda i, lg: (layer, i, 0, 0, 0)),
                pl.BlockSpec((None, tb, POOL_BUF, POOL_W), lambda i, lg: (layer, i, 0, 0)),
                pl.BlockSpec((None, len(POOL_WINDOWS), POOL_GW, POOL_GW), lambda i, lg: (layer, 0, 0, 0)),
                pl.BlockSpec((None, 1, POOL_W), lambda i, lg: (layer, 0, 0)),
            ],
            out_specs=[
                pl.BlockSpec((tb, RET_W), lambda i, lg: (i, 0)),
                pl.BlockSpec((tb, R_HEADS, HEAD_DIM, HEAD_DIM), lambda i, lg: (i, 0, 0, 0)),
                pl.BlockSpec((tb, POOL_W), lambda i, lg: (i, 0)),
                pl.BlockSpec((tb, POOL_BUF, POOL_W), lambda i, lg: (i, 0, 0)),
            ],
        ),
        out_shape=[
            jax.ShapeDtypeStruct((dbs, RET_W), F32),
            jax.ShapeDtypeStruct((dbs, R_HEADS, HEAD_DIM, HEAD_DIM), F32),
            jax.ShapeDtypeStruct((dbs, POOL_W), F32),
            jax.ShapeDtypeStruct((dbs, POOL_BUF, POOL_W), F32),
        ],
        compiler_params=_cparams(("parallel",)),
        name="state_sample",
    )(log_g, zs, zs, zs, zs, zs, state_ret, state_pool, w_pool, pool_scale)


N_CACHE = 4


def _softmax_lanes_with_new(s, s_new):
    m = jnp.maximum(jnp.max(s, axis=1, keepdims=True), s_new)
    p = jnp.exp(s - m)
    p_new = jnp.exp(s_new - m)
    inv = 1.0 / (jnp.sum(p, axis=1, keepdims=True) + p_new)
    return p * inv, p_new * inv


def _rank_select(srow):
    across = jnp.broadcast_to(srow, (LANES, LANES))
    down = across.T
    r_i = lax.broadcasted_iota(jnp.int32, (LANES, LANES), 0)
    c_i = lax.broadcasted_iota(jnp.int32, (LANES, LANES), 1)
    beats = jnp.logical_or(down > across, jnp.logical_and(down == across, r_i < c_i))
    rank = jnp.sum(jnp.where(beats, 1.0, 0.0), axis=0, keepdims=True)
    return jnp.logical_and(rank < SEL_TOPK, srow > -jnp.inf)


def _nsa_sample_sequence(q0_ref, q1_ref, ksn_ref, vsn_ref, kwn_ref, vwn_ref, gl_ref, wsel_ref,
                         w1k_ref, w2k_ref, pek_ref, w1v_ref, w2v_ref, pev_ref, wk_ref, wv_ref,
                         ck, cv, sk, sv, o_ref, *, n_pages, past, w_buf):
    scale = HEAD_DIM ** -0.5
    chunks_per_page = PAGE_SIZE // CMP_STRIDE
    n_chunk = n_pages * chunks_per_page
    n_sel = -(-(past + 1) // SEL_LEN)
    assert n_chunk <= LANES and n_sel <= LANES and past % SEL_LEN == 0

    group_rows = lambda ref, g, n: ref[pl.ds(g, n, stride=N_KV), :]

    def chunks_of(prefs):
        return jnp.concatenate(
            [group_rows(p, g, PAGE_SIZE).astype(BF16).reshape(chunks_per_page, CHUNK_W)
             for g in range(N_KV) for p in prefs],
            axis=0)

    ksum = _compress_core(chunks_of(ck), w1k_ref, w2k_ref, pek_ref)
    yield
    vsum = _compress_core(chunks_of(cv), w1v_ref, w2v_ref, pev_ref)
    yield

    lane = lax.broadcasted_iota(jnp.int32, (1, LANES), 1)
    q_refs = (q0_ref, q1_ref)
    zpad = jnp.zeros((SUBLANES - HPG, HEAD_DIM), F32)
    q8, oc, imps = [], [], []
    for g in range(N_KV):
        qg = jnp.concatenate([q_refs[g][:, h * HEAD_DIM:(h + 1) * HEAD_DIM] for h in range(HPG)] + [zpad],
                             axis=0) * scale
        q8.append(qg.astype(BF16))
        kcs = ksum[g * n_chunk:(g + 1) * n_chunk]
        vcs = vsum[g * n_chunk:(g + 1) * n_chunk]
        s_c = _dot_nt(q8[g], kcs.astype(BF16))
        c_end = lane[:, :n_chunk] * CMP_STRIDE + (CMP_LEN - 1)
        mask_c = jnp.logical_and(c_end <= past, c_end < n_chunk * CMP_STRIDE)
        m = jnp.max(jnp.where(mask_c, s_c, NEG), axis=1, keepdims=True)
        e = jnp.where(mask_c, jnp.exp(s_c - m), 0.0)
        d = jnp.sum(e, axis=1, keepdims=True)
        pc = e * (1.0 / jnp.where(d > 0, d, 1.0))
        oc.append(_dot(pc.astype(BF16), vcs.astype(BF16)))
        imps.append(jnp.sum(pc[0:HPG], axis=0, keepdims=True))
    yield

    imp = jnp.concatenate(imps + [jnp.zeros((SUBLANES - N_KV, n_chunk), F32)], axis=0)
    if n_chunk < LANES:
        imp = jnp.concatenate([imp, jnp.zeros((SUBLANES, LANES - n_chunk), F32)], axis=1)
    hi = imp.astype(BF16)
    r1 = imp - hi.astype(F32)
    mid = r1.astype(BF16)
    lo = (r1 - mid.astype(F32)).astype(BF16)
    wsel = wsel_ref[...]
    p_slc = _dot(hi, wsel) + _dot(mid, wsel) + _dot(lo, wsel)
    cur = past // SEL_LEN
    blk_valid = jnp.logical_and(lane * SEL_LEN <= past, lane < n_sel)
    forced = jnp.logical_or(lane == 0, jnp.logical_and(lane <= cur, lane > cur - N_LOCAL_SEL))
    score = jnp.where(blk_valid, jnp.where(forced, jnp.inf, p_slc), -jnp.inf)
    yield

    gate = jax.nn.sigmoid(gl_ref[...])
    blocks_per_page = PAGE_SIZE // SEL_LEN
    wl = lax.broadcasted_iota(jnp.int32, (1, w_buf), 1)
    dpos = w_buf - wl
    bias_w = jnp.where(jnp.logical_and(jnp.logical_and(dpos >= 0, dpos < WINDOW), past - dpos >= 0), 0.0, NEG)
    for g in range(N_KV):
        kv_new = slice(g * HEAD_DIM, (g + 1) * HEAD_DIM)
        q_f = q8[g].astype(F32)
        selb = jnp.where(_rank_select(score[g:g + 1]), 0.0, NEG)
        bias_pages = []
        for p in range(n_pages):
            bias = selb[:, p * blocks_per_page:p * blocks_per_page + 1]
            for bb in range(1, blocks_per_page):
                bias = jnp.where(lane < bb * SEL_LEN, bias, selb[:, p * blocks_per_page + bb:p * blocks_per_page + bb + 1])
            bias_pages.append(bias)
        k_all = jnp.concatenate([group_rows(p, g, PAGE_SIZE) for p in sk], axis=0).astype(BF16)
        v_all = jnp.concatenate([group_rows(p, g, PAGE_SIZE) for p in sv], axis=0).astype(BF16)
        s_all = _dot_nt(q8[g], k_all) + jnp.concatenate(bias_pages, axis=1)
        s_new = jnp.sum(q_f * ksn_ref[:, kv_new].astype(BF16).astype(F32), axis=1, keepdims=True)
        p_all, p_new = _softmax_lanes_with_new(s_all, s_new)
        o_s = p_new * vsn_ref[:, kv_new] + _dot(p_all.astype(BF16), v_all)
        yield
        s_w = _dot_nt(q8[g], group_rows(wk_ref, g, w_buf).astype(BF16)) + bias_w
        s_wn = jnp.sum(q_f * kwn_ref[:, kv_new].astype(BF16).astype(F32), axis=1, keepdims=True)
        p_w, p_wn = _softmax_lanes_with_new(s_w, s_wn)
        o_w = p_wn * vwn_ref[:, kv_new] + _dot(p_w.astype(BF16), group_rows(wv_ref, g, w_buf).astype(BF16))
        for h in range(HPG):
            c0 = (g * HPG + h) * 3
            og = (gate[:, c0:c0 + 1] * oc[g][h:h + 1] + gate[:, c0 + 1:c0 + 2] * o_s[h:h + 1]
                  + gate[:, c0 + 2:c0 + 3] * o_w[h:h + 1])
            o_ref[:, (g * HPG + h) * HEAD_DIM:(g * HPG + h + 1) * HEAD_DIM] = og


SAMPLE_SEQS = 1


def _nsa_sample_kernel(pt_ref, *refs, n_pages, past, w_buf, layer, n_older):
    blocked, rest = refs[:16], refs[16:]
    caches = rest[:N_CACHE]
    older = rest[N_CACHE:N_CACHE + 4 * n_older]
    outs = rest[N_CACHE + 4 * n_older:-2]
    o_ref = outs[0]
    pages_buf, sem = rest[-2:]
    b = pl.program_id(0)

    def page_copies(step, slot):
        return [pltpu.make_async_copy(caches[c].at[layer, pt_ref[(step * SAMPLE_SEQS + sq) * n_pages + p]],
                                      pages_buf.at[slot, sq, c, p], sem.at[slot])
                for sq in range(SAMPLE_SEQS) for c in range(N_CACHE) for p in range(n_pages)]

    @pl.when(b == 0)
    def _():
        for cp in page_copies(0, 0):
            cp.start()

    @pl.when(b + 1 < pl.num_programs(0))
    def _():
        for cp in page_copies(b + 1, jnp.bitwise_and(b + 1, 1)):
            cp.start()

    slot = jnp.bitwise_and(b, 1)
    for cp in page_copies(b, slot):
        cp.wait()
    per_seq = (0, 1, 2, 3, 4, 5, 6, 14, 15)
    stages = []
    for sq in range(SAMPLE_SEQS):
        ins = [r.at[sq] if k in per_seq else r for k, r in enumerate(blocked)]
        pages = [[pages_buf.at[slot, sq, c, p] for p in range(n_pages)] for c in range(N_CACHE)]
        stages.append(_nsa_sample_sequence(*ins, *pages, o_ref.at[sq], n_pages=n_pages, past=past, w_buf=w_buf))
    for _ in itertools.zip_longest(*stages):
        pass

    if len(outs) > 1:
        wko_ref, wvo_ref = outs[1:]
        shifts = [(layer, blocked[4], blocked[5], blocked[14], blocked[15])]
        shifts += [(lo,) + tuple(older[4 * lo:4 * lo + 4]) for lo in range(n_older)]
        keep = N_KV * (w_buf - 1)
        for lo, kn, vn, wk, wv in shifts:
            for sq in range(SAMPLE_SEQS):
                wko_ref[lo, sq, 0:keep, :] = wk[sq, N_KV:N_KV * w_buf, :]
                wvo_ref[lo, sq, 0:keep, :] = wv[sq, N_KV:N_KV * w_buf, :]
                for g in range(N_KV):
                    wko_ref[lo, sq, keep + g:keep + g + 1, :] = kn[sq, :, g * HEAD_DIM:(g + 1) * HEAD_DIM]
                    wvo_ref[lo, sq, keep + g:keep + g + 1, :] = vn[sq, :, g * HEAD_DIM:(g + 1) * HEAD_DIM]


def _nsa_sample(zs3, page_table, cmp_pages, slc_pages, win_k, win_v, cw, wsel, layer, dbs, past, w_buf, older_zs3):
    n_pages = past // PAGE_SIZE
    depth = win_k.shape[0]
    n_older = 0 if older_zs3 is None else len(older_zs3)
    assert older_zs3 is None or n_older == layer == depth - 1
    assert dbs % SAMPLE_SEQS == 0
    nsq = SAMPLE_SEQS
    rowblk = lambda width, off: pl.BlockSpec((nsq, 1, width), lambda b, pt: (b, 0, off // width))
    const3 = lambda shp: pl.BlockSpec((None,) + shp, lambda b, pt: (layer,) + (0,) * len(shp))
    page_shape = (PAGE_SIZE * N_KV, HEAD_DIM)
    win_shape = (w_buf * N_KV, HEAD_DIM)
    win = pl.BlockSpec((None, nsq) + win_shape, lambda b, pt: (layer, b, 0, 0))
    hbm = pl.BlockSpec(memory_space=pl.ANY)
    in_specs = [
        rowblk(HPG * HEAD_DIM, OFF_NQ), rowblk(HPG * HEAD_DIM, OFF_NQ + HPG * HEAD_DIM),
        rowblk(KV_W, OFF_KS), rowblk(KV_W, OFF_VS), rowblk(KV_W, OFF_KW), rowblk(KV_W, OFF_VW),
        rowblk(LANES, OFF_GL),
        pl.BlockSpec((LANES, LANES), lambda b, pt: (0, 0)),
        const3((CHUNK_W, 2 * HEAD_DIM)), const3((HEAD_DIM, HEAD_DIM)), const3((PE_ROWS, CHUNK_W)),
        const3((CHUNK_W, 2 * HEAD_DIM)), const3((HEAD_DIM, HEAD_DIM)), const3((PE_ROWS, CHUNK_W)),
        win, win,
    ] + [hbm] * N_CACHE
    operands = [zs3] * 7 + [wsel, cw["w1k"], cw["w2k"], cw["pek"], cw["w1v"], cw["w2v"], cw["pev"], win_k, win_v]
    operands += list(cmp_pages) + list(slc_pages)
    out_specs = [pl.BlockSpec((nsq, 1, NSA_W), lambda b, pt: (b, 0, 0))]
    out_shape = [jax.ShapeDtypeStruct((dbs, 1, NSA_W), F32)]
    if older_zs3 is not None:
        for lo in range(n_older):
            win_lo = pl.BlockSpec((None, nsq) + win_shape, lambda b, pt, lo=lo: (lo, b, 0, 0))
            operands += [older_zs3[lo], older_zs3[lo], win_k, win_v]
            in_specs += [rowblk(KV_W, OFF_KW), rowblk(KV_W, OFF_VW), win_lo, win_lo]
        out_specs += [pl.BlockSpec((depth, nsq) + win_shape, lambda b, pt: (0, b, 0, 0))] * 2
        out_shape += [jax.ShapeDtypeStruct((depth, dbs) + win_shape, F32)] * 2
    return pl.pallas_call(
        functools.partial(_nsa_sample_kernel, n_pages=n_pages, past=past, w_buf=w_buf, layer=layer,
                          n_older=n_older),
        grid_spec=pltpu.PrefetchScalarGridSpec(
            num_scalar_prefetch=1,
            grid=(dbs // nsq,),
            in_specs=in_specs,
            out_specs=out_specs,
            scratch_shapes=[pltpu.VMEM((2, nsq, N_CACHE, n_pages) + page_shape, F32),
                            pltpu.SemaphoreType.DMA((2,))],
        ),
        out_shape=out_shape,
        compiler_params=_cparams(("arbitrary",), VMEM_LIMIT_BIG),
        name="nsa_sample",
    )(page_table.reshape(-1), *operands)


def _rope_tables(pos):
    half = HEAD_DIM // 2
    inv = ROPE_THETA ** (-jnp.arange(half, dtype=F32) / half)
    ang = pos.astype(F32)[:, None] * inv[None, :]
    cos, sin = jnp.cos(ang), jnp.sin(ang)
    return jnp.concatenate([cos, cos], axis=-1), jnp.concatenate([-sin, sin], axis=-1)


def _block_sum_matrix():
    w = np.zeros((LANES, LANES), np.float32)
    for j in range(LANES):
        for mm in range(SEL_RATIO):
            for nn in range(CMP_RATIO):
                c = SEL_RATIO * j + mm - nn
                if 0 <= c < LANES:
                    w[c, j] += 1.0
    return w


def _row_tile(m, cap):
    tm = min(m, cap)
    while m % tm:
        tm //= 2
    return tm


def kernel(x_prompt, x_sample, state_ret, state_pool, cache_cmp_k, cache_cmp_v, cache_slc_k, cache_slc_v,
           cache_win_k, cache_win_v, page_table, norm1, w_in, w_pool, pool_scale, cmp_w1_k, cmp_w2_k, cmp_pe_k,
           cmp_w1_v, cmp_w2_v, cmp_pe_v, w_out, norm2, w_up, w_down, norm_f):
    bsz, t, _ = x_prompt.shape
    dbs, dec_seq, _ = x_sample.shape
    depth = w_in.shape[0]
    assert dec_seq == 1 and t % 512 == 0 and dbs % SAMPLE_TB == 0
    n_pages = page_table.shape[1]
    past = n_pages * PAGE_SIZE
    n_pool = cache_cmp_k.shape[1]
    w_buf = cache_win_k.shape[2]
    wlen = min(WINDOW, t)

    w_in_bf = jnp.pad(w_in, ((0, 0), (0, 0), (0, N_PAD - D_IN))).astype(BF16)
    w_out_bf = w_out.astype(BF16)
    w1cat = lambda w1: jnp.concatenate([w1[:, :CMP_STRIDE], w1[:, CMP_STRIDE:]], axis=-1).astype(BF16).reshape(
        depth, CHUNK_W, 2 * HEAD_DIM)
    pe_rows = lambda pe: jnp.pad(pe.reshape(depth, CMP_RATIO, CHUNK_W),
                                 ((0, 0), (0, PE_ROWS - CMP_RATIO), (0, 0))).astype(BF16)
    cw = dict(w1k=w1cat(cmp_w1_k), w2k=cmp_w2_k.astype(BF16), pek=pe_rows(cmp_pe_k),
              w1v=w1cat(cmp_w1_v), w2v=cmp_w2_v.astype(BF16), pev=pe_rows(cmp_pe_v))
    norm1_3, norm2_3 = norm1[:, None, :], norm2[:, None, :]
    pool_scale_3 = pool_scale[:, None, :]
    norm_f2 = norm_f[None, :]
    log_g = jnp.log1p(-jnp.exp2(-5.0 - jnp.arange(R_HEADS, dtype=F32)))
    cos_p, sin_p = _rope_tables(jnp.arange(t))
    cos_s, sin_s = _rope_tables(jnp.full((dbs,), past, jnp.int32))
    wsel = jnp.asarray(_block_sum_matrix(), BF16)
    page_view = lambda c: c.reshape(depth, n_pool, PAGE_SIZE * N_KV, HEAD_DIM)
    cmp_pages = [page_view(c) for c in (cache_cmp_k, cache_cmp_v)]
    slc_pages = [page_view(c) for c in (cache_slc_k, cache_slc_v)]
    win_k = cache_win_k.reshape(depth, dbs, w_buf * N_KV, HEAD_DIM)
    win_v = cache_win_v.reshape(depth, dbs, w_buf * N_KV, HEAD_DIM)

    mp = bsz * t
    tm_p = _row_tile(t, 1024)
    tm_mlp = _row_tile(t, 1024)
    xp = x_prompt.reshape(mp, D_MODEL)
    xs = x_sample.reshape(dbs, D_MODEL)
    outs = {k: [] for k in ("ret_p", "ret_s", "pool_p", "pool_s", "ck_p", "ck_s", "cv_p", "cv_s", "sk_p", "sk_s",
                            "sv_p", "sv_s", "wk_p", "wv_p")}
    zs3_all = []
    for l in range(depth):
        last = l == depth - 1
        z = _inproj(xp, norm1_3, w_in_bf, l, cos_p, sin_p, tm_p)
        o_r, s_new = _ret_prompt(z, log_g, bsz, t)
        o_p = _pool_prompt(z, w_pool, pool_scale_3, l, bsz, t)
        kcs, vcst, vst, vwt = _prep_prompt(z, cw, l, bsz, t)
        o_n = _nsa_prompt(z, kcs, vcst, vst, vwt, bsz, t)
        xp = _outproj(o_r, o_p, o_n, w_out_bf, l, xp, _row_tile(t, 512))
        xp = _mlp(xp, norm2_3, w_up, w_down, l, norm_f2, last, tm_mlp)
        z3 = z.reshape(bsz, t, N_PAD)
        kv = lambda off: z3[:, :, off:off + KV_W].reshape(bsz, t, N_KV, HEAD_DIM)
        outs["ret_p"].append(s_new)
        outs["pool_p"].append(z3[:, t - POOL_BUF:, OFF_PU:OFF_PU + POOL_W])
        outs["ck_p"].append(kv(OFF_KC))
        outs["cv_p"].append(kv(OFF_VC))
        outs["sk_p"].append(kv(OFF_KS))
        outs["sv_p"].append(kv(OFF_VS))
        outs["wk_p"].append(kv(OFF_KW)[:, t - wlen:])
        outs["wv_p"].append(kv(OFF_VW)[:, t - wlen:])
        zs = _inproj(xs, norm1_3, w_in_bf, l, cos_s, sin_s, dbs)
        o_r, s_new, o_p, buf_new = _state_sample(zs, log_g, state_ret, state_pool, w_pool, pool_scale_3, l, dbs, past)
        zs3 = zs.reshape(dbs, 1, N_PAD)
        res = _nsa_sample(zs3, page_table, cmp_pages, slc_pages, win_k, win_v, cw, wsel, l, dbs, past, w_buf,
                          zs3_all if last else None)
        o_n = res[0]
        zs3_all.append(zs3)
        if last:
            wk_s, wv_s = res[1:]
        xs = _outproj(o_r, o_p, o_n.reshape(dbs, NSA_W), w_out_bf, l, xs, dbs)
        xs = _mlp(xs, norm2_3, w_up, w_down, l, norm_f2, last, dbs)
        kvs = lambda off: zs[:, off:off + KV_W].reshape(dbs, 1, N_KV, HEAD_DIM)
        outs["ret_s"].append(s_new)
        outs["pool_s"].append(buf_new)
        outs["ck_s"].append(kvs(OFF_KC))
        outs["cv_s"].append(kvs(OFF_VC))
        outs["sk_s"].append(kvs(OFF_KS))
        outs["sv_s"].append(kvs(OFF_VS))
    st = lambda k: jnp.stack(outs[k])
    win_out = lambda w: w.reshape(depth, dbs, w_buf, N_KV, HEAD_DIM)
    return (xp.reshape(bsz, t, D_MODEL), xs.reshape(dbs, 1, D_MODEL),
            st("ret_p"), st("ret_s"), st("pool_p"), st("pool_s"),
            st("ck_p"), st("ck_s"), st("cv_p"), st("cv_s"), st("sk_p"), st("sk_s"), st("sv_p"), st("sv_s"),
            st("wk_p"), win_out(wk_s), st("wv_p"), win_out(wv_s))
```

```python
import functools
import itertools

import numpy as np
import jax
import jax.numpy as jnp
from jax import lax
from jax.experimental import pallas as pl
from jax.experimental.pallas import tpu as pltpu

F32 = jnp.float32
BF16 = jnp.bfloat16

D_MODEL = 2048
HEAD_DIM = 128
PAGE_SIZE = 128
RET_W = D_MODEL // 4
POOL_W = D_MODEL // 4
NSA_W = D_MODEL - RET_W - POOL_W
R_HEADS = RET_W // HEAD_DIM
N_HEADS = NSA_W // HEAD_DIM
N_KV = max(1, N_HEADS // 4)
HPG = N_HEADS // N_KV
KV_W = N_KV * HEAD_DIM
POOL_WINDOWS = (2, 4, 8, 16)
POOL_GW = POOL_W // len(POOL_WINDOWS)
POOL_BUF = max(POOL_WINDOWS) - 1
RET_CHUNK = 128
CMP_LEN = 32
CMP_STRIDE = 16
CMP_RATIO = CMP_LEN // CMP_STRIDE
SEL_LEN = 64
SEL_RATIO = SEL_LEN // CMP_STRIDE
SEL_TOPK = 16
N_LOCAL_SEL = 2
WINDOW = 512
Q_BLOCK = 128
D_FF = 4 * D_MODEL
ROPE_THETA = 10000.0
EPS = 1e-6
D_IN = 4 * RET_W + POOL_W + NSA_W + 6 * KV_W + 3 * N_HEADS

OFF_RQ = 0
OFF_RK = OFF_RQ + RET_W
OFF_RV = OFF_RK + RET_W
OFF_RG = OFF_RV + RET_W
OFF_PU = OFF_RG + RET_W
OFF_NQ = OFF_PU + POOL_W
OFF_KC = OFF_NQ + NSA_W
OFF_VC = OFF_KC + KV_W
OFF_KS = OFF_VC + KV_W
OFF_VS = OFF_KS + KV_W
OFF_KW = OFF_VS + KV_W
OFF_VW = OFF_KW + KV_W
OFF_GL = OFF_VW + KV_W
PROJ_TN = 512
PROJ_ROW_CHUNK = 256
N_PAD = -(-D_IN // PROJ_TN) * PROJ_TN
ROPE_FULL_TILES = (OFF_RQ // PROJ_TN, OFF_RK // PROJ_TN, OFF_NQ // PROJ_TN, OFF_NQ // PROJ_TN + 1)
ROPE_HALF_TILES = (OFF_KC // PROJ_TN, OFF_KS // PROJ_TN, OFF_KW // PROJ_TN)
RK_TILE = OFF_RK // PROJ_TN

NEG = -1e30
V7X_VMEM_BYTES = 64 * 1024 * 1024
VMEM_LIMIT = V7X_VMEM_BYTES * 3 // 4
VMEM_LIMIT_BIG = V7X_VMEM_BYTES * 31 // 32
LANES = 128
SUBLANES = 8


def _cparams(sem, vmem_limit=VMEM_LIMIT):
    return pltpu.CompilerParams(dimension_semantics=sem, vmem_limit_bytes=vmem_limit)


def _dot(a, b):
    return jnp.dot(a, b, preferred_element_type=F32)


def _dot_nt(a, b):
    return lax.dot_general(a, b, (((1,), (1,)), ((), ())), preferred_element_type=F32)


def _dot_tn(a, b):
    return lax.dot_general(a, b, (((0,), (0,)), ((), ())), preferred_element_type=F32)


def _rms(x, g):
    return x * lax.rsqrt(jnp.mean(x * x, axis=-1, keepdims=True) + EPS) * g


def _silu(x):
    return x * jax.nn.sigmoid(x)


def _inproj_kernel(x_ref, g_ref, w_ref, cos_ref, sin_ref, z_ref, hn_ref):
    j = pl.program_id(1)

    @pl.when(j == 0)
    def _():
        hn_ref[...] = _rms(x_ref[...], g_ref[...]).astype(BF16)

    is_full = functools.reduce(jnp.logical_or, [j == t for t in ROPE_FULL_TILES])
    is_half = functools.reduce(jnp.logical_or, [j == t for t in ROPE_HALF_TILES])
    scale = jnp.where(j == RK_TILE, HEAD_DIM ** -0.5, 1.0).astype(F32)
    is_any = jnp.logical_or(is_full, is_half)
    tm = x_ref.shape[0]
    rc = min(tm, PROJ_ROW_CHUNK)
    for r0 in range(0, tm, rc):
        rows = slice(r0, r0 + rc)
        z = _dot(hn_ref[rows, :], w_ref[...])
        cos_t, sin_t = cos_ref[rows, :] * scale, sin_ref[rows, :] * scale
        tab_lead = (jnp.where(is_any, cos_t, 1.0), jnp.where(is_any, sin_t, 0.0))
        tab_rest = (jnp.where(is_full, cos_t, 1.0), jnp.where(is_full, sin_t, 0.0))
        for s in range(PROJ_TN // HEAD_DIM):
            sl = slice(s * HEAD_DIM, (s + 1) * HEAD_DIM)
            c, sn = tab_lead if s < KV_W // HEAD_DIM else tab_rest
            zs = z[:, sl]
            z_ref[rows, sl] = zs * c + pltpu.roll(zs, HEAD_DIM // 2, 1) * sn


def _inproj(x, gamma, w_in_bf, layer, cos_t, sin_t, tm):
    m = x.shape[0]
    n_row_tiles = cos_t.shape[0] // tm
    return pl.pallas_call(
        _inproj_kernel,
        grid=(m // tm, N_PAD // PROJ_TN),
        in_specs=[
            pl.BlockSpec((tm, D_MODEL), lambda i, j: (i, 0)),
            pl.BlockSpec((None, 1, D_MODEL), lambda i, j: (layer, 0, 0)),
            pl.BlockSpec((None, D_MODEL, PROJ_TN), lambda i, j: (layer, 0, j)),
            pl.BlockSpec((tm, HEAD_DIM), lambda i, j: (i % n_row_tiles, 0)),
            pl.BlockSpec((tm, HEAD_DIM), lambda i, j: (i % n_row_tiles, 0)),
        ],
        out_specs=pl.BlockSpec((tm, PROJ_TN), lambda i, j: (i, j)),
        out_shape=jax.ShapeDtypeStruct((m, N_PAD), F32),
        scratch_shapes=[pltpu.VMEM((tm, D_MODEL), BF16)],
        compiler_params=_cparams(("parallel", "arbitrary")),
        name="inproj",
    )(x, gamma, w_in_bf, cos_t, sin_t)


def _outproj_kernel(r_ref, p_ref, n_ref, w_ref, x_ref, o_ref):
    acc = _dot(r_ref[...].astype(BF16), w_ref[0:RET_W, :])
    acc += _dot(p_ref[...].astype(BF16), w_ref[RET_W:RET_W + POOL_W, :])
    acc += _dot(n_ref[...].astype(BF16), w_ref[RET_W + POOL_W:, :])
    o_ref[...] = x_ref[...] + acc


def _outproj(o_r, o_p, o_n, w_out_bf, layer, x, tm, tn=D_MODEL):
    m = x.shape[0]
    return pl.pallas_call(
        _outproj_kernel,
        grid=(m // tm, D_MODEL // tn),
        in_specs=[
            pl.BlockSpec((tm, RET_W), lambda i, j: (i, 0)),
            pl.BlockSpec((tm, POOL_W), lambda i, j: (i, 0)),
            pl.BlockSpec((tm, NSA_W), lambda i, j: (i, 0)),
            pl.BlockSpec((None, D_MODEL, tn), lambda i, j: (layer, 0, j)),
            pl.BlockSpec((tm, tn), lambda i, j: (i, j)),
        ],
        out_specs=pl.BlockSpec((tm, tn), lambda i, j: (i, j)),
        out_shape=jax.ShapeDtypeStruct((m, D_MODEL), F32),
        compiler_params=_cparams(("parallel", "parallel")),
        name="outproj",
    )(o_r, o_p, o_n, w_out_bf, x)


def _mlp_kernel(x_ref, g_ref, wu_ref, wd_ref, gf_ref, o_ref, hn_ref, *, final_norm):
    f = pl.program_id(1)

    @pl.when(f == 0)
    def _():
        hn_ref[...] = _rms(x_ref[...], g_ref[...]).astype(BF16)
        o_ref[...] = jnp.zeros_like(o_ref)

    u = jnp.maximum(_dot(hn_ref[...], wu_ref[...].astype(BF16)), 0.0)
    o_ref[...] += _dot((u * u).astype(BF16), wd_ref[...].astype(BF16))

    @pl.when(f == pl.num_programs(1) - 1)
    def _():
        y = x_ref[...] + o_ref[...]
        o_ref[...] = _rms(y, gf_ref[...]) if final_norm else y


def _mlp(x, gamma, w_up, w_down, layer, gamma_f, final_norm, tm, tf=512):
    m = x.shape[0]
    return pl.pallas_call(
        functools.partial(_mlp_kernel, final_norm=final_norm),
        grid=(m // tm, D_FF // tf),
        in_specs=[
            pl.BlockSpec((tm, D_MODEL), lambda i, f: (i, 0)),
            pl.BlockSpec((None, 1, D_MODEL), lambda i, f: (layer, 0, 0)),
            pl.BlockSpec((None, D_MODEL, tf), lambda i, f: (layer, 0, f)),
            pl.BlockSpec((None, tf, D_MODEL), lambda i, f: (layer, f, 0)),
            pl.BlockSpec((1, D_MODEL), lambda i, f: (0, 0)),
        ],
        out_specs=pl.BlockSpec((tm, D_MODEL), lambda i, f: (i, 0)),
        out_shape=jax.ShapeDtypeStruct((m, D_MODEL), F32),
        scratch_shapes=[pltpu.VMEM((tm, D_MODEL), BF16)],
        compiler_params=_cparams(("parallel", "arbitrary"), VMEM_LIMIT_BIG),
        name="mlp",
    )(x, gamma, w_up, w_down, gamma_f)


def _ret_prompt_kernel(lg_ref, q_ref, k_ref, v_ref, g_ref, o_ref, s_ref, *, n_chunk):
    ti = pl.program_id(1)
    L = RET_CHUNK
    ri = lax.broadcasted_iota(jnp.int32, (L, L), 0)
    ci = lax.broadcasted_iota(jnp.int32, (L, L), 1)
    diff = (ri - ci).astype(F32)
    causal = diff >= 0
    idx = lax.broadcasted_iota(jnp.int32, (L, 1), 0).astype(F32)
    consts = []
    for h in range(R_HEADS):
        lg = lg_ref[h]
        decay = jnp.where(causal, jnp.exp(jnp.where(causal, diff, 0.0) * lg), 0.0)
        q_dec = jnp.exp((idx + 1.0) * lg)
        k_dec = jnp.exp((L - 1.0 - idx) * lg)
        s_dec = jnp.exp(jnp.full((1, 1), float(L), F32) * lg)
        consts.append((decay, q_dec, k_dec, s_dec))

    @pl.when(ti == 0)
    def _():
        s_ref[...] = jnp.zeros_like(s_ref)

    def body(c, states):
        rows = pl.ds(pl.multiple_of(c * L, L), L)
        new_states = []
        for h, (decay, q_dec, k_dec, s_dec) in enumerate(consts):
            sl = slice(h * HEAD_DIM, (h + 1) * HEAD_DIM)
            q, k, v = q_ref[rows, sl], k_ref[rows, sl], v_ref[rows, sl]
            vb = v.astype(BF16)
            att = _dot_nt(q.astype(BF16), k.astype(BF16)) * decay
            o = _dot(att.astype(BF16), vb) + _dot((q * q_dec).astype(BF16), states[h].astype(BF16))
            new_states.append(s_dec * states[h] + _dot_tn((k * k_dec).astype(BF16), vb))
            o = o * lax.rsqrt(jnp.mean(o * o, axis=-1, keepdims=True) + EPS)
            o_ref[rows, sl] = o * _silu(g_ref[rows, sl])
        return tuple(new_states)

    states = lax.fori_loop(0, n_chunk, body, tuple(s_ref[h] for h in range(R_HEADS)))
    for h in range(R_HEADS):
        s_ref[h] = states[h]


def _ret_prompt(z, log_g, bsz, t, tt=1024):
    tt = min(tt, t)
    nt = t // tt
    blk = lambda off: pl.BlockSpec((tt, RET_W), lambda b, i, lg: (b * nt + i, off // RET_W))
    return pl.pallas_call(
        functools.partial(_ret_prompt_kernel, n_chunk=tt // RET_CHUNK),
        grid_spec=pltpu.PrefetchScalarGridSpec(
            num_scalar_prefetch=1,
            grid=(bsz, nt),
            in_specs=[blk(OFF_RQ), blk(OFF_RK), blk(OFF_RV), blk(OFF_RG)],
            out_specs=[
                pl.BlockSpec((tt, RET_W), lambda b, i, lg: (b * nt + i, 0)),
                pl.BlockSpec((None, R_HEADS, HEAD_DIM, HEAD_DIM), lambda b, i, lg: (b, 0, 0, 0)),
            ],
        ),
        out_shape=[
            jax.ShapeDtypeStruct((bsz * t, RET_W), F32),
            jax.ShapeDtypeStruct((bsz, R_HEADS, HEAD_DIM, HEAD_DIM), F32),
        ],
        compiler_params=_cparams(("parallel", "arbitrary")),
        name="ret_prompt",
    )(log_g, z, z, z, z)


POOL_HALO = 16


def _pool_prompt_kernel(u_ref, wp_ref, ps_ref, o_ref, ext_ref, *, tt):
    t = pl.program_id(1)

    @pl.when(t == 0)
    def _():
        ext_ref[0:POOL_HALO, :] = jnp.zeros((POOL_HALO, POOL_W), F32)

    u = u_ref[...]
    ext_ref[POOL_HALO:POOL_HALO + tt, :] = u
    pos = t * tt + lax.broadcasted_iota(jnp.int32, (tt, 1), 0)
    for g, w in enumerate(POOL_WINDOWS):
        sl = slice(g * POOL_GW, (g + 1) * POOL_GW)
        s = u[:, sl]
        for i in range(1, w):
            s = s + ext_ref[pl.ds(POOL_HALO - i, tt), sl]
        cnt = jnp.minimum(pos + 1, w).astype(F32)
        p = s / cnt - u[:, sl]
        o_ref[:, sl] = _dot(p.astype(BF16), wp_ref[g].astype(BF16)) * ps_ref[:, sl]
    ext_ref[0:POOL_HALO, :] = ext_ref[tt:tt + POOL_HALO, :]


def _pool_prompt(z, w_pool, pool_scale, layer, bsz, t, tt=512):
    nt = t // tt
    return pl.pallas_call(
        functools.partial(_pool_prompt_kernel, tt=tt),
        grid=(bsz, nt),
        in_specs=[
            pl.BlockSpec((tt, POOL_W), lambda b, i: (b * nt + i, OFF_PU // POOL_W)),
            pl.BlockSpec((None, len(POOL_WINDOWS), POOL_GW, POOL_GW), lambda b, i: (layer, 0, 0, 0)),
            pl.BlockSpec((None, 1, POOL_W), lambda b, i: (layer, 0, 0)),
        ],
        out_specs=pl.BlockSpec((tt, POOL_W), lambda b, i: (b * nt + i, 0)),
        out_shape=jax.ShapeDtypeStruct((bsz * t, POOL_W), F32),
        scratch_shapes=[pltpu.VMEM((POOL_HALO + tt, POOL_W), F32)],
        compiler_params=_cparams(("parallel", "arbitrary")),
        name="pool_prompt",
    )(z, w_pool, pool_scale)


CHUNK_W = CMP_STRIDE * HEAD_DIM
PE_ROWS = 16


def _compress_core(chunks, w1_ref, w2_ref, pe_ref):
    n_rows = chunks.shape[0]
    acc = _dot(jnp.concatenate([chunks, pe_ref[...]], axis=0), w1_ref[...])
    first = acc[:n_rows, :HEAD_DIM]
    second = acc[:n_rows, HEAD_DIM:]
    pe_term = acc[n_rows:n_rows + 1, :HEAD_DIM] + acc[n_rows + 1:n_rows + 2, HEAD_DIM:]
    h = first + pltpu.roll(second, n_rows - 1, 0) + pe_term
    return _dot(_silu(h).astype(BF16), w2_ref[...])


def _prep_prompt_kernel(kc_ref, vc_ref, vs_ref, vw_ref, w1k_ref, w2k_ref, pek_ref, w1v_ref, w2v_ref, pev_ref,
                        kcs_ref, vcst_ref, vst_ref, vwt_ref, *, n_chunk, n_tile):
    chunks_of = lambda r: jnp.concatenate(
        [r[pl.ds(s, n_chunk, stride=CMP_STRIDE), :].astype(BF16) for s in range(CMP_STRIDE)], axis=1)
    kcs_ref[...] = _compress_core(chunks_of(kc_ref), w1k_ref, w2k_ref, pek_ref).astype(BF16)
    vcst_ref[...] = _compress_core(chunks_of(vc_ref), w1v_ref, w2v_ref, pev_ref).T.astype(BF16)

    def tbody(i, carry):
        rows = pl.ds(pl.multiple_of(i * HEAD_DIM, HEAD_DIM), HEAD_DIM)
        vst_ref[i] = vs_ref[rows, :].T.astype(BF16)
        vwt_ref[i] = vw_ref[rows, :].T.astype(BF16)
        return carry

    lax.fori_loop(0, n_tile, tbody, 0)


def _prep_prompt(z, cw, layer, bsz, t):
    n_chunk = t // CMP_STRIDE
    n_tile = t // HEAD_DIM
    col = lambda off: pl.BlockSpec((t, HEAD_DIM), lambda b, g: (b, off // HEAD_DIM + g))
    w1 = pl.BlockSpec((None, CHUNK_W, 2 * HEAD_DIM), lambda b, g: (layer, 0, 0))
    w2 = pl.BlockSpec((None, HEAD_DIM, HEAD_DIM), lambda b, g: (layer, 0, 0))
    pe = pl.BlockSpec((None, PE_ROWS, CHUNK_W), lambda b, g: (layer, 0, 0))
    return pl.pallas_call(
        functools.partial(_prep_prompt_kernel, n_chunk=n_chunk, n_tile=n_tile),
        grid=(bsz, N_KV),
        in_specs=[col(OFF_KC), col(OFF_VC), col(OFF_VS), col(OFF_VW), w1, w2, pe, w1, w2, pe],
        out_specs=[
            pl.BlockSpec((None, None, n_chunk, HEAD_DIM), lambda b, g: (b, g, 0, 0)),
            pl.BlockSpec((None, None, HEAD_DIM, n_chunk), lambda b, g: (b, g, 0, 0)),
            pl.BlockSpec((None, None, n_tile, HEAD_DIM, HEAD_DIM), lambda b, g: (b, g, 0, 0, 0)),
            pl.BlockSpec((None, None, n_tile, HEAD_DIM, HEAD_DIM), lambda b, g: (b, g, 0, 0, 0)),
        ],
        out_shape=[
            jax.ShapeDtypeStruct((bsz, N_KV, n_chunk, HEAD_DIM), BF16),
            jax.ShapeDtypeStruct((bsz, N_KV, HEAD_DIM, n_chunk), BF16),
            jax.ShapeDtypeStruct((bsz, N_KV, n_tile, HEAD_DIM, HEAD_DIM), BF16),
            jax.ShapeDtypeStruct((bsz, N_KV, n_tile, HEAD_DIM, HEAD_DIM), BF16),
        ],
        compiler_params=_cparams(("parallel", "parallel")),
        name="prep_prompt",
    )(z, z, z, z, cw["w1k"], cw["w2k"], cw["pek"], cw["w1v"], cw["w2v"], cw["pev"])


LOG2E = 1.4426950408889634


def _softmax_tiles(tiles, carry):
    m, l, acc = carry
    m_new = m
    for s, _ in tiles:
        m_new = jnp.maximum(m_new, jnp.max(s, axis=0, keepdims=True))
    alpha = jnp.exp2(m - m_new)
    l = alpha * l
    acc = alpha * acc
    for s, vt in tiles:
        p = jnp.exp2(s - m_new)
        l = l + jnp.sum(p, axis=0, keepdims=True)
        acc = acc + _dot(vt, p.astype(BF16))
    return m_new, l, acc


def _select_blocks(score, score_ref):
    n_blocks = score.shape[0]
    score_ref[...] = score
    groups = [score[v * SUBLANES:(v + 1) * SUBLANES] for v in range(n_blocks // SUBLANES)]
    jb = lax.broadcasted_iota(jnp.int32, groups[0].shape, 0)
    ranks = [jnp.zeros(gv.shape, F32) for gv in groups]
    for k in range(n_blocks):
        rk = score_ref[pl.ds(k, 1), :]
        for v, gv in enumerate(groups):
            if v * SUBLANES > k:
                beats = rk >= gv
            elif (v + 1) * SUBLANES - 1 <= k:
                beats = rk > gv
            else:
                beats = jnp.logical_or(rk > gv, jnp.logical_and(jb + v * SUBLANES > k, rk == gv))
            ranks[v] = ranks[v] + jnp.where(beats, 1.0, 0.0)
    rank = jnp.concatenate(ranks, axis=0)
    return jnp.logical_and(rank < SEL_TOPK, score > -jnp.inf)


def _nsa_prompt_kernel(q0_ref, q1_ref, gl_ref, kcs_ref, vcst_ref, ks_ref, vst_ref, kw_ref, vwt_ref, o_ref,
                       imp_ref, selb_ref, gt_ref, score_ref, *, n_chunk, n_sel):
    i = pl.program_id(1)
    groups = range(N_KV)
    qw = HPG * Q_BLOCK
    qscale = HEAD_DIM ** -0.5 * LOG2E
    lane = lax.broadcasted_iota(jnp.int32, (1, qw), 1)
    tl4 = jnp.bitwise_and(lane, Q_BLOCK - 1)
    tpos4 = i * Q_BLOCK + tl4
    tpos = i * Q_BLOCK + lax.broadcasted_iota(jnp.int32, (1, Q_BLOCK), 1)
    kv_lanes = lambda g: slice(g * HEAD_DIM, (g + 1) * HEAD_DIM)

    def q_transposed(g):
        q = (q0_ref, q1_ref)[g][...]
        return jnp.concatenate([(q[:, h * HEAD_DIM:(h + 1) * HEAD_DIM] * qscale).T for h in range(HPG)],
                               axis=1).astype(BF16)

    qt = [q_transposed(g) for g in groups]

    c_end = lax.broadcasted_iota(jnp.int32, (n_chunk, 1), 0) * CMP_STRIDE + (CMP_LEN - 1)
    mask_c = jnp.logical_and(c_end <= tpos4, c_end < n_chunk * CMP_STRIDE)

    def compressed(g):
        sc = _dot(kcs_ref[g], qt[g])
        m = jnp.max(jnp.where(mask_c, sc, NEG), axis=0, keepdims=True)
        e = jnp.where(mask_c, jnp.exp2(sc - m), 0.0)
        d = jnp.sum(e, axis=0, keepdims=True)
        pc = e * (1.0 / jnp.where(d > 0, d, 1.0))
        imp = pc[:, 0:Q_BLOCK]
        for h in range(1, HPG):
            imp = imp + pc[:, h * Q_BLOCK:(h + 1) * Q_BLOCK]
        return _dot(vcst_ref[g], pc.astype(BF16)), imp

    oc, imps = zip(*[compressed(g) for g in groups])

    jb = lax.broadcasted_iota(jnp.int32, (n_sel, 1), 0)
    blk_valid = jb * SEL_LEN <= tpos
    cur = lax.shift_right_logical(tpos, int(np.log2(SEL_LEN)))
    forced = jnp.logical_or(jb == 0, jnp.logical_and(jb <= cur, jb > cur - N_LOCAL_SEL))

    def select(g):
        imp_ref[g, 0:SUBLANES, :] = jnp.zeros((SUBLANES, Q_BLOCK), F32)
        imp_ref[g, SUBLANES:SUBLANES + n_chunk, :] = imps[g]
        imp_ref[g, SUBLANES + n_chunk:, :] = jnp.zeros((SUBLANES, Q_BLOCK), F32)
        p_slc = None
        for mm in range(SEL_RATIO):
            for nn in range(CMP_RATIO):
                r = imp_ref[g, pl.ds(SUBLANES + mm - nn, n_sel, stride=SEL_RATIO), :]
                p_slc = r if p_slc is None else p_slc + r
        score = jnp.where(blk_valid, jnp.where(forced, jnp.inf, p_slc), -jnp.inf)
        sel = _select_blocks(score, score_ref.at[g])
        selb_ref[g] = jnp.concatenate([jnp.where(sel, 0.0, NEG)] * HPG, axis=1)

    for g in groups:
        select(g)

    row = lax.broadcasted_iota(jnp.int32, (Q_BLOCK, 1), 0)
    causal_bias = jnp.where(row <= tl4, 0.0, NEG)
    init = (jnp.full((1, qw), NEG, F32), jnp.zeros((1, qw), F32), jnp.zeros((HEAD_DIM, qw), F32))
    blocks_per_tile = Q_BLOCK // SEL_LEN

    n_back = WINDOW // Q_BLOCK

    def window(g):
        win_tiles = []
        for w in range(n_back + 1):
            kb = i - n_back + w
            kbc = jnp.maximum(kb, 0)
            rows = pl.ds(pl.multiple_of(kbc * Q_BLOCK, Q_BLOCK), Q_BLOCK)
            s = _dot(kw_ref[rows, kv_lanes(g)].astype(BF16), qt[g])
            if w == n_back:
                s = s + causal_bias
            else:
                before_start = jnp.where(kb >= 0, 0.0, NEG).astype(F32)
                s = s + (jnp.where(row > tl4, before_start, NEG) if w == 0 else before_start)
            win_tiles.append((s, vwt_ref[g, kbc]))
        return _softmax_tiles(win_tiles, init)[1:]

    l_w, a_w = zip(*[window(g) for g in groups])

    n_pair = lax.shift_right_logical(i, 1)
    odd_bias = jnp.where(jnp.bitwise_and(i, 1) == 1, 0.0, NEG).astype(F32)

    def sel_scores(g, kt, tile_bias):
        rows = pl.ds(pl.multiple_of(kt * Q_BLOCK, Q_BLOCK), Q_BLOCK)
        s = _dot(ks_ref[rows, kv_lanes(g)].astype(BF16), qt[g])
        parts = []
        for bb in range(blocks_per_tile):
            bias = selb_ref[g, pl.ds(kt * blocks_per_tile + bb, 1), :] + tile_bias
            parts.append(s[bb * SEL_LEN:(bb + 1) * SEL_LEN] + bias)
        return jnp.concatenate(parts, axis=0)

    def sel_body(u, carry):
        scores = [(sel_scores(g, 2 * u, 0.0), sel_scores(g, 2 * u + 1, 0.0)) for g in groups]
        return tuple(_softmax_tiles([(scores[g][0], vst_ref[g, 2 * u]), (scores[g][1], vst_ref[g, 2 * u + 1])],
                                    carry[g]) for g in groups)

    carry = lax.fori_loop(0, n_pair, sel_body, tuple(init for g in groups))
    prev = jnp.maximum(i - 1, 0)
    last_scores = [(sel_scores(g, prev, odd_bias), sel_scores(g, i, 0.0) + causal_bias) for g in groups]

    gt_ref[...] = jax.nn.sigmoid(gl_ref[...]).T
    for g in groups:
        _, l_s, a_s = _softmax_tiles([(last_scores[g][0], vst_ref[g, prev]), (last_scores[g][1], vst_ref[g, i])],
                                     carry[g])
        gate = lambda c: jnp.concatenate(
            [gt_ref[pl.ds((g * HPG + h) * 3 + c, 1), :] for h in range(HPG)], axis=1)
        ot = gate(0) * oc[g] + gate(1) * (a_s * (1.0 / l_s)) + gate(2) * (a_w[g] * (1.0 / l_w[g]))
        for h in range(HPG):
            col = (g * HPG + h) * HEAD_DIM
            o_ref[:, col:col + HEAD_DIM] = ot[:, h * Q_BLOCK:(h + 1) * Q_BLOCK].T


def _nsa_prompt(z, kcs, vcst, vst, vwt, bsz, t):
    n_chunk = t // CMP_STRIDE
    n_sel = t // SEL_LEN
    n_tile = t // Q_BLOCK
    nqb = t // Q_BLOCK
    qw = HPG * HEAD_DIM
    return pl.pallas_call(
        functools.partial(_nsa_prompt_kernel, n_chunk=n_chunk, n_sel=n_sel),
        grid=(bsz, nqb),
        in_specs=[
            pl.BlockSpec((Q_BLOCK, qw), lambda b, i: (b * nqb + i, OFF_NQ // qw)),
            pl.BlockSpec((Q_BLOCK, qw), lambda b, i: (b * nqb + i, OFF_NQ // qw + 1)),
            pl.BlockSpec((Q_BLOCK, LANES), lambda b, i: (b * nqb + i, OFF_GL // LANES)),
            pl.BlockSpec((None, N_KV, n_chunk, HEAD_DIM), lambda b, i: (b, 0, 0, 0)),
            pl.BlockSpec((None, N_KV, HEAD_DIM, n_chunk), lambda b, i: (b, 0, 0, 0)),
            pl.BlockSpec((t, KV_W), lambda b, i: (b, OFF_KS // KV_W)),
            pl.BlockSpec((None, N_KV, n_tile, HEAD_DIM, HEAD_DIM), lambda b, i: (b, 0, 0, 0, 0)),
            pl.BlockSpec((t, KV_W), lambda b, i: (b, OFF_KW // KV_W)),
            pl.BlockSpec((None, N_KV, n_tile, HEAD_DIM, HEAD_DIM), lambda b, i: (b, 0, 0, 0, 0)),
        ],
        out_specs=pl.BlockSpec((Q_BLOCK, NSA_W), lambda b, i: (b * nqb + i, 0)),
        out_shape=jax.ShapeDtypeStruct((bsz * t, NSA_W), F32),
        scratch_shapes=[
            pltpu.VMEM((N_KV, n_chunk + 2 * SUBLANES, Q_BLOCK), F32),
            pltpu.VMEM((N_KV, n_sel, HPG * Q_BLOCK), F32),
            pltpu.VMEM((LANES, Q_BLOCK), F32),
            pltpu.VMEM((N_KV, n_sel, Q_BLOCK), F32),
        ],
        compiler_params=_cparams(("parallel", "arbitrary")),
        name="nsa_prompt",
    )(z, z, z, kcs, vcst, z, vst, z, vwt)


SAMPLE_TB = 8


def _state_sample_kernel(lg_ref, q_ref, k_ref, v_ref, g_ref, u_ref, sr_ref, sp_ref, wp_ref, ps_ref,
                         or_ref, sn_ref, op_ref, pn_ref, *, pos):
    tb = SAMPLE_TB
    pad = jnp.zeros((SUBLANES - 1, HEAD_DIM), F32)
    for h in range(R_HEADS):
        gamma = jnp.exp(jnp.full((1, HEAD_DIM), lg_ref[h], F32))
        sl = slice(h * HEAD_DIM, (h + 1) * HEAD_DIM)
        rows = []
        for b in range(tb):
            q = q_ref[b:b + 1, sl]
            k = k_ref[b:b + 1, sl]
            v = v_ref[b:b + 1, sl]
            state = sr_ref[b, h]
            k8 = jnp.concatenate([k, pad], axis=0).astype(BF16)
            v8 = jnp.concatenate([v, pad], axis=0).astype(BF16)
            qk = jnp.sum(q * k, axis=1, keepdims=True)
            qs = _dot(jnp.concatenate([q * gamma, pad], axis=0).astype(BF16), state.astype(BF16))[0:1]
            rows.append(qk * v + qs)
            sn_ref[b, h] = gamma[:, 0:1] * state + _dot_tn(k8, v8)
        o = jnp.concatenate(rows, axis=0)
        o = o * lax.rsqrt(jnp.mean(o * o, axis=-1, keepdims=True) + EPS)
        or_ref[:, sl] = o * _silu(g_ref[:, sl])
    u = u_ref[...]
    for gi, w in enumerate(POOL_WINDOWS):
        sl = slice(gi * POOL_GW, (gi + 1) * POOL_GW)
        s = u[:, sl] + jnp.sum(sp_ref[:, POOL_BUF - (w - 1):POOL_BUF, sl], axis=1)
        p = s / float(min(pos + 1, w)) - u[:, sl]
        op_ref[:, sl] = _dot(p.astype(BF16), wp_ref[gi].astype(BF16)) * ps_ref[:, sl]
    pn_ref[:, 0:POOL_BUF - 1, :] = sp_ref[:, 1:POOL_BUF, :]
    for b in range(tb):
        pn_ref[b, POOL_BUF - 1:POOL_BUF, :] = u[b:b + 1, :]


def _state_sample(zs, log_g, state_ret, state_pool, w_pool, pool_scale, layer, dbs, pos):
    tb = SAMPLE_TB
    col = lambda off: pl.BlockSpec((tb, RET_W), lambda i, lg: (i, off // RET_W))
    return pl.pallas_call(
        functools.partial(_state_sample_kernel, pos=pos),
        grid_spec=pltpu.PrefetchScalarGridSpec(
            num_scalar_prefetch=1,
            grid=(dbs // tb,),
            in_specs=[
                col(OFF_RQ), col(OFF_RK), col(OFF_RV), col(OFF_RG), col(OFF_PU),
                pl.BlockSpec((None, tb, R_HEADS, HEAD_DIM, HEAD_DIM), lambda i, lg: (layer, i, 0, 0, 0)),
                pl.BlockSpec((None, tb, POOL_BUF, POOL_W), lambda i, lg: (layer, i, 0, 0)),
                pl.BlockSpec((None, len(POOL_WINDOWS), POOL_GW, POOL_GW), lambda i, lg: (layer, 0, 0, 0)),
                pl.BlockSpec((None, 1, POOL_W), lambda i, lg: (layer, 0, 0)),
            ],
            out_specs=[
                pl.BlockSpec((tb, RET_W), lambda i, lg: (i, 0)),
                pl.BlockSpec((tb, R_HEADS, HEAD_DIM, HEAD_DIM), lambda i, lg: (i, 0, 0, 0)),
                pl.BlockSpec((tb, POOL_W), lambda i, lg: (i, 0)),
                pl.BlockSpec((tb, POOL_BUF, POOL_W), lambda i, lg: (i, 0, 0)),
            ],
        ),
        out_shape=[
            jax.ShapeDtypeStruct((dbs, RET_W), F32),
            jax.ShapeDtypeStruct((dbs, R_HEADS, HEAD_DIM, HEAD_DIM), F32),
            jax.ShapeDtypeStruct((dbs, POOL_W), F32),
            jax.ShapeDtypeStruct((dbs, POOL_BUF, POOL_W), F32),
        ],
        compiler_params=_cparams(("parallel",)),
        name="state_sample",
    )(log_g, zs, zs, zs, zs, zs, state_ret, state_pool, w_pool, pool_scale)


N_CACHE = 4


def _softmax_lanes_with_new(s, s_new):
    m = jnp.maximum(jnp.max(s, axis=1, keepdims=True), s_new)
    p = jnp.exp(s - m)
    p_new = jnp.exp(s_new - m)
    inv = 1.0 / (jnp.sum(p, axis=1, keepdims=True) + p_new)
    return p * inv, p_new * inv


def _rank_select(srow):
    across = jnp.broadcast_to(srow, (LANES, LANES))
    down = across.T
    r_i = lax.broadcasted_iota(jnp.int32, (LANES, LANES), 0)
    c_i = lax.broadcasted_iota(jnp.int32, (LANES, LANES), 1)
    beats = jnp.logical_or(down > across, jnp.logical_and(down == across, r_i < c_i))
    rank = jnp.sum(jnp.where(beats, 1.0, 0.0), axis=0, keepdims=True)
    return jnp.logical_and(rank < SEL_TOPK, srow > -jnp.inf)


def _nsa_sample_sequence(q0_ref, q1_ref, ksn_ref, vsn_ref, kwn_ref, vwn_ref, gl_ref, wsel_ref,
                         w1k_ref, w2k_ref, pek_ref, w1v_ref, w2v_ref, pev_ref, wk_ref, wv_ref,
                         ck, cv, sk, sv, o_ref, *, n_pages, past, w_buf):
    scale = HEAD_DIM ** -0.5
    chunks_per_page = PAGE_SIZE // CMP_STRIDE
    n_chunk = n_pages * chunks_per_page
    n_sel = -(-(past + 1) // SEL_LEN)
    assert n_chunk <= LANES and n_sel <= LANES and past % SEL_LEN == 0

    group_rows = lambda ref, g, n: ref[pl.ds(g, n, stride=N_KV), :]

    def chunks_of(prefs):
        return jnp.concatenate(
            [group_rows(p, g, PAGE_SIZE).astype(BF16).reshape(chunks_per_page, CHUNK_W)
             for g in range(N_KV) for p in prefs],
            axis=0)

    ksum = _compress_core(chunks_of(ck), w1k_ref, w2k_ref, pek_ref)
    yield
    vsum = _compress_core(chunks_of(cv), w1v_ref, w2v_ref, pev_ref)
    yield

    lane = lax.broadcasted_iota(jnp.int32, (1, LANES), 1)
    q_refs = (q0_ref, q1_ref)
    zpad = jnp.zeros((SUBLANES - HPG, HEAD_DIM), F32)
    q8, oc, imps = [], [], []
    for g in range(N_KV):
        qg = jnp.concatenate([q_refs[g][:, h * HEAD_DIM:(h + 1) * HEAD_DIM] for h in range(HPG)] + [zpad],
                             axis=0) * scale
        q8.append(qg.astype(BF16))
        kcs = ksum[g * n_chunk:(g + 1) * n_chunk]
        vcs = vsum[g * n_chunk:(g + 1) * n_chunk]
        s_c = _dot_nt(q8[g], kcs.astype(BF16))
        c_end = lane[:, :n_chunk] * CMP_STRIDE + (CMP_LEN - 1)
        mask_c = jnp.logical_and(c_end <= past, c_end < n_chunk * CMP_STRIDE)
        m = jnp.max(jnp.where(mask_c, s_c, NEG), axis=1, keepdims=True)
        e = jnp.where(mask_c, jnp.exp(s_c - m), 0.0)
        d = jnp.sum(e, axis=1, keepdims=True)
        pc = e * (1.0 / jnp.where(d > 0, d, 1.0))
        oc.append(_dot(pc.astype(BF16), vcs.astype(BF16)))
        imps.append(jnp.sum(pc[0:HPG], axis=0, keepdims=True))
    yield

    imp = jnp.concatenate(imps + [jnp.zeros((SUBLANES - N_KV, n_chunk), F32)], axis=0)
    if n_chunk < LANES:
        imp = jnp.concatenate([imp, jnp.zeros((SUBLANES, LANES - n_chunk), F32)], axis=1)
    hi = imp.astype(BF16)
    r1 = imp - hi.astype(F32)
    mid = r1.astype(BF16)
    lo = (r1 - mid.astype(F32)).astype(BF16)
    wsel = wsel_ref[...]
    p_slc = _dot(hi, wsel) + _dot(mid, wsel) + _dot(lo, wsel)
    cur = past // SEL_LEN
    blk_valid = jnp.logical_and(lane * SEL_LEN <= past, lane < n_sel)
    forced = jnp.logical_or(lane == 0, jnp.logical_and(lane <= cur, lane > cur - N_LOCAL_SEL))
    score = jnp.where(blk_valid, jnp.where(forced, jnp.inf, p_slc), -jnp.inf)
    yield

    gate = jax.nn.sigmoid(gl_ref[...])
    blocks_per_page = PAGE_SIZE // SEL_LEN
    wl = lax.broadcasted_iota(jnp.int32, (1, w_buf), 1)
    dpos = w_buf - wl
    bias_w = jnp.where(jnp.logical_and(jnp.logical_and(dpos >= 0, dpos < WINDOW), past - dpos >= 0), 0.0, NEG)
    for g in range(N_KV):
        kv_new = slice(g * HEAD_DIM, (g + 1) * HEAD_DIM)
        q_f = q8[g].astype(F32)
        selb = jnp.where(_rank_select(score[g:g + 1]), 0.0, NEG)
        bias_pages = []
        for p in range(n_pages):
            bias = selb[:, p * blocks_per_page:p * blocks_per_page + 1]
            for bb in range(1, blocks_per_page):
                bias = jnp.where(lane < bb * SEL_LEN, bias, selb[:, p * blocks_per_page + bb:p * blocks_per_page + bb + 1])
            bias_pages.append(bias)
        k_all = jnp.concatenate([group_rows(p, g, PAGE_SIZE) for p in sk], axis=0).astype(BF16)
        v_all = jnp.concatenate([group_rows(p, g, PAGE_SIZE) for p in sv], axis=0).astype(BF16)
        s_all = _dot_nt(q8[g], k_all) + jnp.concatenate(bias_pages, axis=1)
        s_new = jnp.sum(q_f * ksn_ref[:, kv_new].astype(BF16).astype(F32), axis=1, keepdims=True)
        p_all, p_new = _softmax_lanes_with_new(s_all, s_new)
        o_s = p_new * vsn_ref[:, kv_new] + _dot(p_all.astype(BF16), v_all)
        yield
        s_w = _dot_nt(q8[g], group_rows(wk_ref, g, w_buf).astype(BF16)) + bias_w
        s_wn = jnp.sum(q_f * kwn_ref[:, kv_new].astype(BF16).astype(F32), axis=1, keepdims=True)
        p_w, p_wn = _softmax_lanes_with_new(s_w, s_wn)
        o_w = p_wn * vwn_ref[:, kv_new] + _dot(p_w.astype(BF16), group_rows(wv_ref, g, w_buf).astype(BF16))
        for h in range(HPG):
            c0 = (g * HPG + h) * 3
            og = (gate[:, c0:c0 + 1] * oc[g][h:h + 1] + gate[:, c0 + 1:c0 + 2] * o_s[h:h + 1]
                  + gate[:, c0 + 2:c0 + 3] * o_w[h:h + 1])
            o_ref[:, (g * HPG + h) * HEAD_DIM:(g * HPG + h + 1) * HEAD_DIM] = og


SAMPLE_SEQS = 1


def _nsa_sample_kernel(pt_ref, *refs, n_pages, past, w_buf, layer, n_older):
    blocked, rest = refs[:16], refs[16:]
    caches = rest[:N_CACHE]
    older = rest[N_CACHE:N_CACHE + 4 * n_older]
    outs = rest[N_CACHE + 4 * n_older:-2]
    o_ref = outs[0]
    pages_buf, sem = rest[-2:]
    b = pl.program_id(0)

    def page_copies(step, slot):
        return [pltpu.make_async_copy(caches[c].at[layer, pt_ref[(step * SAMPLE_SEQS + sq) * n_pages + p]],
                                      pages_buf.at[slot, sq, c, p], sem.at[slot])
                for sq in range(SAMPLE_SEQS) for c in range(N_CACHE) for p in range(n_pages)]

    @pl.when(b == 0)
    def _():
        for k, cp in enumerate(page_copies(0, 0)):
            cp.start(priority=k % 2)

    @pl.when(b + 1 < pl.num_programs(0))
    def _():
        for k, cp in enumerate(page_copies(b + 1, jnp.bitwise_and(b + 1, 1))):
            cp.start(priority=k % 2)

    slot = jnp.bitwise_and(b, 1)
    for cp in page_copies(b, slot):
        cp.wait()
    per_seq = (0, 1, 2, 3, 4, 5, 6, 14, 15)
    stages = []
    for sq in range(SAMPLE_SEQS):
        ins = [r.at[sq] if k in per_seq else r for k, r in enumerate(blocked)]
        pages = [[pages_buf.at[slot, sq, c, p] for p in range(n_pages)] for c in range(N_CACHE)]
        stages.append(_nsa_sample_sequence(*ins, *pages, o_ref.at[sq], n_pages=n_pages, past=past, w_buf=w_buf))
    for _ in itertools.zip_longest(*stages):
        pass

    if len(outs) > 1:
        wko_ref, wvo_ref = outs[1:]
        shifts = [(layer, blocked[4], blocked[5], blocked[14], blocked[15])]
        shifts += [(lo,) + tuple(older[4 * lo:4 * lo + 4]) for lo in range(n_older)]
        keep = N_KV * (w_buf - 1)
        for lo, kn, vn, wk, wv in shifts:
            for sq in range(SAMPLE_SEQS):
                wko_ref[lo, sq, 0:keep, :] = wk[sq, N_KV:N_KV * w_buf, :]
                wvo_ref[lo, sq, 0:keep, :] = wv[sq, N_KV:N_KV * w_buf, :]
                for g in range(N_KV):
                    wko_ref[lo, sq, keep + g:keep + g + 1, :] = kn[sq, :, g * HEAD_DIM:(g + 1) * HEAD_DIM]
                    wvo_ref[lo, sq, keep + g:keep + g + 1, :] = vn[sq, :, g * HEAD_DIM:(g + 1) * HEAD_DIM]


def _nsa_sample(zs3, page_table, cmp_pages, slc_pages, win_k, win_v, cw, wsel, layer, dbs, past, w_buf, older_zs3):
    n_pages = past // PAGE_SIZE
    depth = win_k.shape[0]
    n_older = 0 if older_zs3 is None else len(older_zs3)
    assert older_zs3 is None or n_older == layer == depth - 1
    assert dbs % SAMPLE_SEQS == 0
    nsq = SAMPLE_SEQS
    rowblk = lambda width, off: pl.BlockSpec((nsq, 1, width), lambda b, pt: (b, 0, off // width))
    const3 = lambda shp: pl.BlockSpec((None,) + shp, lambda b, pt: (layer,) + (0,) * len(shp))
    page_shape = (PAGE_SIZE * N_KV, HEAD_DIM)
    win_shape = (w_buf * N_KV, HEAD_DIM)
    win = pl.BlockSpec((None, nsq) + win_shape, lambda b, pt: (layer, b, 0, 0))
    hbm = pl.BlockSpec(memory_space=pl.ANY)
    in_specs = [
        rowblk(HPG * HEAD_DIM, OFF_NQ), rowblk(HPG * HEAD_DIM, OFF_NQ + HPG * HEAD_DIM),
        rowblk(KV_W, OFF_KS), rowblk(KV_W, OFF_VS), rowblk(KV_W, OFF_KW), rowblk(KV_W, OFF_VW),
        rowblk(LANES, OFF_GL),
        pl.BlockSpec((LANES, LANES), lambda b, pt: (0, 0)),
        const3((CHUNK_W, 2 * HEAD_DIM)), const3((HEAD_DIM, HEAD_DIM)), const3((PE_ROWS, CHUNK_W)),
        const3((CHUNK_W, 2 * HEAD_DIM)), const3((HEAD_DIM, HEAD_DIM)), const3((PE_ROWS, CHUNK_W)),
        win, win,
    ] + [hbm] * N_CACHE
    operands = [zs3] * 7 + [wsel, cw["w1k"], cw["w2k"], cw["pek"], cw["w1v"], cw["w2v"], cw["pev"], win_k, win_v]
    operands += list(cmp_pages) + list(slc_pages)
    out_specs = [pl.BlockSpec((nsq, 1, NSA_W), lambda b, pt: (b, 0, 0))]
    out_shape = [jax.ShapeDtypeStruct((dbs, 1, NSA_W), F32)]
    if older_zs3 is not None:
        for lo in range(n_older):
            win_lo = pl.BlockSpec((None, nsq) + win_shape, lambda b, pt, lo=lo: (lo, b, 0, 0))
            operands += [older_zs3[lo], older_zs3[lo], win_k, win_v]
            in_specs += [rowblk(KV_W, OFF_KW), rowblk(KV_W, OFF_VW), win_lo, win_lo]
        out_specs += [pl.BlockSpec((depth, nsq) + win_shape, lambda b, pt: (0, b, 0, 0))] * 2
        out_shape += [jax.ShapeDtypeStruct((depth, dbs) + win_shape, F32)] * 2
    return pl.pallas_call(
        functools.partial(_nsa_sample_kernel, n_pages=n_pages, past=past, w_buf=w_buf, layer=layer,
                          n_older=n_older),
        grid_spec=pltpu.PrefetchScalarGridSpec(
            num_scalar_prefetch=1,
            grid=(dbs // nsq,),
            in_specs=in_specs,
            out_specs=out_specs,
            scratch_shapes=[pltpu.VMEM((2, nsq, N_CACHE, n_pages) + page_shape, F32),
                            pltpu.SemaphoreType.DMA((2,))],
        ),
        out_shape=out_shape,
        compiler_params=_cparams(("arbitrary",), VMEM_LIMIT_BIG),
        name="nsa_sample",
    )(page_table.reshape(-1), *operands)


def _rope_tables(pos):
    half = HEAD_DIM // 2
    inv = ROPE_THETA ** (-jnp.arange(half, dtype=F32) / half)
    ang = pos.astype(F32)[:, None] * inv[None, :]
    cos, sin = jnp.cos(ang), jnp.sin(ang)
    return jnp.concatenate([cos, cos], axis=-1), jnp.concatenate([-sin, sin], axis=-1)


def _block_sum_matrix():
    w = np.zeros((LANES, LANES), np.float32)
    for j in range(LANES):
        for mm in range(SEL_RATIO):
            for nn in range(CMP_RATIO):
                c = SEL_RATIO * j + mm - nn
                if 0 <= c < LANES:
                    w[c, j] += 1.0
    return w


def _row_tile(m, cap):
    tm = min(m, cap)
    while m % tm:
        tm //= 2
    return tm


def kernel(x_prompt, x_sample, state_ret, state_pool, cache_cmp_k, cache_cmp_v, cache_slc_k, cache_slc_v,
           cache_win_k, cache_win_v, page_table, norm1, w_in, w_pool, pool_scale, cmp_w1_k, cmp_w2_k, cmp_pe_k,
           cmp_w1_v, cmp_w2_v, cmp_pe_v, w_out, norm2, w_up, w_down, norm_f):
    bsz, t, _ = x_prompt.shape
    dbs, dec_seq, _ = x_sample.shape
    depth = w_in.shape[0]
    assert dec_seq == 1 and t % 512 == 0 and dbs % SAMPLE_TB == 0
    n_pages = page_table.shape[1]
    past = n_pages * PAGE_SIZE
    n_pool = cache_cmp_k.shape[1]
    w_buf = cache_win_k.shape[2]
    wlen = min(WINDOW, t)

    w_in_bf = jnp.pad(w_in, ((0, 0), (0, 0), (0, N_PAD - D_IN))).astype(BF16)
    w_out_bf = w_out.astype(BF16)
    w1cat = lambda w1: jnp.concatenate([w1[:, :CMP_STRIDE], w1[:, CMP_STRIDE:]], axis=-1).astype(BF16).reshape(
        depth, CHUNK_W, 2 * HEAD_DIM)
    pe_rows = lambda pe: jnp.pad(pe.reshape(depth, CMP_RATIO, CHUNK_W),
                                 ((0, 0), (0, PE_ROWS - CMP_RATIO), (0, 0))).astype(BF16)
    cw = dict(w1k=w1cat(cmp_w1_k), w2k=cmp_w2_k.astype(BF16), pek=pe_rows(cmp_pe_k),
              w1v=w1cat(cmp_w1_v), w2v=cmp_w2_v.astype(BF16), pev=pe_rows(cmp_pe_v))
    norm1_3, norm2_3 = norm1[:, None, :], norm2[:, None, :]
    pool_scale_3 = pool_scale[:, None, :]
    norm_f2 = norm_f[None, :]
    log_g = jnp.log1p(-jnp.exp2(-5.0 - jnp.arange(R_HEADS, dtype=F32)))
    cos_p, sin_p = _rope_tables(jnp.arange(t))
    cos_s, sin_s = _rope_tables(jnp.full((dbs,), past, jnp.int32))
    wsel = jnp.asarray(_block_sum_matrix(), BF16)
    page_view = lambda c: c.reshape(depth, n_pool, PAGE_SIZE * N_KV, HEAD_DIM)
    cmp_pages = [page_view(c) for c in (cache_cmp_k, cache_cmp_v)]
    slc_pages = [page_view(c) for c in (cache_slc_k, cache_slc_v)]
    win_k = cache_win_k.reshape(depth, dbs, w_buf * N_KV, HEAD_DIM)
    win_v = cache_win_v.reshape(depth, dbs, w_buf * N_KV, HEAD_DIM)

    mp = bsz * t
    tm_p = _row_tile(t, 1024)
    tm_mlp = _row_tile(t, 1024)
    xp = x_prompt.reshape(mp, D_MODEL)
    xs = x_sample.reshape(dbs, D_MODEL)
    outs = {k: [] for k in ("ret_p", "ret_s", "pool_p", "pool_s", "ck_p", "ck_s", "cv_p", "cv_s", "sk_p", "sk_s",
                            "sv_p", "sv_s", "wk_p", "wv_p")}
    zs3_all = []
    for l in range(depth):
        last = l == depth - 1
        z = _inproj(xp, norm1_3, w_in_bf, l, cos_p, sin_p, tm_p)
        o_r, s_new = _ret_prompt(z, log_g, bsz, t)
        o_p = _pool_prompt(z, w_pool, pool_scale_3, l, bsz, t)
        kcs, vcst, vst, vwt = _prep_prompt(z, cw, l, bsz, t)
        o_n = _nsa_prompt(z, kcs, vcst, vst, vwt, bsz, t)
        xp = _outproj(o_r, o_p, o_n, w_out_bf, l, xp, _row_tile(t, 512))
        xp = _mlp(xp, norm2_3, w_up, w_down, l, norm_f2, last, tm_mlp)
        z3 = z.reshape(bsz, t, N_PAD)
        kv = lambda off: z3[:, :, off:off + KV_W].reshape(bsz, t, N_KV, HEAD_DIM)
        outs["ret_p"].append(s_new)
        outs["pool_p"].append(z3[:, t - POOL_BUF:, OFF_PU:OFF_PU + POOL_W])
        outs["ck_p"].append(kv(OFF_KC))
        outs["cv_p"].append(kv(OFF_VC))
        outs["sk_p"].append(kv(OFF_KS))
        outs["sv_p"].append(kv(OFF_VS))
        outs["wk_p"].append(kv(OFF_KW)[:, t - wlen:])
        outs["wv_p"].append(kv(OFF_VW)[:, t - wlen:])
        zs = _inproj(xs, norm1_3, w_in_bf, l, cos_s, sin_s, dbs)
        o_r, s_new, o_p, buf_new = _state_sample(zs, log_g, state_ret, state_pool, w_pool, pool_scale_3, l, dbs, past)
        zs3 = zs.reshape(dbs, 1, N_PAD)
        res = _nsa_sample(zs3, page_table, cmp_pages, slc_pages, win_k, win_v, cw, wsel, l, dbs, past, w_buf,
                          zs3_all if last else None)
        o_n = res[0]
        zs3_all.append(zs3)
        if last:
            wk_s, wv_s = res[1:]
        xs = _outproj(o_r, o_p, o_n.reshape(dbs, NSA_W), w_out_bf, l, xs, dbs)
        xs = _mlp(xs, norm2_3, w_up, w_down, l, norm_f2, last, dbs)
        kvs = lambda off: zs[:, off:off + KV_W].reshape(dbs, 1, N_KV, HEAD_DIM)
        outs["ret_s"].append(s_new)
        outs["pool_s"].append(buf_new)
        outs["ck_s"].append(kvs(OFF_KC))
        outs["cv_s"].append(kvs(OFF_VC))
        outs["sk_s"].append(kvs(OFF_KS))
        outs["sv_s"].append(kvs(OFF_VS))
    st = lambda k: jnp.stack(outs[k])
    win_out = lambda w: w.reshape(depth, dbs, w_buf, N_KV, HEAD_DIM)
    return (xp.reshape(bsz, t, D_MODEL), xs.reshape(dbs, 1, D_MODEL),
            st("ret_p"), st("ret_s"), st("pool_p"), st("pool_s"),
            st("ck_p"), st("ck_s"), st("cv_p"), st("cv_s"), st("sk_p"), st("sk_s"), st("sv_p"), st("sv_s"),
            st("wk_p"), win_out(wk_s), st("wv_p"), win_out(wv_s))
```
